```python
import math
import jax, jax.numpy as jnp
from jax import lax
import numpy as np

D_MODEL = 4096
BATCH = 1
SEQ = 8192
DEPTH = 1

CTX_LEN = 256
GRID_W = 64
CHUNK_ROWS = 2
CHUNK = CHUNK_ROWS * GRID_W
D_MIX = D_MODEL
GM_WIDTH = 2048
GM_HEADS = 8
GM_HEAD_DIM = GM_WIDTH // GM_HEADS
SSM_WIDTH = 2048
SSM_HEAD_DIM = 64
SSM_HEADS = SSM_WIDTH // SSM_HEAD_DIM
SSM_GROUPS = 8
SSM_STATE = 128
SSM_CONV = 5
SSM_XBC = SSM_WIDTH + 2 * SSM_GROUPS * SSM_STATE
SSD_CHUNK = 128
IN_COLS = 2 * GM_WIDTH + SSM_WIDTH + SSM_XBC + 2 * SSM_HEADS
N_EXPERTS = 16
EXPERT_FF = 2048
EC_FACTOR = 2
ALPHA = (2 * DEPTH) ** 0.25
BETA = (8 * DEPTH) ** -0.25
LN_EPS = 1e-5
RMS_EPS = 1e-5

kernel_name = "hybrid_gmlp_ssd_ec_moe_deepnorm"


def ln_plain(x):
    xf = x.astype(jnp.float32)
    mu = jnp.mean(xf, -1, keepdims=True)
    var = jnp.mean(jnp.square(xf - mu), -1, keepdims=True)
    return ((xf - mu) * lax.rsqrt(var + LN_EPS)).astype(x.dtype)


def ln_affine(x, gain, bias):
    xf = x.astype(jnp.float32)
    mu = jnp.mean(xf, -1, keepdims=True)
    var = jnp.mean(jnp.square(xf - mu), -1, keepdims=True)
    y = (xf - mu) * lax.rsqrt(var + LN_EPS) * gain.astype(jnp.float32) + bias.astype(jnp.float32)
    return y.astype(x.dtype)


def modulate(x, shift, scale):
    return ln_plain(x) * (1 + scale) + shift


def ada_params(cond, w_ada, b_ada):
    m = jax.nn.silu(cond) @ w_ada + b_ada
    m = m.reshape(m.shape[:-1] + (1, 6, D_MODEL))
    return tuple(m[..., i, :] for i in range(6))


def in_proj(h, w_in):
    proj = h @ w_in
    o1 = 2 * GM_WIDTH
    o2 = o1 + SSM_WIDTH
    o3 = o2 + SSM_XBC
    return jnp.split(proj, [o1, o2, o3], axis=-1)


def chunk_mlp(gm_uv, n_chunks, v_gain, v_bias, w_s, b_s):
    b, L, _ = gm_uv.shape
    u, v = jnp.split(jax.nn.gelu(gm_uv), 2, axis=-1)
    v = ln_affine(v, v_gain, v_bias).reshape(b, n_chunks, CHUNK, GM_HEADS, GM_HEAD_DIM)
    mixed = jnp.einsum('hts,bcshd->bcthd', w_s, v) + b_s.T[:, :, None]
    return u * mixed.reshape(b, L, GM_WIDTH)


def depthwise_conv(x, w, bias):
    pad = (SSM_CONV - 1) // 2
    y = lax.conv_general_dilated(x, w[:, None, :], window_strides=(1,), padding=[(pad, pad)],
                                 dimension_numbers=('NWC', 'WIO', 'NWC'),
                                 feature_group_count=x.shape[-1])
    return y + bias


def ssm_front(xbc, dt_raw, conv_w, conv_b, dt_bias):
    b, L, _ = xbc.shape
    xbc = jax.nn.silu(depthwise_conv(xbc, conv_w, conv_b))
    xs, B, C = jnp.split(xbc, [SSM_WIDTH, SSM_WIDTH + SSM_GROUPS * SSM_STATE], axis=-1)
    xs = xs.reshape(b, L, SSM_HEADS, SSM_HEAD_DIM)
    B = B.reshape(b, L, SSM_GROUPS, SSM_STATE)
    C = C.reshape(b, L, SSM_GROUPS, SSM_STATE)
    dt = jax.nn.softplus(dt_raw.astype(jnp.float32).reshape(b, L, 2, SSM_HEADS) + dt_bias)
    return xs, B, C, dt


def segsum_exp(dA):
    cs = jnp.cumsum(dA, -1)
    diff = cs[..., :, None] - cs[..., None, :]
    l = dA.shape[-1]
    mask = jnp.tril(jnp.ones((l, l), dtype=bool))
    return jnp.exp(jnp.where(mask, diff, -jnp.inf))


def ssd_scan(xs, dt, A, B, C, h0):
    b, L, H, P = xs.shape
    nc = L // SSD_CHUNK
    rep = H // B.shape[2]
    X = (xs.astype(jnp.float32) * dt[..., None]).reshape(b, nc, SSD_CHUNK, H, P)
    dA = (dt * A).reshape(b, nc, SSD_CHUNK, H).transpose(0, 3, 1, 2)
    Bh = jnp.repeat(B.astype(jnp.float32), rep, axis=2).reshape(b, nc, SSD_CHUNK, H, SSM_STATE)
    Ch = jnp.repeat(C.astype(jnp.float32), rep, axis=2).reshape(b, nc, SSD_CHUNK, H, SSM_STATE)
    cs = jnp.cumsum(dA, -1)
    scores = jnp.einsum('bclhn,bcshn->bhcls', Ch, Bh) * segsum_exp(dA)
    y_diag = jnp.einsum('bhcls,bcshp->bclhp', scores, X)
    decay_states = jnp.exp(cs[..., -1:] - cs)
    states = jnp.einsum('bclhn,bhcl,bclhp->bchpn', Bh, decay_states, X)
    chunk_decay = jnp.exp(cs[..., -1])

    def step(h, inp):
        s_c, d_c = inp
        return h * d_c[..., None, None] + s_c, h

    h_final, h_in = lax.scan(step, h0, (states.transpose(1, 0, 2, 3, 4), chunk_decay.transpose(2, 0, 1)))
    y_off = jnp.einsum('bclhn,cbhpn,bhcl->bclhp', Ch, h_in, jnp.exp(cs))
    return (y_diag + y_off).reshape(b, L, H, P), h_final


def bidir_ssd(xs, dt2, A2, B, C, h0_f, h0_b):
    y_f, h_f = ssd_scan(xs, dt2[:, :, 0], A2[0], B, C, h0_f)
    fl = lambda a: jnp.flip(a, 1)
    y_b, h_b = ssd_scan(fl(xs), fl(dt2[:, :, 1]), A2[1], fl(B), fl(C), h0_b)
    return y_f + fl(y_b), h_f, h_b


def ssm_output(y, xs, z, d_skip, norm_gain):
    b, L, H, P = xs.shape
    y = (y + d_skip[:, None] * xs.astype(jnp.float32)).reshape(b, L, SSM_WIDTH)
    g = (y * jax.nn.silu(z.astype(jnp.float32))).reshape(b, L, SSM_GROUPS, SSM_WIDTH // SSM_GROUPS)
    g = g * lax.rsqrt(jnp.mean(g * g, -1, keepdims=True) + RMS_EPS)
    return (g.reshape(b, L, SSM_WIDTH) * norm_gain.astype(jnp.float32)).astype(z.dtype)


def mixer_merge(gm_uv, n_chunks, y_ssd, xs, z, gm_v_gain, gm_v_bias, gm_w_s, gm_b_s,
                ssm_d, ssm_norm_gain, w_out):
    gm = chunk_mlp(gm_uv, n_chunks, gm_v_gain, gm_v_bias, gm_w_s, gm_b_s)
    ss = ssm_output(y_ssd, xs, z, ssm_d, ssm_norm_gain)
    return jnp.concatenate([gm, ss], axis=-1) @ w_out


def expert_choice_ffn(h, w_router, w_gate, w_up, w_down):
    b, n, _ = h.shape
    cap = EC_FACTOR * n // N_EXPERTS
    aff = jax.nn.softmax((h @ w_router).astype(jnp.float32), axis=-1)
    gates, idx = lax.top_k(aff.transpose(0, 2, 1), cap)
    bidx = jnp.arange(b)[:, None, None]
    xe = h[bidx, idx]
    a = jnp.einsum('becd,edf->becf', xe, w_gate)
    u = jnp.einsum('becd,edf->becf', xe, w_up)
    ye = jnp.einsum('becf,efd->becd', jax.nn.silu(a) * u, w_down) * gates[..., None].astype(h.dtype)
    return jnp.zeros_like(h).at[bidx, idx].add(ye)


def setup_inputs(seed: int = 0) -> dict:
    key = jax.random.key(seed)
    ks = jax.random.split(key, 28)
    f32 = jnp.float32
    nrm = lambda k, shape, s: jax.random.normal(k, shape, f32) * s
    D = D_MODEL
    x = nrm(ks[0], (BATCH, SEQ, D), 1.0)
    c = nrm(ks[1], (BATCH, D), 1.0)
    ctx = nrm(ks[2], (BATCH, CTX_LEN, D), 1.0)
    c_ctx = nrm(ks[3], (D,), 1.0)
    w_ada = nrm(ks[4], (DEPTH, D, 6 * D), 0.5 * D ** -0.5)
    b_ada = nrm(ks[5], (DEPTH, 6 * D), 0.02)
    w_in = nrm(ks[6], (DEPTH, D, IN_COLS), D ** -0.5)
    gm_v_gain = 1.0 + nrm(ks[7], (DEPTH, GM_WIDTH), 0.02)
    gm_v_bias = nrm(ks[8], (DEPTH, GM_WIDTH), 0.02)
    gm_w_s = nrm(ks[9], (DEPTH, GM_HEADS, CHUNK, CHUNK), CHUNK ** -0.5)
    gm_b_s = 1.0 + nrm(ks[10], (DEPTH, GM_HEADS, CHUNK), 0.02)
    ssm_conv_w = nrm(ks[11], (DEPTH, SSM_CONV, SSM_XBC), SSM_CONV ** -0.5)
    ssm_conv_b = nrm(ks[12], (DEPTH, SSM_XBC), 0.02)
    dt_init = jnp.exp(jax.random.uniform(ks[13], (DEPTH, 2, SSM_HEADS), f32, math.log(1e-3), math.log(1e-1)))
    ssm_dt_bias = dt_init + jnp.log(-jnp.expm1(-dt_init))
    ssm_a_log = jnp.log(jax.random.uniform(ks[14], (DEPTH, 2, SSM_HEADS), f32, 1.0, 16.0))
    ssm_d = 1.0 + nrm(ks[15], (DEPTH, SSM_HEADS), 0.02)
    ssm_norm_gain = 1.0 + nrm(ks[16], (DEPTH, SSM_WIDTH), 0.02)
    w_out = nrm(ks[17], (DEPTH, D_MIX, D), BETA * D_MIX ** -0.5)
    ln_mix_gain = 1.0 + nrm(ks[18], (DEPTH, D), 0.02)
    ln_mix_bias = nrm(ks[19], (DEPTH, D), 0.02)
    w_router = nrm(ks[20], (DEPTH, D, N_EXPERTS), D ** -0.5)
    w_gate = nrm(ks[21], (DEPTH, N_EXPERTS, D, EXPERT_FF), D ** -0.5)
    w_up = nrm(ks[22], (DEPTH, N_EXPERTS, D, EXPERT_FF), D ** -0.5)
    w_down = nrm(ks[23], (DEPTH, N_EXPERTS, EXPERT_FF, D), BETA * EXPERT_FF ** -0.5)
    ln_ffn_gain = 1.0 + nrm(ks[24], (DEPTH, D), 0.02)
    ln_ffn_bias = nrm(ks[25], (DEPTH, D), 0.02)
    return {"x": x, "c": c, "ctx": ctx, "c_ctx": c_ctx, "w_ada": w_ada, "b_ada": b_ada,
            "w_in": w_in, "gm_v_gain": gm_v_gain, "gm_v_bias": gm_v_bias, "gm_w_s": gm_w_s,
            "gm_b_s": gm_b_s, "ssm_conv_w": ssm_conv_w, "ssm_conv_b": ssm_conv_b,
            "ssm_dt_bias": ssm_dt_bias, "ssm_a_log": ssm_a_log, "ssm_d": ssm_d,
            "ssm_norm_gain": ssm_norm_gain, "w_out": w_out, "ln_mix_gain": ln_mix_gain,
            "ln_mix_bias": ln_mix_bias, "w_router": w_router, "w_gate": w_gate, "w_up": w_up,
            "w_down": w_down, "ln_ffn_gain": ln_ffn_gain, "ln_ffn_bias": ln_ffn_bias}


def reference(x, c, ctx, c_ctx, w_ada, b_ada, w_in, gm_v_gain, gm_v_bias, gm_w_s, gm_b_s,
              ssm_conv_w, ssm_conv_b, ssm_dt_bias, ssm_a_log, ssm_d, ssm_norm_gain, w_out,
              ln_mix_gain, ln_mix_bias, w_router, w_gate, w_up, w_down, ln_ffn_gain, ln_ffn_bias):
    b, L, _ = x.shape
    rows = L // GRID_W
    lat_chunks = rows // CHUNK_ROWS
    ctx_chunks = ctx.shape[1] // CHUNK
    for layer in range(DEPTH):
        sh_m, sc_m, g_m, sh_f, sc_f, g_f = ada_params(c, w_ada[layer], b_ada[layer])
        csh_m, csc_m, cg_m, csh_f, csc_f, cg_f = ada_params(c_ctx, w_ada[layer], b_ada[layer])
        A2 = -jnp.exp(ssm_a_log[layer].astype(jnp.float32))

        hc = modulate(ctx, csh_m, csc_m)
        gm_uv_c, z_c, xbc_c, dt_raw_c = in_proj(hc, w_in[layer])
        xs_c, B_c, C_c, dt_c = ssm_front(xbc_c, dt_raw_c, ssm_conv_w[layer], ssm_conv_b[layer], ssm_dt_bias[layer])
        h_zero = jnp.zeros((b, SSM_HEADS, SSM_HEAD_DIM, SSM_STATE), jnp.float32)
        y_c, hf_ctx, hb_ctx = bidir_ssd(xs_c, dt_c, A2, B_c, C_c, h_zero, h_zero)

        h = modulate(x, sh_m, sc_m)
        gm_uv, z, xbc, dt_raw = in_proj(h, w_in[layer])
        xs, B, C, dt = ssm_front(xbc, dt_raw, ssm_conv_w[layer], ssm_conv_b[layer], ssm_dt_bias[layer])
        y_l, _, _ = bidir_ssd(xs, dt, A2, B, C, hf_ctx, hb_ctx)
        mix = mixer_merge(gm_uv, lat_chunks, y_l, xs, z, gm_v_gain[layer], gm_v_bias[layer],
                          gm_w_s[layer], gm_b_s[layer], ssm_d[layer], ssm_norm_gain[layer], w_out[layer])
        x_new = ln_affine(ALPHA * x + g_m * mix, ln_mix_gain[layer], ln_mix_bias[layer])
        hf = modulate(x_new, sh_f, sc_f)
        moe = expert_choice_ffn(hf, w_router[layer], w_gate[layer], w_up[layer], w_down[layer])
        x_new = ln_affine(ALPHA * x_new + g_f * moe, ln_ffn_gain[layer], ln_ffn_bias[layer])

        if layer < DEPTH - 1:
            mix_c = mixer_merge(gm_uv_c, ctx_chunks, y_c, xs_c, z_c, gm_v_gain[layer], gm_v_bias[layer],
                                gm_w_s[layer], gm_b_s[layer], ssm_d[layer], ssm_norm_gain[layer], w_out[layer])
            ctx = ln_affine(ALPHA * ctx + cg_m * mix_c, ln_mix_gain[layer], ln_mix_bias[layer])
            hcf = modulate(ctx, csh_f, csc_f)
            moe_c = expert_choice_ffn(hcf, w_router[layer], w_gate[layer], w_up[layer], w_down[layer])
            ctx = ln_affine(ALPHA * ctx + cg_f * moe_c, ln_ffn_gain[layer], ln_ffn_bias[layer])
        x = x_new
    return x
```

```python
import functools

import jax
import jax.numpy as jnp
from jax import lax
from jax.experimental import pallas as pl
from jax.experimental.pallas import tpu as pltpu

F32 = jnp.float32
BF16 = jnp.bfloat16
I32 = jnp.int32
U32 = jnp.uint32

D_MODEL = 4096
GRID_W = 64
CHUNK = 128
GM_WIDTH = 2048
GM_HEADS = 8
GM_HEAD_DIM = GM_WIDTH // GM_HEADS
SSM_WIDTH = 2048
SSM_HEAD_DIM = 64
SSM_HEADS = SSM_WIDTH // SSM_HEAD_DIM
SSM_GROUPS = 8
SSM_STATE = 128
SSM_CONV = 5
SSM_XBC = SSM_WIDTH + 2 * SSM_GROUPS * SSM_STATE
SSD_CHUNK = 128
HEADS_PER_GROUP = SSM_HEADS // SSM_GROUPS
GROUP_WIDTH = SSM_WIDTH // SSM_GROUPS
N_EXPERTS = 16
EXPERT_FF = 2048
EC_FACTOR = 2
DEPTH = 1
ALPHA = (2 * DEPTH) ** 0.25
LN_EPS = 1e-5
RMS_EPS = 1e-5

COL_Z = 2 * GM_WIDTH
COL_XBC = COL_Z + SSM_WIDTH
COL_DT = COL_XBC + SSM_XBC

V7X_LANES = 128
V7X_SUBLANES = 8
V7X_VMEM_BYTES = 64 * 1024 * 1024
MIB = 1024 * 1024

ROW_STRIP = 32
SLOT_TILE = 128


def _cparams(semantics, vmem_mib):
    return pltpu.CompilerParams(dimension_semantics=semantics,
                                vmem_limit_bytes=min(vmem_mib * MIB, V7X_VMEM_BYTES - 2 * MIB))


def _ln_rows(x):
    mu = jnp.mean(x, axis=-1, keepdims=True)
    xc = x - mu
    var = jnp.mean(xc * xc, axis=-1, keepdims=True)
    return xc * lax.rsqrt(var + LN_EPS)


def _ada_kernel(c_ref, w_ref, b_ref, o_ref):
    s = jax.nn.silu(c_ref[...]).astype(BF16)
    o_ref[...] = jnp.dot(s, w_ref[...].astype(BF16), preferred_element_type=F32) + b_ref[...]


def _ada_params(cond, w_ada, b_ada):
    k, n = w_ada.shape
    tn = 512
    return pl.pallas_call(
        _ada_kernel,
        out_shape=jax.ShapeDtypeStruct((V7X_SUBLANES, n), F32),
        grid=(n // tn,),
        in_specs=[pl.BlockSpec((V7X_SUBLANES, k), lambda j: (0, 0)),
                  pl.BlockSpec((k, tn), lambda j: (0, j)),
                  pl.BlockSpec((1, tn), lambda j: (0, j))],
        out_specs=pl.BlockSpec((V7X_SUBLANES, tn), lambda j: (0, j)),
        compiler_params=_cparams(("arbitrary",), 32),
        name="ada_params",
    )(cond, w_ada, b_ada)


def _modin_kernel(x_ref, ctx_ref, p_ref, wdt_ref, bias_ref, alog_ref, h_ref, dtda_ref):
    i = pl.program_id(0)
    shift = p_ref[0, 0:1, :]
    scale1 = 1.0 + p_ref[0, 1:2, :]
    tm = h_ref.shape[0]

    def rows(src_ref):
        def body(r, carry):
            sl = pl.ds(pl.multiple_of(r * ROW_STRIP, ROW_STRIP), ROW_STRIP)
            h_ref[sl, :] = (_ln_rows(src_ref[sl, :]) * scale1 + shift).astype(BF16)
            return carry
        lax.fori_loop(0, tm // ROW_STRIP, body, 0)

    @pl.when(i == 0)
    def _():
        rows(ctx_ref)

    @pl.when(i > 0)
    def _():
        rows(x_ref)

    raw = jnp.dot(h_ref[...], wdt_ref[...], preferred_element_type=F32) + bias_ref[...]
    dt = jnp.maximum(raw, 0.0) + jnp.log1p(jnp.exp(-jnp.abs(raw)))
    dtda_ref[:, 0:V7X_LANES] = dt
    dtda_ref[:, V7X_LANES:2 * V7X_LANES] = dt * (-jnp.exp(alog_ref[...]))


def _modulate_in(x2, ctx2, mods, wdt, dt_bias, a_log):
    l, d = x2.shape
    lc = ctx2.shape[0]
    tm = lc
    n_tiles = (l + lc) // tm
    return pl.pallas_call(
        _modin_kernel,
        out_shape=(jax.ShapeDtypeStruct((l + lc, d), BF16),
                   jax.ShapeDtypeStruct((l + lc, 2 * V7X_LANES), F32)),
        grid=(n_tiles,),
        in_specs=[pl.BlockSpec((tm, d), lambda i: (jnp.maximum(i - 1, 0), 0)),
                  pl.BlockSpec((tm, d), lambda i: (0, 0)),
                  pl.BlockSpec((1, V7X_SUBLANES, d), lambda i: (jnp.where(i == 0, 1, 0), 0, 0)),
                  pl.BlockSpec((d, V7X_LANES), lambda i: (0, 0)),
                  pl.BlockSpec((1, V7X_LANES), lambda i: (0, 0)),
                  pl.BlockSpec((1, V7X_LANES), lambda i: (0, 0))],
        out_specs=(pl.BlockSpec((tm, d), lambda i: (i, 0)),
                   pl.BlockSpec((tm, 2 * V7X_LANES), lambda i: (i, 0))),
        compiler_params=_cparams(("arbitrary",), 40),
        name="modulate_in",
    )(x2, ctx2, mods, wdt, dt_bias, a_log)


def _mm_kernel(a_ref, w_ref, o_ref, wbf_ref):
    @pl.when(pl.program_id(1) == 0)
    def _():
        wbf_ref[...] = w_ref[...].astype(BF16)

    o_ref[...] = jnp.dot(a_ref[...], wbf_ref[...], preferred_element_type=F32)


def _matmul(a, w, n_out, tm, tn, name):
    m, k = a.shape
    return pl.pallas_call(
        _mm_kernel,
        out_shape=jax.ShapeDtypeStruct((m, n_out), F32),
        grid=(n_out // tn, m // tm),
        in_specs=[pl.BlockSpec((tm, k), lambda j, i: (i, 0)),
                  pl.BlockSpec((k, tn), lambda j, i: (0, j))],
        out_specs=pl.BlockSpec((tm, tn), lambda j, i: (i, j)),
        scratch_shapes=[pltpu.VMEM((k, tn), BF16)],
        compiler_params=_cparams(("arbitrary", "arbitrary"), 56),
        name=name,
    )(a, w)


def _mm2_kernel(a0_ref, a1_ref, w_ref, o_ref, wbf_ref):
    @pl.when(pl.program_id(1) == 0)
    def _():
        wbf_ref[...] = w_ref[...].astype(BF16)

    k0 = a0_ref.shape[1]
    acc = jnp.dot(a0_ref[...], wbf_ref[0:k0, :], preferred_element_type=F32)
    o_ref[...] = acc + jnp.dot(a1_ref[...], wbf_ref[k0:, :], preferred_element_type=F32)


def _matmul_concat(a0, a1, w, tm, tn, name):
    m, k0 = a0.shape
    k1 = a1.shape[1]
    n_out = w.shape[1]
    return pl.pallas_call(
        _mm2_kernel,
        out_shape=jax.ShapeDtypeStruct((m, n_out), F32),
        grid=(n_out // tn, m // tm),
        in_specs=[pl.BlockSpec((tm, k0), lambda j, i: (i, 0)),
                  pl.BlockSpec((tm, k1), lambda j, i: (i, 0)),
                  pl.BlockSpec((k0 + k1, tn), lambda j, i: (0, j))],
        out_specs=pl.BlockSpec((tm, tn), lambda j, i: (i, j)),
        scratch_shapes=[pltpu.VMEM((k0 + k1, tn), BF16)],
        compiler_params=_cparams(("arbitrary", "arbitrary"), 56),
        name=name,
    )(a0, a1, w)


def _conv_kernel(main_ref, prev_ref, next_ref, w_ref, b_ref, o_ref, ext_ref):
    i = pl.program_id(0)
    rb = main_ref.shape[0]
    halo = V7X_SUBLANES
    pad = (SSM_CONV - 1) // 2
    seq_start = jnp.logical_or(i == 0, i == 1)
    seq_end = jnp.logical_or(i == 0, i == pl.num_programs(0) - 1)
    ext_ref[0:halo, :] = jnp.where(seq_start, 0.0, prev_ref[...])
    ext_ref[halo:halo + rb, :] = main_ref[...]
    ext_ref[halo + rb:2 * halo + rb, :] = jnp.where(seq_end, 0.0, next_ref[...])
    acc = jnp.broadcast_to(b_ref[...], o_ref.shape)
    for k in range(SSM_CONV):
        lo = halo - pad + k
        acc = acc + w_ref[k:k + 1, :] * ext_ref[lo:lo + rb, :]
    o_ref[...] = acc * jax.nn.sigmoid(acc)


def _conv_silu(proj, conv_w8, conv_b, rb, tc):
    rows = proj.shape[0]
    n_row_tiles = rows // rb
    halo_per_tile = rb // V7X_SUBLANES
    n_halo_blocks = rows // V7X_SUBLANES
    col0 = COL_XBC // tc
    return pl.pallas_call(
        _conv_kernel,
        out_shape=jax.ShapeDtypeStruct((rows, SSM_XBC), F32),
        grid=(n_row_tiles, SSM_XBC // tc),
        in_specs=[pl.BlockSpec((rb, tc), lambda i, j: (i, col0 + j)),
                  pl.BlockSpec((V7X_SUBLANES, tc),
                               lambda i, j: (jnp.maximum(i * halo_per_tile - 1, 0), col0 + j)),
                  pl.BlockSpec((V7X_SUBLANES, tc),
                               lambda i, j: (jnp.minimum((i + 1) * halo_per_tile, n_halo_blocks - 1), col0 + j)),
                  pl.BlockSpec((V7X_SUBLANES, tc), lambda i, j: (0, j)),
                  pl.BlockSpec((1, tc), lambda i, j: (0, j))],
        out_specs=pl.BlockSpec((rb, tc), lambda i, j: (i, j)),
        scratch_shapes=[pltpu.VMEM((rb + 2 * V7X_SUBLANES, tc), F32)],
        compiler_params=_cparams(("arbitrary", "arbitrary"), 32),
        name="conv_silu",
    )(proj, proj, proj, conv_w8, conv_b)


def _gmlp_kernel(uv_ref, gain_ref, bias_ref, ws_ref, bst_ref, o_ref):
    g = jax.nn.gelu(uv_ref[...])
    u = g[:, 0:GM_WIDTH]
    v = _ln_rows(g[:, GM_WIDTH:2 * GM_WIDTH]) * gain_ref[...] + bias_ref[...]
    vb = v.astype(BF16)
    for h in range(GM_HEADS):
        cols = slice(h * GM_HEAD_DIM, (h + 1) * GM_HEAD_DIM)
        mixed = jnp.dot(ws_ref[h], vb[:, cols], preferred_element_type=F32) + bst_ref[:, h:h + 1]
        o_ref[:, cols] = (u[:, cols] * mixed).astype(BF16)


def _chunk_mlp(proj, lat_row0, n_chunks, v_gain, v_bias, ws_bf, bs_t):
    blk0 = lat_row0 // CHUNK
    return pl.pallas_call(
        _gmlp_kernel,
        out_shape=jax.ShapeDtypeStruct((n_chunks * CHUNK, GM_WIDTH), BF16),
        grid=(n_chunks,),
        in_specs=[pl.BlockSpec((CHUNK, 2 * GM_WIDTH), lambda c: (blk0 + c, 0)),
                  pl.BlockSpec((1, GM_WIDTH), lambda c: (0, 0)),
                  pl.BlockSpec((1, GM_WIDTH), lambda c: (0, 0)),
                  pl.BlockSpec((GM_HEADS, CHUNK, CHUNK), lambda c: (0, 0, 0)),
                  pl.BlockSpec((CHUNK, GM_HEADS), lambda c: (0, 0))],
        out_specs=pl.BlockSpec((CHUNK, GM_WIDTH), lambda c: (c, 0)),
        compiler_params=_cparams(("arbitrary",), 32),
        name="chunk_mlp",
    )(proj, v_gain, v_bias, ws_bf, bs_t)


def _ssd_chunk(direction, xs_ref, b_ref, c_ref, dtda_ref, mcum_ref, expand_ref, st_ref, emit_group):
    d = direction
    lane0 = d * SSM_HEADS
    dt = dtda_ref[:, 0:V7X_LANES]
    d_a = dtda_ref[:, V7X_LANES:2 * V7X_LANES]
    cs = jnp.dot(mcum_ref[...], d_a, preferred_element_type=F32, precision=lax.Precision.HIGHEST)
    cs_t = cs.T
    last = SSD_CHUNK - 1 if d == 0 else 0
    cs_end = cs[last:last + 1, :]
    dt_decay = dt * jnp.exp(cs_end - cs)
    exp_cs = jnp.exp(cs)
    chunk_decay = jnp.broadcast_to(jnp.exp(cs_end), (V7X_SUBLANES, V7X_LANES))
    q = jnp.concatenate([dt, dt_decay, chunk_decay], axis=0)
    q_hi = q.astype(BF16)
    q_lo = (q - q_hi.astype(F32)).astype(BF16)

    row_i = lax.broadcasted_iota(I32, (SSD_CHUNK, SSD_CHUNK), 0)
    col_i = lax.broadcasted_iota(I32, (SSD_CHUNK, SSD_CHUNK), 1)
    causal = (row_i >= col_i) if d == 0 else (row_i <= col_i)
    first_head = col_i < SSM_HEAD_DIM

    for g in range(SSM_GROUPS):
        gcols = slice(g * GROUP_WIDTH, (g + 1) * GROUP_WIDTH)
        ncols = slice(g * SSM_STATE, (g + 1) * SSM_STATE)
        c32 = c_ref[:, ncols]
        b_bf = b_ref[:, ncols].astype(BF16)
        c_bf = c32.astype(BF16)
        cb = lax.dot_general(c_bf, b_bf, (((1,), (1,)), ((), ())), preferred_element_type=F32)
        e_g = expand_ref[:, gcols]
        q_e = (jnp.dot(q_hi, e_g, preferred_element_type=F32)
               + jnp.dot(q_lo, e_g, preferred_element_type=F32))
        xs_g = xs_ref[:, gcols]
        x_dt = (xs_g * q_e[0:SSD_CHUNK]).astype(BF16)
        x_dec = (xs_g * q_e[SSD_CHUNK:2 * SSD_CHUNK]).astype(BF16)
        h_old = st_ref[g]
        h_bf = h_old.astype(BF16)
        y_pairs = []
        for pr in range(HEADS_PER_GROUP // 2):
            pcols = slice(pr * V7X_LANES, (pr + 1) * V7X_LANES)
            xp = x_dt[:, pcols]
            hp = h_bf[:, pcols]
            zero = jnp.zeros_like(xp)
            rhs = jnp.concatenate([jnp.where(first_head, xp, zero), jnp.where(first_head, hp, zero),
                                   jnp.where(first_head, zero, xp), jnp.where(first_head, zero, hp)], axis=0)
            parts = []
            for k in range(2):
                lane = lane0 + g * HEADS_PER_GROUP + 2 * pr + k
                seg = jnp.exp(jnp.where(causal, cs[:, lane:lane + 1] - cs_t[lane:lane + 1, :], -jnp.inf))
                parts.append((cb * seg).astype(BF16))
                parts.append((c32 * exp_cs[:, lane:lane + 1]).astype(BF16))
            lhs = jnp.concatenate(parts, axis=1)
            y_pairs.append(jnp.dot(lhs, rhs, preferred_element_type=F32))
        emit_group(g, xs_g, jnp.concatenate(y_pairs, axis=1))
        s_new = lax.dot_general(b_bf, x_dec, (((0,), (0,)), ((), ())), preferred_element_type=F32)
        st_ref[g] = h_old * q_e[2 * SSD_CHUNK:2 * SSD_CHUNK + 1] + s_new


def _ssd_fwd_kernel(xs_ref, b_ref, c_ref, dtda_ref, mcum_ref, expand_ref, y_ref, st_ref):
    @pl.when(pl.program_id(0) == 0)
    def _():
        st_ref[...] = jnp.zeros_like(st_ref)

    def emit(g, xs_g, y_g):
        y_ref[:, g * GROUP_WIDTH:(g + 1) * GROUP_WIDTH] = y_g

    _ssd_chunk(0, xs_ref, b_ref, c_ref, dtda_ref, mcum_ref, expand_ref, st_ref, emit)


def _ssd_bwd_kernel(xs_ref, b_ref, c_ref, dtda_ref, mcum_ref, expand_ref, yf_ref, z_ref,
                    dskip_ref, ngain_ref, o_ref, st_ref):
    @pl.when(pl.program_id(0) == 0)
    def _():
        st_ref[...] = jnp.zeros_like(st_ref)

    def emit(g, xs_g, y_g):
        gcols = slice(g * GROUP_WIDTH, (g + 1) * GROUP_WIDTH)
        y = y_g + yf_ref[:, gcols] + dskip_ref[:, gcols] * xs_g
        gated = y * jax.nn.silu(z_ref[:, gcols])
        ms = jnp.mean(gated * gated, axis=-1, keepdims=True)
        o_ref[:, gcols] = (gated * lax.rsqrt(ms + RMS_EPS) * ngain_ref[:, gcols]).astype(BF16)

    _ssd_chunk(1, xs_ref, b_ref, c_ref, dtda_ref, mcum_ref, expand_ref, st_ref, emit)


def _ssd_specs(chunk_of_step):
    xs_blk = SSM_WIDTH
    n_blk = SSM_GROUPS * SSM_STATE
    return [pl.BlockSpec((SSD_CHUNK, xs_blk), lambda i: (chunk_of_step(i), 0)),
            pl.BlockSpec((SSD_CHUNK, n_blk), lambda i: (chunk_of_step(i), SSM_WIDTH // n_blk)),
            pl.BlockSpec((SSD_CHUNK, n_blk), lambda i: (chunk_of_step(i), SSM_WIDTH // n_blk + 1)),
            pl.BlockSpec((SSD_CHUNK, 2 * V7X_LANES), lambda i: (chunk_of_step(i), 0)),
            pl.BlockSpec((SSD_CHUNK, SSD_CHUNK), lambda i: (0, 0)),
            pl.BlockSpec((V7X_LANES, SSM_WIDTH), lambda i: (0, 0))]


def _ssd_forward(conv, dtda, n_ctx_chunks, n_lat_chunks):
    n_steps = n_ctx_chunks + n_lat_chunks
    tri = (jnp.arange(SSD_CHUNK)[:, None] >= jnp.arange(SSD_CHUNK)[None, :]).astype(F32)
    expand = _head_expand(0)
    return pl.pallas_call(
        _ssd_fwd_kernel,
        out_shape=jax.ShapeDtypeStruct((n_lat_chunks * SSD_CHUNK, SSM_WIDTH), F32),
        grid=(n_steps,),
        in_specs=_ssd_specs(lambda i: i),
        out_specs=pl.BlockSpec((SSD_CHUNK, SSM_WIDTH), lambda i: (jnp.maximum(i - n_ctx_chunks, 0), 0)),
        scratch_shapes=[pltpu.VMEM((SSM_GROUPS, SSM_STATE, GROUP_WIDTH), F32)],
        compiler_params=_cparams(("arbitrary",), 40),
        name="ssd_forward",
    )(conv, conv, conv, dtda, tri, expand)


def _ssd_backward(conv, dtda, y_fwd, proj, d_skip, norm_gain, n_ctx_chunks, n_lat_chunks):
    n_steps = n_ctx_chunks + n_lat_chunks
    tri = (jnp.arange(SSD_CHUNK)[:, None] <= jnp.arange(SSD_CHUNK)[None, :]).astype(F32)
    expand = _head_expand(1)

    def chunk_of_step(i):
        return jnp.where(i < n_ctx_chunks, n_ctx_chunks - 1 - i, n_steps - 1 + n_ctx_chunks - i)

    def lat_of_step(i):
        return jnp.where(i < n_ctx_chunks, n_lat_chunks - 1, n_steps - 1 - i)

    return pl.pallas_call(
        _ssd_bwd_kernel,
        out_shape=jax.ShapeDtypeStruct((n_lat_chunks * SSD_CHUNK, SSM_WIDTH), BF16),
        grid=(n_steps,),
        in_specs=_ssd_specs(chunk_of_step) + [
            pl.BlockSpec((SSD_CHUNK, SSM_WIDTH), lambda i: (lat_of_step(i), 0)),
            pl.BlockSpec((SSD_CHUNK, SSM_WIDTH), lambda i: (chunk_of_step(i), COL_Z // SSM_WIDTH)),
            pl.BlockSpec((1, SSM_WIDTH), lambda i: (0, 0)),
            pl.BlockSpec((1, SSM_WIDTH), lambda i: (0, 0))],
        out_specs=pl.BlockSpec((SSD_CHUNK, SSM_WIDTH), lambda i: (lat_of_step(i), 0)),
        scratch_shapes=[pltpu.VMEM((SSM_GROUPS, SSM_STATE, GROUP_WIDTH), F32)],
        compiler_params=_cparams(("arbitrary",), 40),
        name="ssd_backward",
    )(conv, conv, conv, dtda, tri, expand, y_fwd, proj, d_skip, norm_gain)


def _head_expand(direction):
    lane = jnp.arange(V7X_LANES)[:, None]
    chan = jnp.arange(SSM_WIDTH)[None, :]
    return (lane == direction * SSM_HEADS + chan // SSM_HEAD_DIM).astype(BF16)


def _postmix_kernel(x_ref, mix_ref, p_ref, lng_ref, lnb_ref, wrt_ref, hfp_ref, acc_ref, aff_ref, hf_scr):
    gate_m = p_ref[0, 2:3, :]
    shift_f = p_ref[0, 3:4, :]
    scale1_f = 1.0 + p_ref[0, 4:5, :]
    tm = x_ref.shape[0]
    half = D_MODEL // 2

    def body(r, carry):
        sl = pl.ds(pl.multiple_of(r * ROW_STRIP, ROW_STRIP), ROW_STRIP)
        xn = _ln_rows(ALPHA * x_ref[sl, :] + gate_m * mix_ref[sl, :]) * lng_ref[...] + lnb_ref[...]
        acc_ref[sl, :] = ALPHA * xn
        hf = (_ln_rows(xn) * scale1_f + shift_f).astype(BF16)
        hf_scr[sl, :] = hf
        lo = lax.bitcast_convert_type(hf[:, 0:half].astype(F32), U32)
        hi = lax.bitcast_convert_type(hf[:, half:].astype(F32), U32)
        hfp_ref[sl, :] = (lo >> 16) | (hi & jnp.uint32(0xFFFF0000))
        return carry

    lax.fori_loop(0, tm // ROW_STRIP, body, 0)
    logits = lax.dot_general(wrt_ref[...], hf_scr[...], (((1,), (1,)), ((), ())),
                             preferred_element_type=F32)
    mx = jnp.max(logits, axis=0, keepdims=True)
    ex = jnp.exp(logits - mx)
    aff_ref[...] = ex / jnp.sum(ex, axis=0, keepdims=True)


def _post_mix(x2, mix, mods, ln_gain, ln_bias, wr_t, tm):
    l, d = x2.shape
    return pl.pallas_call(
        _postmix_kernel,
        out_shape=(jax.ShapeDtypeStruct((l, d // 2), U32),
                   jax.ShapeDtypeStruct((l, d), F32),
                   jax.ShapeDtypeStruct((N_EXPERTS, l), F32)),
        grid=(l // tm,),
        in_specs=[pl.BlockSpec((tm, d), lambda i: (i, 0)),
                  pl.BlockSpec((tm, d), lambda i: (i, 0)),
                  pl.BlockSpec((1, V7X_SUBLANES, d), lambda i: (0, 0, 0)),
                  pl.BlockSpec((1, d), lambda i: (0, 0)),
                  pl.BlockSpec((1, d), lambda i: (0, 0)),
                  pl.BlockSpec((N_EXPERTS, d), lambda i: (0, 0))],
        out_specs=(pl.BlockSpec((tm, d // 2), lambda i: (i, 0)),
                   pl.BlockSpec((tm, d), lambda i: (i, 0)),
                   pl.BlockSpec((N_EXPERTS, tm), lambda i: (0, i))),
        scratch_shapes=[pltpu.VMEM((tm, d), BF16)],
        compiler_params=_cparams(("arbitrary",), 48),
        name="post_mix",
    )(x2, mix, mods, ln_gain, ln_bias, wr_t)


def _topk_kernel(aff_ref, triu_ref, slow_ref, idx_ref, gate_ref, cum_scr, *, cap):
    n_e, n_blk, n_lane = aff_ref.shape
    aff = aff_ref[...]
    bits = lax.bitcast_convert_type(aff, I32)

    def count(mask):
        c = jnp.sum(jnp.where(mask, 1.0, 0.0), axis=1, keepdims=True)
        return jnp.sum(c, axis=2, keepdims=True)

    def search(k, thr):
        cand = thr | jnp.left_shift(jnp.int32(1), 30 - k)
        return jnp.where(count(bits >= cand) >= cap, cand, thr)

    thr = lax.fori_loop(0, 31, search, jnp.zeros((n_e, 1, 1), I32))
    above = bits > thr
    equal = bits == thr
    need = cap - count(above)

    def prefix(mask):
        m = jnp.where(mask, 1.0, 0.0).astype(BF16)
        within = jnp.dot(m.reshape(n_e * n_blk, n_lane), triu_ref[...],
                         preferred_element_type=F32).reshape(n_e, n_blk, n_lane)
        tot = jnp.broadcast_to(within[:, :, n_lane - 1:n_lane], (n_e, n_blk, n_lane)).astype(BF16)
        offs = [jnp.dot(slow_ref[...], tot[e], preferred_element_type=F32) for e in range(n_e)]
        return within + jnp.stack(offs, axis=0)

    cum_above = prefix(above)
    cum_equal = prefix(equal)
    chosen = jnp.logical_or(above, jnp.logical_and(equal, cum_equal <= need))
    cum = cum_above + jnp.minimum(cum_equal, need)
    cum_scr[...] = jnp.where(chosen, cum, -1.0)

    jt = SLOT_TILE
    lane_f =lax.broadcasted_iota(I32, (jt, n_lane), 1).astype(F32)

    def per_expert(e, carry):
        def per_tile(t, carry2):
            slot = (lax.broadcasted_iota(I32, (jt, n_lane), 0) + (t * jt + 1)).astype(F32)
            blk_acc = jnp.zeros((jt, n_lane), F32)
            gate_acc = jnp.zeros((jt, n_lane), F32)
            for b in range(n_blk):
                hit = cum_scr[e, b:b + 1, :] == slot
                blk_acc = blk_acc + jnp.where(hit, float(b + 1), 0.0)
                gate_acc = gate_acc + jnp.where(hit, aff_ref[e, b:b + 1, :], 0.0)
            tok = jnp.where(blk_acc > 0.0, (blk_acc - 1.0) * n_lane + lane_f, 0.0)
            tok = jnp.sum(tok, axis=1, keepdims=True)
            gate = jnp.sum(gate_acc, axis=1, keepdims=True)
            rows = pl.ds(pl.multiple_of(t * jt, jt), jt)
            idx_ref[e, rows, :] = jnp.broadcast_to(tok, (jt, n_lane)).astype(I32)
            gate_ref[e, rows, :] = jnp.broadcast_to(gate, (jt, n_lane))
            return carry2
        return lax.fori_loop(0, cap // jt, per_tile, carry)

    lax.fori_loop(0, n_e, per_expert, 0)


def _expert_choice_topk(aff_t, cap):
    n_e, l = aff_t.shape
    n_blk = l // V7X_LANES
    aff3 = aff_t.reshape(n_e, n_blk, V7X_LANES)
    triu = (jnp.arange(V7X_LANES)[:, None] <= jnp.arange(V7X_LANES)[None, :]).astype(BF16)
    slow = (jnp.arange(n_blk)[:, None] > jnp.arange(n_blk)[None, :]).astype(BF16)
    return pl.pallas_call(
        functools.partial(_topk_kernel, cap=cap),
        out_shape=(jax.ShapeDtypeStruct((n_e, cap, V7X_LANES), I32),
                   jax.ShapeDtypeStruct((n_e, cap, V7X_LANES), F32)),
        grid=(1,),
        in_specs=[pl.BlockSpec((n_e, n_blk, V7X_LANES), lambda i: (0, 0, 0)),
                  pl.BlockSpec((V7X_LANES, V7X_LANES), lambda i: (0, 0)),
                  pl.BlockSpec((n_blk, n_blk), lambda i: (0, 0))],
        out_specs=(pl.BlockSpec((n_e, cap, V7X_LANES), lambda i: (0, 0, 0)),
                   pl.BlockSpec((n_e, cap, V7X_LANES), lambda i: (0, 0, 0))),
        scratch_shapes=[pltpu.VMEM((n_e, n_blk, V7X_LANES), F32)],
        compiler_params=_cparams(("arbitrary",), 48),
        name="expert_choice_topk",
    )(aff3, triu, slow)


def _row_copy(src, dst, src_row, dst_row, sem):
    return pltpu.make_async_copy(src.at[pl.ds(src_row, 1), :], dst.at[pl.ds(dst_row, 1), :], sem)


def _gather_kernel(idx_ref, src, dst, sem, *, rows_per_step):
    base = pl.program_id(0) * rows_per_step

    def issue(j, carry):
        _row_copy(src, dst, idx_ref[base + j], base + j, sem).start()
        return carry

    def drain(j, carry):
        _row_copy(src, dst, 0, 0, sem).wait()
        return carry

    lax.fori_loop(0, rows_per_step, issue, 0)
    lax.fori_loop(0, rows_per_step, drain, 0)


def _gather_rows(idx_flat, src, rows_per_step):
    n = idx_flat.shape[0]
    return pl.pallas_call(
        functools.partial(_gather_kernel, rows_per_step=rows_per_step),
        out_shape=jax.ShapeDtypeStruct((n, src.shape[1]), src.dtype),
        grid_spec=pltpu.PrefetchScalarGridSpec(
            num_scalar_prefetch=1, grid=(n // rows_per_step,),
            in_specs=[pl.BlockSpec(memory_space=pl.ANY)],
            out_specs=pl.BlockSpec(memory_space=pl.ANY),
            scratch_shapes=[pltpu.SemaphoreType.DMA(())]),
        compiler_params=_cparams(("arbitrary",), 16),
        name="gather_rows",
    )(idx_flat, src)


def _ffn_kernel(xp_ref, wg_ref, wu_ref, wd_ref, gate_ref, gf_ref, o_ref, xe_scr, h_scr, *, n_ff_tiles):
    s = pl.program_id(1)
    half = D_MODEL // 2
    tf = wg_ref.shape[2]

    @pl.when(s == 0)
    def _():
        w = xp_ref[...]
        xe_scr[:, 0:half] = lax.bitcast_convert_type(w << 16, F32).astype(BF16)
        xe_scr[:, half:] = lax.bitcast_convert_type(w & jnp.uint32(0xFFFF0000), F32).astype(BF16)

    @pl.when(s < n_ff_tiles)
    def _():
        xe = xe_scr[...]
        a = jnp.dot(xe, wg_ref[0].astype(BF16), preferred_element_type=F32)
        u = jnp.dot(xe, wu_ref[0].astype(BF16), preferred_element_type=F32)
        h_scr[:, pl.ds(pl.multiple_of(s * tf, tf), tf)] = (jax.nn.silu(a) * u).astype(BF16)

    @pl.when(s >= n_ff_tiles)
    def _():
        y = jnp.dot(h_scr[...], wd_ref[0].astype(BF16), preferred_element_type=F32)
        o_ref[0] = y * gate_ref[0, :, 0:1] * gf_ref[...]


def _expert_ffn(xe_packed, w_gate, w_up, w_down, gates, gate_f, cap, tf, tn):
    n_e, d, ff = w_gate.shape
    n_ff = ff // tf
    n_out = d // tn
    col_a = lambda e, s: (e, 0, jnp.minimum(s, n_ff - 1))
    col_b = lambda e, s: (e, 0, jnp.maximum(s - n_ff, 0))
    return pl.pallas_call(
        functools.partial(_ffn_kernel, n_ff_tiles=n_ff),
        out_shape=jax.ShapeDtypeStruct((n_e, cap, d), F32),
        grid=(n_e, n_ff + n_out),
        in_specs=[pl.BlockSpec((cap, d // 2), lambda e, s: (e, 0), pipeline_mode=pl.Buffered(1)),
                  pl.BlockSpec((1, d, tf), col_a),
                  pl.BlockSpec((1, d, tf), col_a),
                  pl.BlockSpec((1, ff, tn), col_b),
                  pl.BlockSpec((1, cap, V7X_LANES), lambda e, s: (e, 0, 0), pipeline_mode=pl.Buffered(1)),
                  pl.BlockSpec((1, tn), lambda e, s: (0, jnp.maximum(s - n_ff, 0)))],
        out_specs=pl.BlockSpec((1, cap, tn), col_b),
        scratch_shapes=[pltpu.VMEM((cap, d), BF16), pltpu.VMEM((cap, ff), BF16)],
        compiler_params=_cparams(("arbitrary", "arbitrary"), 60),
        name="expert_ffn",
    )(xe_packed, w_gate, w_up, w_down, gates, gate_f)


def _combine_kernel(idx_ref, ye_ref, acc_in, acc_out, buf, sems, *, cap):
    tm = buf.shape[0]
    base = pl.program_id(0) * cap + pl.program_id(1) * tm

    def fetch(j, carry):
        _row_copy(acc_in, buf, idx_ref[base + j], j, sems.at[0]).start()
        return carry

    def fetch_wait(j, carry):
        _row_copy(acc_in, buf, 0, 0, sems.at[0]).wait()
        return carry

    def put(j, carry):
        _row_copy(buf, acc_out, j, idx_ref[base + j], sems.at[1]).start()
        return carry

    def put_wait(j, carry):
        _row_copy(buf, acc_out, 0, 0, sems.at[1]).wait()
        return carry

    lax.fori_loop(0, tm, fetch, 0)
    lax.fori_loop(0, tm, fetch_wait, 0)
    buf[...] = buf[...] + ye_ref[0]
    lax.fori_loop(0, tm, put, 0)
    lax.fori_loop(0, tm, put_wait, 0)


def _combine(idx_flat, ye, acc, tm):
    n_e, cap, d = ye.shape
    return pl.pallas_call(
        functools.partial(_combine_kernel, cap=cap),
        out_shape=jax.ShapeDtypeStruct(acc.shape, acc.dtype),
        grid_spec=pltpu.PrefetchScalarGridSpec(
            num_scalar_prefetch=1, grid=(n_e, cap // tm),
            in_specs=[pl.BlockSpec((1, tm, d), lambda e, m, idx: (e, m, 0)),
                      pl.BlockSpec(memory_space=pl.ANY)],
            out_specs=pl.BlockSpec(memory_space=pl.ANY),
            scratch_shapes=[pltpu.VMEM((tm, d), F32), pltpu.SemaphoreType.DMA((2,))]),
        input_output_aliases={2: 0},
        compiler_params=_cparams(("arbitrary", "arbitrary"), 32),
        name="combine",
    )(idx_flat, ye, acc)


def _final_ln_kernel(a_ref, g_ref, b_ref, o_ref):
    def body(r, carry):
        sl = pl.ds(pl.multiple_of(r * ROW_STRIP, ROW_STRIP), ROW_STRIP)
        o_ref[sl, :] = _ln_rows(a_ref[sl, :]) * g_ref[...] + b_ref[...]
        return carry
    lax.fori_loop(0, a_ref.shape[0] // ROW_STRIP, body, 0)


def _final_ln(acc, gain, bias, tm):
    l, d = acc.shape
    return pl.pallas_call(
        _final_ln_kernel,
        out_shape=jax.ShapeDtypeStruct((l, d), F32),
        grid=(l // tm,),
        in_specs=[pl.BlockSpec((tm, d), lambda i: (i, 0)),
                  pl.BlockSpec((1, d), lambda i: (0, 0)),
                  pl.BlockSpec((1, d), lambda i: (0, 0))],
        out_specs=pl.BlockSpec((tm, d), lambda i: (i, 0)),
        compiler_params=_cparams(("arbitrary",), 32),
        name="final_ln",
    )(acc, gain, bias)


def _pad_rows(a, rows):
    return jnp.pad(a, ((0, rows - a.shape[0]), (0, 0)))


def _pad_lanes(a, lanes):
    return jnp.pad(a, ((0, 0), (0, lanes - a.shape[1])))


def kernel(x, c, ctx, c_ctx, w_ada, b_ada, w_in, gm_v_gain, gm_v_bias, gm_w_s, gm_b_s, ssm_conv_w, ssm_conv_b, ssm_dt_bias, ssm_a_log, ssm_d, ssm_norm_gain, w_out, ln_mix_gain, ln_mix_bias, w_router, w_gate, w_up, w_down, ln_ffn_gain, ln_ffn_bias):
    b, l, d = x.shape
    lc = ctx.shape[1]
    assert b == 1 and d == D_MODEL and w_ada.shape[0] == DEPTH
    x2 = x[0]
    ctx2 = ctx[0]
    layer = 0

    cond = _pad_rows(jnp.concatenate([c, c_ctx[None, :]], axis=0), V7X_SUBLANES)
    ada = _ada_params(cond, w_ada[layer], b_ada[layer][None, :])
    mods = jnp.pad(ada[:2].reshape(2, 6, d), ((0, 0), (0, V7X_SUBLANES - 6), (0, 0)))

    w_in_l = w_in[layer]
    wdt = _pad_lanes(w_in_l[:, COL_DT:], V7X_LANES).astype(BF16)
    dt_bias = _pad_lanes(ssm_dt_bias[layer].reshape(1, 2 * SSM_HEADS), V7X_LANES)
    a_log = _pad_lanes(ssm_a_log[layer].reshape(1, 2 * SSM_HEADS), V7X_LANES)
    h, dtda = _modulate_in(x2, ctx2, mods, wdt, dt_bias, a_log)
    proj = _matmul(h, w_in_l, COL_DT, tm=768, tn=512, name="in_proj")

    conv_w8 = _pad_rows(ssm_conv_w[layer], V7X_SUBLANES)
    conv = _conv_silu(proj, conv_w8, ssm_conv_b[layer][None, :], rb=lc, tc=512)

    n_lat_chunks = l // CHUNK
    n_ctx_chunks = lc // SSD_CHUNK
    gm = _chunk_mlp(proj, lc, n_lat_chunks, gm_v_gain[layer][None, :], gm_v_bias[layer][None, :],
                    gm_w_s[layer].astype(BF16), gm_b_s[layer].T)

    y_fwd = _ssd_forward(conv, dtda, n_ctx_chunks, n_lat_chunks)
    d_skip = jnp.repeat(ssm_d[layer], SSM_HEAD_DIM)[None, :]
    ss = _ssd_backward(conv, dtda, y_fwd, proj, d_skip, ssm_norm_gain[layer][None, :],
                       n_ctx_chunks, n_lat_chunks)

    mix = _matmul_concat(gm, ss, w_out[layer], tm=1024, tn=512, name="out_proj")

    hf_packed, acc, aff_t = _post_mix(x2, mix, mods, ln_mix_gain[layer][None, :], ln_mix_bias[layer][None, :],
                                      w_router[layer].T.astype(BF16), tm=256)

    cap = EC_FACTOR * l // N_EXPERTS
    idx3, gates = _expert_choice_topk(aff_t, cap)
    idx_flat = idx3[:, :, 0].reshape(N_EXPERTS * cap)

    xe_packed = _gather_rows(idx_flat, hf_packed, rows_per_step=256)
    gate_f = mods[0, 5:6, :]
    ye = _expert_ffn(xe_packed, w_gate[layer], w_up[layer], w_down[layer], gates, gate_f,
                     cap, tf=256, tn=512)
    acc = _combine(idx_flat, ye, acc, tm=256)
    out = _final_ln(acc, ln_ffn_gain[layer][None, :], ln_ffn_bias[layer][None, :], tm=256)
    return out[None]
```

```python
import functools

import jax
import jax.numpy as jnp
from jax import lax
from jax.experimental import pallas as pl
from jax.experimental.pallas import tpu as pltpu

F32 = jnp.float32
BF16 = jnp.bfloat16
I32 = jnp.int32
U32 = jnp.uint32

D_MODEL = 4096
GRID_W = 64
CHUNK = 128
GM_WIDTH = 2048
GM_HEADS = 8
GM_HEAD_DIM = GM_WIDTH // GM_HEADS
SSM_WIDTH = 2048
SSM_HEAD_DIM = 64
SSM_HEADS = SSM_WIDTH // SSM_HEAD_DIM
SSM_GROUPS = 8
SSM_STATE = 128
SSM_CONV = 5
SSM_XBC = SSM_WIDTH + 2 * SSM_GROUPS * SSM_STATE
SSD_CHUNK = 128
HEADS_PER_GROUP = SSM_HEADS // SSM_GROUPS
GROUP_WIDTH = SSM_WIDTH // SSM_GROUPS
N_EXPERTS = 16
EXPERT_FF = 2048
EC_FACTOR = 2
DEPTH = 1
ALPHA = (2 * DEPTH) ** 0.25
LN_EPS = 1e-5
RMS_EPS = 1e-5

COL_Z = 2 * GM_WIDTH
COL_XBC = COL_Z + SSM_WIDTH
COL_DT = COL_XBC + SSM_XBC

V7X_LANES = 128
V7X_SUBLANES = 8
V7X_VMEM_BYTES = 64 * 1024 * 1024
MIB = 1024 * 1024

ROW_STRIP = 32
SLOT_TILE = 128
DMA_UNROLL = 8


def _cparams(semantics, vmem_mib):
    return pltpu.CompilerParams(dimension_semantics=semantics,
                                vmem_limit_bytes=min(vmem_mib * MIB, V7X_VMEM_BYTES - 2 * MIB))


def _ln_rows(x):
    mu = jnp.mean(x, axis=-1, keepdims=True)
    xc = x - mu
    var = jnp.mean(xc * xc, axis=-1, keepdims=True)
    return xc * lax.rsqrt(var + LN_EPS)


def _ada_kernel(c_ref, w_ref, b_ref, o_ref):
    s = jax.nn.silu(c_ref[...]).astype(BF16)
    o_ref[...] = jnp.dot(s, w_ref[...].astype(BF16), preferred_element_type=F32) + b_ref[...]


def _ada_params(cond, w_ada, b_ada):
    k, n = w_ada.shape
    tn = 512
    return pl.pallas_call(
        _ada_kernel,
        out_shape=jax.ShapeDtypeStruct((V7X_SUBLANES, n), F32),
        grid=(n // tn,),
        in_specs=[pl.BlockSpec((V7X_SUBLANES, k), lambda j: (0, 0)),
                  pl.BlockSpec((k, tn), lambda j: (0, j)),
                  pl.BlockSpec((1, tn), lambda j: (0, j))],
        out_specs=pl.BlockSpec((V7X_SUBLANES, tn), lambda j: (0, j)),
        compiler_params=_cparams(("arbitrary",), 32),
        name="ada_params",
    )(cond, w_ada, b_ada)


def _modin_kernel(x_ref, ctx_ref, p_ref, wdt_ref, bias_ref, alog_ref, h_ref, dtda_ref):
    i = pl.program_id(0)
    shift = p_ref[0, 0:1, :]
    scale1 = 1.0 + p_ref[0, 1:2, :]
    tm = h_ref.shape[0]

    def rows(src_ref):
        def body(r, carry):
            sl = pl.ds(pl.multiple_of(r * ROW_STRIP, ROW_STRIP), ROW_STRIP)
            h_ref[sl, :] = (_ln_rows(src_ref[sl, :]) * scale1 + shift).astype(BF16)
            return carry
        lax.fori_loop(0, tm // ROW_STRIP, body, 0)

    @pl.when(i == 0)
    def _():
        rows(ctx_ref)

    @pl.when(i > 0)
    def _():
        rows(x_ref)

    raw = jnp.dot(h_ref[...], wdt_ref[...], preferred_element_type=F32) + bias_ref[...]
    dt = jnp.maximum(raw, 0.0) + jnp.log1p(jnp.exp(-jnp.abs(raw)))
    dtda_ref[:, 0:V7X_LANES] = dt
    dtda_ref[:, V7X_LANES:2 * V7X_LANES] = dt * (-jnp.exp(alog_ref[...]))


def _modulate_in(x2, ctx2, mods, wdt, dt_bias, a_log):
    l, d = x2.shape
    lc = ctx2.shape[0]
    tm = lc
    n_tiles = (l + lc) // tm
    return pl.pallas_call(
        _modin_kernel,
        out_shape=(jax.ShapeDtypeStruct((l + lc, d), BF16),
                   jax.ShapeDtypeStruct((l + lc, 2 * V7X_LANES), F32)),
        grid=(n_tiles,),
        in_specs=[pl.BlockSpec((tm, d), lambda i: (jnp.maximum(i - 1, 0), 0)),
                  pl.BlockSpec((tm, d), lambda i: (0, 0)),
                  pl.BlockSpec((1, V7X_SUBLANES, d), lambda i: (jnp.where(i == 0, 1, 0), 0, 0)),
                  pl.BlockSpec((d, V7X_LANES), lambda i: (0, 0)),
                  pl.BlockSpec((1, V7X_LANES), lambda i: (0, 0)),
                  pl.BlockSpec((1, V7X_LANES), lambda i: (0, 0))],
        out_specs=(pl.BlockSpec((tm, d), lambda i: (i, 0)),
                   pl.BlockSpec((tm, 2 * V7X_LANES), lambda i: (i, 0))),
        compiler_params=_cparams(("arbitrary",), 40),
        name="modulate_in",
    )(x2, ctx2, mods, wdt, dt_bias, a_log)


def _mm_kernel(a_ref, w_ref, o_ref, wbf_ref):
    @pl.when(pl.program_id(1) == 0)
    def _():
        wbf_ref[...] = w_ref[...].astype(BF16)

    o_ref[...] = jnp.dot(a_ref[...], wbf_ref[...], preferred_element_type=F32)


def _matmul(a, w3, n_out, tm, tn, name):
    m, k = a.shape
    return pl.pallas_call(
        _mm_kernel,
        out_shape=jax.ShapeDtypeStruct((m, n_out), F32),
        grid=(n_out // tn, m // tm),
        in_specs=[pl.BlockSpec((tm, k), lambda j, i: (i, 0)),
                  pl.BlockSpec((None, k, tn), lambda j, i: (0, 0, j))],
        out_specs=pl.BlockSpec((tm, tn), lambda j, i: (i, j)),
        scratch_shapes=[pltpu.VMEM((k, tn), BF16)],
        compiler_params=_cparams(("arbitrary", "arbitrary"), 56),
        name=name,
    )(a, w3)


def _mm2_kernel(a0_ref, a1_ref, w_ref, o_ref, wbf_ref):
    @pl.when(pl.program_id(1) == 0)
    def _():
        wbf_ref[...] = w_ref[...].astype(BF16)

    k0 = a0_ref.shape[1]
    acc = jnp.dot(a0_ref[...], wbf_ref[0:k0, :], preferred_element_type=F32)
    o_ref[...] = acc + jnp.dot(a1_ref[...], wbf_ref[k0:, :], preferred_element_type=F32)


def _matmul_concat(a0, a1, w, tm, tn, name):
    m, k0 = a0.shape
    k1 = a1.shape[1]
    n_out = w.shape[1]
    return pl.pallas_call(
        _mm2_kernel,
        out_shape=jax.ShapeDtypeStruct((m, n_out), F32),
        grid=(n_out // tn, m // tm),
        in_specs=[pl.BlockSpec((tm, k0), lambda j, i: (i, 0)),
                  pl.BlockSpec((tm, k1), lambda j, i: (i, 0)),
                  pl.BlockSpec((k0 + k1, tn), lambda j, i: (0, j))],
        out_specs=pl.BlockSpec((tm, tn), lambda j, i: (i, j)),
        scratch_shapes=[pltpu.VMEM((k0 + k1, tn), BF16)],
        compiler_params=_cparams(("arbitrary", "arbitrary"), 56),
        name=name,
    )(a0, a1, w)


def _conv_kernel(main_ref, prev_ref, next_ref, w_ref, b_ref, o_ref, ext_ref):
    i = pl.program_id(0)
    rb = main_ref.shape[0]
    halo = V7X_SUBLANES
    pad = (SSM_CONV - 1) // 2
    seq_start = jnp.logical_or(i == 0, i == 1)
    seq_end = jnp.logical_or(i == 0, i == pl.num_programs(0) - 1)
    ext_ref[0:halo, :] = jnp.where(seq_start, 0.0, prev_ref[...])
    ext_ref[halo:halo + rb, :] = main_ref[...]
    ext_ref[halo + rb:2 * halo + rb, :] = jnp.where(seq_end, 0.0, next_ref[...])
    acc = jnp.broadcast_to(b_ref[...], o_ref.shape)
    for k in range(SSM_CONV):
        lo = halo - pad + k
        acc = acc + w_ref[k:k + 1, :] * ext_ref[lo:lo + rb, :]
    o_ref[...] = acc * jax.nn.sigmoid(acc)


def _conv_silu(proj, conv_w8, conv_b, rb, tc):
    rows = proj.shape[0]
    n_row_tiles = rows // rb
    halo_per_tile = rb // V7X_SUBLANES
    n_halo_blocks = rows // V7X_SUBLANES
    col0 = COL_XBC // tc
    return pl.pallas_call(
        _conv_kernel,
        out_shape=jax.ShapeDtypeStruct((rows, SSM_XBC), F32),
        grid=(n_row_tiles, SSM_XBC // tc),
        in_specs=[pl.BlockSpec((rb, tc), lambda i, j: (i, col0 + j)),
                  pl.BlockSpec((V7X_SUBLANES, tc),
                               lambda i, j: (jnp.maximum(i * halo_per_tile - 1, 0), col0 + j)),
                  pl.BlockSpec((V7X_SUBLANES, tc),
                               lambda i, j: (jnp.minimum((i + 1) * halo_per_tile, n_halo_blocks - 1), col0 + j)),
                  pl.BlockSpec((V7X_SUBLANES, tc), lambda i, j: (0, j)),
                  pl.BlockSpec((1, tc), lambda i, j: (0, j))],
        out_specs=pl.BlockSpec((rb, tc), lambda i, j: (i, j)),
        scratch_shapes=[pltpu.VMEM((rb + 2 * V7X_SUBLANES, tc), F32)],
        compiler_params=_cparams(("arbitrary", "arbitrary"), 32),
        name="conv_silu",
    )(proj, proj, proj, conv_w8, conv_b)


def _gmlp_kernel(uv_ref, gain_ref, bias_ref, ws_ref, bst_ref, o_ref):
    g = jax.nn.gelu(uv_ref[...])
    u = g[:, 0:GM_WIDTH]
    v = _ln_rows(g[:, GM_WIDTH:2 * GM_WIDTH]) * gain_ref[...] + bias_ref[...]
    vb = v.astype(BF16)
    for h in range(GM_HEADS):
        cols = slice(h * GM_HEAD_DIM, (h + 1) * GM_HEAD_DIM)
        mixed = jnp.dot(ws_ref[h], vb[:, cols], preferred_element_type=F32) + bst_ref[:, h:h + 1]
        o_ref[:, cols] = (u[:, cols] * mixed).astype(BF16)


def _chunk_mlp(proj, lat_row0, n_chunks, v_gain, v_bias, ws_bf, bs_t):
    blk0 = lat_row0 // CHUNK
    return pl.pallas_call(
        _gmlp_kernel,
        out_shape=jax.ShapeDtypeStruct((n_chunks * CHUNK, GM_WIDTH), BF16),
        grid=(n_chunks,),
        in_specs=[pl.BlockSpec((CHUNK, 2 * GM_WIDTH), lambda c: (blk0 + c, 0)),
                  pl.BlockSpec((1, GM_WIDTH), lambda c: (0, 0)),
                  pl.BlockSpec((1, GM_WIDTH), lambda c: (0, 0)),
                  pl.BlockSpec((GM_HEADS, CHUNK, CHUNK), lambda c: (0, 0, 0)),
                  pl.BlockSpec((CHUNK, GM_HEADS), lambda c: (0, 0))],
        out_specs=pl.BlockSpec((CHUNK, GM_WIDTH), lambda c: (c, 0)),
        compiler_params=_cparams(("arbitrary",), 32),
        name="chunk_mlp",
    )(proj, v_gain, v_bias, ws_bf, bs_t)


def _ssd_chunk(direction, xs_ref, b_ref, c_ref, dtda_ref, mcum_ref, expand_ref, st_ref, emit_group):
    d = direction
    lane0 = d * SSM_HEADS
    dt = dtda_ref[:, 0:V7X_LANES]
    d_a = dtda_ref[:, V7X_LANES:2 * V7X_LANES]
    cs = jnp.dot(mcum_ref[...], d_a, preferred_element_type=F32, precision=lax.Precision.HIGHEST)
    cs_t = cs.T
    last = SSD_CHUNK - 1 if d == 0 else 0
    cs_end = cs[last:last + 1, :]
    dt_decay = dt * jnp.exp(cs_end - cs)
    exp_cs = jnp.exp(cs)
    chunk_decay = jnp.broadcast_to(jnp.exp(cs_end), (V7X_SUBLANES, V7X_LANES))
    q = jnp.concatenate([dt, dt_decay, chunk_decay], axis=0)
    q_hi = q.astype(BF16)
    q_lo = (q - q_hi.astype(F32)).astype(BF16)

    row_i = lax.broadcasted_iota(I32, (SSD_CHUNK, SSD_CHUNK), 0)
    col_i = lax.broadcasted_iota(I32, (SSD_CHUNK, SSD_CHUNK), 1)
    causal = (row_i >= col_i) if d == 0 else (row_i <= col_i)
    first_head = col_i < SSM_HEAD_DIM

    for g in range(SSM_GROUPS):
        gcols = slice(g * GROUP_WIDTH, (g + 1) * GROUP_WIDTH)
        ncols = slice(g * SSM_STATE, (g + 1) * SSM_STATE)
        c32 = c_ref[:, ncols]
        b_bf = b_ref[:, ncols].astype(BF16)
        c_bf = c32.astype(BF16)
        cb = lax.dot_general(c_bf, b_bf, (((1,), (1,)), ((), ())), preferred_element_type=F32)
        e_g = expand_ref[:, gcols]
        q_e = (jnp.dot(q_hi, e_g, preferred_element_type=F32)
               + jnp.dot(q_lo, e_g, preferred_element_type=F32))
        xs_g = xs_ref[:, gcols]
        x_dt = (xs_g * q_e[0:SSD_CHUNK]).astype(BF16)
        x_dec = (xs_g * q_e[SSD_CHUNK:2 * SSD_CHUNK]).astype(BF16)
        h_old = st_ref[g]
        h_bf = h_old.astype(BF16)
        y_pairs = []
        for pr in range(HEADS_PER_GROUP // 2):
            pcols = slice(pr * V7X_LANES, (pr + 1) * V7X_LANES)
            xp = x_dt[:, pcols]
            hp = h_bf[:, pcols]
            zero = jnp.zeros_like(xp)
            rhs = jnp.concatenate([jnp.where(first_head, xp, zero), jnp.where(first_head, hp, zero),
                                   jnp.where(first_head, zero, xp), jnp.where(first_head, zero, hp)], axis=0)
            parts = []
            for k in range(2):
                lane = lane0 + g * HEADS_PER_GROUP + 2 * pr + k
                seg = jnp.exp(jnp.where(causal, cs[:, lane:lane + 1] - cs_t[lane:lane + 1, :], -jnp.inf))
                parts.append((cb * seg).astype(BF16))
                parts.append((c32 * exp_cs[:, lane:lane + 1]).astype(BF16))
            lhs = jnp.concatenate(parts, axis=1)
            y_pairs.append(jnp.dot(lhs, rhs, preferred_element_type=F32))
        emit_group(g, xs_g, jnp.concatenate(y_pairs, axis=1))
        s_new = lax.dot_general(b_bf, x_dec, (((0,), (0,)), ((), ())), preferred_element_type=F32)
        st_ref[g] = h_old * q_e[2 * SSD_CHUNK:2 * SSD_CHUNK + 1] + s_new


def _ssd_fwd_kernel(xs_ref, b_ref, c_ref, dtda_ref, mcum_ref, expand_ref, y_ref, st_ref):
    @pl.when(pl.program_id(0) == 0)
    def _():
        st_ref[...] = jnp.zeros_like(st_ref)

    def emit(g, xs_g, y_g):
        y_ref[:, g * GROUP_WIDTH:(g + 1) * GROUP_WIDTH] = y_g

    _ssd_chunk(0, xs_ref, b_ref, c_ref, dtda_ref, mcum_ref, expand_ref, st_ref, emit)


def _ssd_bwd_kernel(xs_ref, b_ref, c_ref, dtda_ref, mcum_ref, expand_ref, yf_ref, z_ref,
                    dskip_ref, ngain_ref, o_ref, st_ref):
    @pl.when(pl.program_id(0) == 0)
    def _():
        st_ref[...] = jnp.zeros_like(st_ref)

    def emit(g, xs_g, y_g):
        gcols = slice(g * GROUP_WIDTH, (g + 1) * GROUP_WIDTH)
        y = y_g + yf_ref[:, gcols] + dskip_ref[:, gcols] * xs_g
        gated = y * jax.nn.silu(z_ref[:, gcols])
        ms = jnp.mean(gated * gated, axis=-1, keepdims=True)
        o_ref[:, gcols] = (gated * lax.rsqrt(ms + RMS_EPS) * ngain_ref[:, gcols]).astype(BF16)

    _ssd_chunk(1, xs_ref, b_ref, c_ref, dtda_ref, mcum_ref, expand_ref, st_ref, emit)


def _ssd_specs(chunk_of_step):
    xs_blk = SSM_WIDTH
    n_blk = SSM_GROUPS * SSM_STATE
    return [pl.BlockSpec((SSD_CHUNK, xs_blk), lambda i: (chunk_of_step(i), 0)),
            pl.BlockSpec((SSD_CHUNK, n_blk), lambda i: (chunk_of_step(i), SSM_WIDTH // n_blk)),
            pl.BlockSpec((SSD_CHUNK, n_blk), lambda i: (chunk_of_step(i), SSM_WIDTH // n_blk + 1)),
            pl.BlockSpec((SSD_CHUNK, 2 * V7X_LANES), lambda i: (chunk_of_step(i), 0)),
            pl.BlockSpec((SSD_CHUNK, SSD_CHUNK), lambda i: (0, 0)),
            pl.BlockSpec((V7X_LANES, SSM_WIDTH), lambda i: (0, 0))]


def _ssd_forward(conv, dtda, n_ctx_chunks, n_lat_chunks):
    n_steps = n_ctx_chunks + n_lat_chunks
    tri = (jnp.arange(SSD_CHUNK)[:, None] >= jnp.arange(SSD_CHUNK)[None, :]).astype(F32)
    expand = _head_expand(0)
    return pl.pallas_call(
        _ssd_fwd_kernel,
        out_shape=jax.ShapeDtypeStruct((n_lat_chunks * SSD_CHUNK, SSM_WIDTH), F32),
        grid=(n_steps,),
        in_specs=_ssd_specs(lambda i: i),
        out_specs=pl.BlockSpec((SSD_CHUNK, SSM_WIDTH), lambda i: (jnp.maximum(i - n_ctx_chunks, 0), 0)),
        scratch_shapes=[pltpu.VMEM((SSM_GROUPS, SSM_STATE, GROUP_WIDTH), F32)],
        compiler_params=_cparams(("arbitrary",), 40),
        name="ssd_forward",
    )(conv, conv, conv, dtda, tri, expand)


def _ssd_backward(conv, dtda, y_fwd, proj, d_skip, norm_gain, n_ctx_chunks, n_lat_chunks):
    n_steps = n_ctx_chunks + n_lat_chunks
    tri = (jnp.arange(SSD_CHUNK)[:, None] <= jnp.arange(SSD_CHUNK)[None, :]).astype(F32)
    expand = _head_expand(1)

    def chunk_of_step(i):
        return jnp.where(i < n_ctx_chunks, n_ctx_chunks - 1 - i, n_steps - 1 + n_ctx_chunks - i)

    def lat_of_step(i):
        return jnp.where(i < n_ctx_chunks, n_lat_chunks - 1, n_steps - 1 - i)

    return pl.pallas_call(
        _ssd_bwd_kernel,
        out_shape=jax.ShapeDtypeStruct((n_lat_chunks * SSD_CHUNK, SSM_WIDTH), BF16),
        grid=(n_steps,),
        in_specs=_ssd_specs(chunk_of_step) + [
            pl.BlockSpec((SSD_CHUNK, SSM_WIDTH), lambda i: (lat_of_step(i), 0)),
            pl.BlockSpec((SSD_CHUNK, SSM_WIDTH), lambda i: (chunk_of_step(i), COL_Z // SSM_WIDTH)),
            pl.BlockSpec((1, SSM_WIDTH), lambda i: (0, 0)),
            pl.BlockSpec((1, SSM_WIDTH), lambda i: (0, 0))],
        out_specs=pl.BlockSpec((SSD_CHUNK, SSM_WIDTH), lambda i: (lat_of_step(i), 0)),
        scratch_shapes=[pltpu.VMEM((SSM_GROUPS, SSM_STATE, GROUP_WIDTH), F32)],
        compiler_params=_cparams(("arbitrary",), 40),
        name="ssd_backward",
    )(conv, conv, conv, dtda, tri, expand, y_fwd, proj, d_skip, norm_gain)


def _head_expand(direction):
    lane = jnp.arange(V7X_LANES)[:, None]
    chan = jnp.arange(SSM_WIDTH)[None, :]
    return (lane == direction * SSM_HEADS + chan // SSM_HEAD_DIM).astype(BF16)


def _postmix_kernel(x_ref, mix_ref, p_ref, lng_ref, lnb_ref, wrt_ref, hfp_ref, acc_ref, aff_ref, hf_scr):
    gate_m = p_ref[0, 2:3, :]
    shift_f = p_ref[0, 3:4, :]
    scale1_f = 1.0 + p_ref[0, 4:5, :]
    tm = x_ref.shape[0]
    half = D_MODEL // 2

    def body(r, carry):
        sl = pl.ds(pl.multiple_of(r * ROW_STRIP, ROW_STRIP), ROW_STRIP)
        xn = _ln_rows(ALPHA * x_ref[sl, :] + gate_m * mix_ref[sl, :]) * lng_ref[...] + lnb_ref[...]
        acc_ref[sl, :] = ALPHA * xn
        hf = (_ln_rows(xn) * scale1_f + shift_f).astype(BF16)
        hf_scr[sl, :] = hf
        lo = lax.bitcast_convert_type(hf[:, 0:half].astype(F32), U32)
        hi = lax.bitcast_convert_type(hf[:, half:].astype(F32), U32)
        hfp_ref[sl, :] = (lo >> 16) | (hi & jnp.uint32(0xFFFF0000))
        return carry

    lax.fori_loop(0, tm // ROW_STRIP, body, 0)
    logits = lax.dot_general(wrt_ref[...], hf_scr[...], (((1,), (1,)), ((), ())),
                             preferred_element_type=F32)
    mx = jnp.max(logits, axis=0, keepdims=True)
    ex = jnp.exp(logits - mx)
    aff_ref[...] = ex / jnp.sum(ex, axis=0, keepdims=True)


def _post_mix(x2, mix, mods, ln_gain, ln_bias, wr_t, tm):
    l, d = x2.shape
    return pl.pallas_call(
        _postmix_kernel,
        out_shape=(jax.ShapeDtypeStruct((l, d // 2), U32),
                   jax.ShapeDtypeStruct((l, d), F32),
                   jax.ShapeDtypeStruct((N_EXPERTS, l), F32)),
        grid=(l // tm,),
        in_specs=[pl.BlockSpec((tm, d), lambda i: (i, 0)),
                  pl.BlockSpec((tm, d), lambda i: (i, 0)),
                  pl.BlockSpec((1, V7X_SUBLANES, d), lambda i: (0, 0, 0)),
                  pl.BlockSpec((1, d), lambda i: (0, 0)),
                  pl.BlockSpec((1, d), lambda i: (0, 0)),
                  pl.BlockSpec((N_EXPERTS, d), lambda i: (0, 0))],
        out_specs=(pl.BlockSpec((tm, d // 2), lambda i: (i, 0)),
                   pl.BlockSpec((tm, d), lambda i: (i, 0)),
                   pl.BlockSpec((N_EXPERTS, tm), lambda i: (0, i))),
        scratch_shapes=[pltpu.VMEM((tm, d), BF16)],
        compiler_params=_cparams(("arbitrary",), 48),
        name="post_mix",
    )(x2, mix, mods, ln_gain, ln_bias, wr_t)


def _topk_kernel(aff_ref, triu_ref, slow_ref, idx_ref, gate_ref, cum_scr, *, cap):
    n_e, n_blk, n_lane = aff_ref.shape
    aff = aff_ref[...]
    bits = lax.bitcast_convert_type(aff, I32)

    def count(mask):
        c = jnp.sum(jnp.where(mask, 1.0, 0.0), axis=1, keepdims=True)
        return jnp.sum(c, axis=2, keepdims=True)

    def search(k, thr):
        cand = thr | jnp.left_shift(jnp.int32(1), 30 - k)
        return jnp.where(count(bits >= cand) >= cap, cand, thr)

    thr = lax.fori_loop(0, 31, search, jnp.zeros((n_e, 1, 1), I32))
    above = bits > thr
    equal = bits == thr
    need = cap - count(above)

    def prefix(mask):
        m = jnp.where(mask, 1.0, 0.0).astype(BF16)
        within = jnp.dot(m.reshape(n_e * n_blk, n_lane), triu_ref[...],
                         preferred_element_type=F32).reshape(n_e, n_blk, n_lane)
        tot = jnp.broadcast_to(within[:, :, n_lane - 1:n_lane], (n_e, n_blk, n_lane)).astype(BF16)
        offs = [jnp.dot(slow_ref[...], tot[e], preferred_element_type=F32) for e in range(n_e)]
        return within + jnp.stack(offs, axis=0)

    cum_above = prefix(above)
    cum_equal = prefix(equal)
    chosen = jnp.logical_or(above, jnp.logical_and(equal, cum_equal <= need))
    cum = cum_above + jnp.minimum(cum_equal, need)
    cum_scr[...] = jnp.where(chosen, cum, -1.0)

    jt = SLOT_TILE
    lane_f =lax.broadcasted_iota(I32, (jt, n_lane), 1).astype(F32)

    def per_expert(e, carry):
        def per_tile(t, carry2):
            slot = (lax.broadcasted_iota(I32, (jt, n_lane), 0) + (t * jt + 1)).astype(F32)
            blk_acc = jnp.zeros((jt, n_lane), F32)
            gate_acc = jnp.zeros((jt, n_lane), F32)
            for b in range(n_blk):
                hit = cum_scr[e, b:b + 1, :] == slot
                blk_acc = blk_acc + jnp.where(hit, float(b + 1), 0.0)
                gate_acc = gate_acc + jnp.where(hit, aff_ref[e, b:b + 1, :], 0.0)
            tok = jnp.where(blk_acc > 0.0, (blk_acc - 1.0) * n_lane + lane_f, 0.0)
            tok = jnp.sum(tok, axis=1, keepdims=True)
            gate = jnp.sum(gate_acc, axis=1, keepdims=True)
            rows = pl.ds(pl.multiple_of(t * jt, jt), jt)
            idx_ref[e, rows, :] = jnp.broadcast_to(tok, (jt, n_lane)).astype(I32)
            gate_ref[e, rows, :] = jnp.broadcast_to(gate, (jt, n_lane))
            return carry2
        return lax.fori_loop(0, cap // jt, per_tile, carry)

    lax.fori_loop(0, n_e, per_expert, 0)


def _expert_choice_topk(aff_t, cap):
    n_e, l = aff_t.shape
    n_blk = l // V7X_LANES
    aff3 = aff_t.reshape(n_e, n_blk, V7X_LANES)
    triu = (jnp.arange(V7X_LANES)[:, None] <= jnp.arange(V7X_LANES)[None, :]).astype(BF16)
    slow = (jnp.arange(n_blk)[:, None] > jnp.arange(n_blk)[None, :]).astype(BF16)
    return pl.pallas_call(
        functools.partial(_topk_kernel, cap=cap),
        out_shape=(jax.ShapeDtypeStruct((n_e, cap, V7X_LANES), I32),
                   jax.ShapeDtypeStruct((n_e, cap, V7X_LANES), F32)),
        grid=(1,),
        in_specs=[pl.BlockSpec((n_e, n_blk, V7X_LANES), lambda i: (0, 0, 0)),
                  pl.BlockSpec((V7X_LANES, V7X_LANES), lambda i: (0, 0)),
                  pl.BlockSpec((n_blk, n_blk), lambda i: (0, 0))],
        out_specs=(pl.BlockSpec((n_e, cap, V7X_LANES), lambda i: (0, 0, 0)),
                   pl.BlockSpec((n_e, cap, V7X_LANES), lambda i: (0, 0, 0))),
        scratch_shapes=[pltpu.VMEM((n_e, n_blk, V7X_LANES), F32)],
        compiler_params=_cparams(("arbitrary",), 48),
        name="expert_choice_topk",
    )(aff3, triu, slow)


def _row_copy(src, dst, src_row, dst_row, sem):
    return pltpu.make_async_copy(src.at[pl.ds(src_row, 1), :], dst.at[pl.ds(dst_row, 1), :], sem)


def _for_rows(n_rows, fn):
    def body(i, carry):
        for u in range(DMA_UNROLL):
            fn(i * DMA_UNROLL + u)
        return carry
    lax.fori_loop(0, n_rows // DMA_UNROLL, body, 0)


def _wait_rows_in(hbm, buf, sem):
    pltpu.make_async_copy(hbm.at[pl.ds(0, buf.shape[0]), :], buf, sem).wait()


def _wait_rows_out(buf, hbm, sem):
    pltpu.make_async_copy(buf, hbm.at[pl.ds(0, buf.shape[0]), :], sem).wait()


def _ffn_kernel(idx_ref, hfp, wg_ref, wu_ref, wd_ref, gate_ref, gf_ref, o_ref,
                xe_scr, h_scr, stage, sems, *, n_ff_tiles, n_experts, n_steps, cap):
    e = pl.program_id(0)
    s = pl.program_id(1)
    rows = cap // n_steps
    half = D_MODEL // 2
    tf = wg_ref.shape[2]

    def issue(expert, chunk, slot):
        base = expert * cap + chunk * rows
        _for_rows(rows, lambda j: _row_copy(hfp, stage.at[slot], idx_ref[base + j], j, sems.at[slot]).start())

    def land(buf, chunk, slot):
        _wait_rows_in(hfp, stage.at[slot], sems.at[slot])
        w = stage[slot]
        first = chunk * rows
        r = pl.ds(first if isinstance(first, int) else pl.multiple_of(first, rows), rows)
        xe_scr[buf, r, 0:half] = lax.bitcast_convert_type(w << 16, F32).astype(BF16)
        xe_scr[buf, r, half:] = lax.bitcast_convert_type(w & jnp.uint32(0xFFFF0000), F32).astype(BF16)

    @pl.when(jnp.logical_and(e == 0, s == 0))
    def _():
        issue(0, 0, 0)
        for k in range(n_steps):
            if k + 1 < n_steps:
                issue(0, k + 1, (k + 1) % 2)
            land(0, k, k % 2)

    @pl.when(jnp.logical_and(e > 0, s == 0))
    def _():
        land(e % 2, n_steps - 1, (n_steps - 1) % 2)

    @pl.when(jnp.logical_and(e + 1 < n_experts, s > 0))
    def _():
        land((e + 1) % 2, s - 1, (s - 1) % 2)

    @pl.when(e + 1 < n_experts)
    def _():
        issue(e + 1, s, s % 2)

    @pl.when(s < n_ff_tiles)
    def _():
        xe = xe_scr[e % 2]
        a = jnp.dot(xe, wg_ref[0].astype(BF16), preferred_element_type=F32)
        u = jnp.dot(xe, wu_ref[0].astype(BF16), preferred_element_type=F32)
        h_scr[:, pl.ds(pl.multiple_of(s * tf, tf), tf)] = (jax.nn.silu(a) * u).astype(BF16)

    @pl.when(s >= n_ff_tiles)
    def _():
        y = jnp.dot(h_scr[...], wd_ref[0].astype(BF16), preferred_element_type=F32)
        o_ref[0] = y * gate_ref[0, :, 0:1] * gf_ref[...]


def _expert_ffn(idx_flat, hf_packed, w_gate, w_up, w_down, gates, gate_f, cap, tf, tn):
    n_e, d, ff = w_gate.shape
    n_ff = ff // tf
    n_out = d // tn
    n_steps = n_ff + n_out
    rows = cap // n_steps
    col_a = lambda e, s, idx: (e, 0, jnp.minimum(s, n_ff - 1))
    col_b = lambda e, s, idx: (e, 0, jnp.maximum(s - n_ff, 0))
    return pl.pallas_call(
        functools.partial(_ffn_kernel, n_ff_tiles=n_ff, n_experts=n_e, n_steps=n_steps, cap=cap),
        out_shape=jax.ShapeDtypeStruct((n_e, cap, d), F32),
        grid_spec=pltpu.PrefetchScalarGridSpec(
            num_scalar_prefetch=1, grid=(n_e, n_steps),
            in_specs=[pl.BlockSpec(memory_space=pl.ANY),
                      pl.BlockSpec((1, d, tf), col_a),
                      pl.BlockSpec((1, d, tf), col_a),
                      pl.BlockSpec((1, ff, tn), col_b),
                      pl.BlockSpec((1, cap, V7X_LANES), lambda e, s, idx: (e, 0, 0), pipeline_mode=pl.Buffered(1)),
                      pl.BlockSpec((1, tn), lambda e, s, idx: (0, jnp.maximum(s - n_ff, 0)))],
            out_specs=pl.BlockSpec((1, cap, tn), col_b),
            scratch_shapes=[pltpu.VMEM((2, cap, d), BF16), pltpu.VMEM((cap, ff), BF16),
                            pltpu.VMEM((2, rows, d // 2), U32), pltpu.SemaphoreType.DMA((2,))]),
        compiler_params=_cparams(("arbitrary", "arbitrary"), 60),
        name="expert_ffn",
    )(idx_flat, hf_packed, w_gate, w_up, w_down, gates, gate_f)


def _combine_kernel(idx_ref, ye_ref, acc_in, acc_out, buf, gsem, ssem, *, cap, n_tiles):
    e = pl.program_id(0)
    m = pl.program_id(1)
    n_buf, tm, _ = buf.shape

    def fetch(tile, b):
        base = e * cap + tile * tm
        _for_rows(tm, lambda j: _row_copy(acc_in, buf.at[b], idx_ref[base + j], j, gsem.at[b]).start())

    def put(tile, b):
        base = e * cap + tile * tm
        _for_rows(tm, lambda j: _row_copy(buf.at[b], acc_out, j, idx_ref[base + j], ssem.at[b]).start())

    for t in range(n_tiles):
        @pl.when(m == t)
        def _(t=t):
            b = t % n_buf
            if t == 0:
                fetch(0, 0)
                if n_tiles > 1:
                    fetch(1, 1 % n_buf)
            elif t + 1 < n_tiles:
                nb = (t + 1) % n_buf
                if t + 1 >= n_buf:
                    _wait_rows_out(buf.at[nb], acc_out, ssem.at[nb])
                fetch(t + 1, nb)
            _wait_rows_in(acc_in, buf.at[b], gsem.at[b])
            buf[b] = buf[b] + ye_ref[0]
            put(t, b)
            if t == n_tiles - 1:
                for u in range(max(0, n_tiles - n_buf), n_tiles):
                    _wait_rows_out(buf.at[u % n_buf], acc_out, ssem.at[u % n_buf])


def _combine(idx_flat, ye, acc, tm, n_buf):
    n_e, cap, d = ye.shape
    n_tiles = cap // tm
    return pl.pallas_call(
        functools.partial(_combine_kernel, cap=cap, n_tiles=n_tiles),
        out_shape=jax.ShapeDtypeStruct(acc.shape, acc.dtype),
        grid_spec=pltpu.PrefetchScalarGridSpec(
            num_scalar_prefetch=1, grid=(n_e, n_tiles),
            in_specs=[pl.BlockSpec((1, tm, d), lambda e, m, idx: (e, m, 0)),
                      pl.BlockSpec(memory_space=pl.ANY)],
            out_specs=pl.BlockSpec(memory_space=pl.ANY),
            scratch_shapes=[pltpu.VMEM((n_buf, tm, d), F32),
                            pltpu.SemaphoreType.DMA((n_buf,)), pltpu.SemaphoreType.DMA((n_buf,))]),
        input_output_aliases={2: 0},
        compiler_params=_cparams(("arbitrary", "arbitrary"), 32),
        name="combine",
    )(idx_flat, ye, acc)


def _final_ln_kernel(a_ref, g_ref, b_ref, o_ref):
    def body(r, carry):
        sl = pl.ds(pl.multiple_of(r * ROW_STRIP, ROW_STRIP), ROW_STRIP)
        o_ref[sl, :] = _ln_rows(a_ref[sl, :]) * g_ref[...] + b_ref[...]
        return carry
    lax.fori_loop(0, a_ref.shape[0] // ROW_STRIP, body, 0)


def _final_ln(acc, gain, bias, tm):
    l, d = acc.shape
    return pl.pallas_call(
        _final_ln_kernel,
        out_shape=jax.ShapeDtypeStruct((l, d), F32),
        grid=(l // tm,),
        in_specs=[pl.BlockSpec((tm, d), lambda i: (i, 0)),
                  pl.BlockSpec((1, d), lambda i: (0, 0)),
                  pl.BlockSpec((1, d), lambda i: (0, 0))],
        out_specs=pl.BlockSpec((tm, d), lambda i: (i, 0)),
        compiler_params=_cparams(("arbitrary",), 32),
        name="final_ln",
    )(acc, gain, bias)


def _pad_rows(a, rows):
    return jnp.pad(a, ((0, rows - a.shape[0]), (0, 0)))


def _pad_lanes(a, lanes):
    return jnp.pad(a, ((0, 0), (0, lanes - a.shape[1])))


def kernel(x, c, ctx, c_ctx, w_ada, b_ada, w_in, gm_v_gain, gm_v_bias, gm_w_s, gm_b_s, ssm_conv_w, ssm_conv_b, ssm_dt_bias, ssm_a_log, ssm_d, ssm_norm_gain, w_out, ln_mix_gain, ln_mix_bias, w_router, w_gate, w_up, w_down, ln_ffn_gain, ln_ffn_bias):
    b, l, d = x.shape
    lc = ctx.shape[1]
    assert b == 1 and d == D_MODEL and w_ada.shape[0] == DEPTH
    x2 = x[0]
    ctx2 = ctx[0]
    layer = 0

    cond = _pad_rows(jnp.concatenate([c, c_ctx[None, :]], axis=0), V7X_SUBLANES)
    ada = _ada_params(cond, w_ada[layer], b_ada[layer][None, :])
    mods = jnp.pad(ada[:2].reshape(2, 6, d), ((0, 0), (0, V7X_SUBLANES - 6), (0, 0)))

    wdt = _pad_lanes(w_in[layer, :, COL_DT:], V7X_LANES).astype(BF16)
    dt_bias = _pad_lanes(ssm_dt_bias[layer].reshape(1, 2 * SSM_HEADS), V7X_LANES)
    a_log = _pad_lanes(ssm_a_log[layer].reshape(1, 2 * SSM_HEADS), V7X_LANES)
    h, dtda = _modulate_in(x2, ctx2, mods, wdt, dt_bias, a_log)
    proj = _matmul(h, w_in, COL_DT, tm=768, tn=512, name="in_proj")

    conv_w8 = _pad_rows(ssm_conv_w[layer], V7X_SUBLANES)
    conv = _conv_silu(proj, conv_w8, ssm_conv_b[layer][None, :], rb=lc, tc=1024)

    n_lat_chunks = l // CHUNK
    n_ctx_chunks = lc // SSD_CHUNK
    gm = _chunk_mlp(proj, lc, n_lat_chunks, gm_v_gain[layer][None, :], gm_v_bias[layer][None, :],
                    gm_w_s[layer].astype(BF16), gm_b_s[layer].T)

    y_fwd = _ssd_forward(conv, dtda, n_ctx_chunks, n_lat_chunks)
    d_skip = jnp.repeat(ssm_d[layer], SSM_HEAD_DIM)[None, :]
    ss = _ssd_backward(conv, dtda, y_fwd, proj, d_skip, ssm_norm_gain[layer][None, :],
                       n_ctx_chunks, n_lat_chunks)

    mix = _matmul_concat(gm, ss, w_out[layer], tm=1024, tn=512, name="out_proj")

    hf_packed, acc, aff_t = _post_mix(x2, mix, mods, ln_mix_gain[layer][None, :], ln_mix_bias[layer][None, :],
                                      w_router[layer].T.astype(BF16), tm=256)

    cap = EC_FACTOR * l // N_EXPERTS
    idx3, gates = _expert_choice_topk(aff_t, cap)
    idx_flat = idx3[:, :, 0].reshape(N_EXPERTS * cap)

    gate_f = mods[0, 5:6, :]
    ye = _expert_ffn(idx_flat, hf_packed, w_gate[layer], w_up[layer], w_down[layer], gates, gate_f,
                     cap, tf=256, tn=512)
    acc = _combine(idx_flat, ye, acc, tm=256, n_buf=3)
    out = _final_ln(acc, ln_ffn_gain[layer][None, :], ln_ffn_bias[layer][None, :], tm=256)
    return out[None]
```

```python
import functools

import jax
import jax.numpy as jnp
from jax import lax
from jax.experimental import pallas as pl
from jax.experimental.pallas import tpu as pltpu

F32 = jnp.float32
BF16 = jnp.bfloat16
I32 = jnp.int32
U32 = jnp.uint32

D_MODEL = 4096
GRID_W = 64
CHUNK = 128
GM_WIDTH = 2048
GM_HEADS = 8
GM_HEAD_DIM = GM_WIDTH // GM_HEADS
SSM_WIDTH = 2048
SSM_HEAD_DIM = 64
SSM_HEADS = SSM_WIDTH // SSM_HEAD_DIM
SSM_GROUPS = 8
SSM_STATE = 128
SSM_CONV = 5
SSM_XBC = SSM_WIDTH + 2 * SSM_GROUPS * SSM_STATE
SSD_CHUNK = 128
HEADS_PER_GROUP = SSM_HEADS // SSM_GROUPS
GROUP_WIDTH = SSM_WIDTH // SSM_GROUPS
N_EXPERTS = 16
EXPERT_FF = 2048
EC_FACTOR = 2
DEPTH = 1
ALPHA = (2 * DEPTH) ** 0.25
LN_EPS = 1e-5
RMS_EPS = 1e-5

COL_Z = 2 * GM_WIDTH
COL_XBC = COL_Z + SSM_WIDTH
COL_DT = COL_XBC + SSM_XBC

V7X_LANES = 128
V7X_SUBLANES = 8
V7X_VMEM_BYTES = 64 * 1024 * 1024
MIB = 1024 * 1024

ROW_STRIP = 32
SLOT_TILE = 128


def _cparams(semantics, vmem_mib):
    return pltpu.CompilerParams(dimension_semantics=semantics,
                                vmem_limit_bytes=min(vmem_mib * MIB, V7X_VMEM_BYTES - 2 * MIB))


def _ln_rows(x):
    mu = jnp.mean(x, axis=-1, keepdims=True)
    xc = x - mu
    var = jnp.mean(xc * xc, axis=-1, keepdims=True)
    return xc * lax.rsqrt(var + LN_EPS)


def _ada_kernel(c_ref, w_ref, b_ref, o_ref):
    s = jax.nn.silu(c_ref[...]).astype(BF16)
    o_ref[...] = jnp.dot(s, w_ref[...].astype(BF16), preferred_element_type=F32) + b_ref[...]


def _ada_params(cond, w_ada, b_ada):
    k, n = w_ada.shape
    tn = 512
    return pl.pallas_call(
        _ada_kernel,
        out_shape=jax.ShapeDtypeStruct((V7X_SUBLANES, n), F32),
        grid=(n // tn,),
        in_specs=[pl.BlockSpec((V7X_SUBLANES, k), lambda j: (0, 0)),
                  pl.BlockSpec((k, tn), lambda j: (0, j)),
                  pl.BlockSpec((1, tn), lambda j: (0, j))],
        out_specs=pl.BlockSpec((V7X_SUBLANES, tn), lambda j: (0, j)),
        compiler_params=_cparams(("arbitrary",), 32),
        name="ada_params",
    )(cond, w_ada, b_ada)


def _modin_kernel(x_ref, ctx_ref, p_ref, wdt_ref, bias_ref, alog_ref, h_ref, dtda_ref):
    i = pl.program_id(0)
    shift = p_ref[0, 0:1, :]
    scale1 = 1.0 + p_ref[0, 1:2, :]
    tm = h_ref.shape[0]

    def rows(src_ref):
        def body(r, carry):
            sl = pl.ds(pl.multiple_of(r * ROW_STRIP, ROW_STRIP), ROW_STRIP)
            h_ref[sl, :] = (_ln_rows(src_ref[sl, :]) * scale1 + shift).astype(BF16)
            return carry
        lax.fori_loop(0, tm // ROW_STRIP, body, 0)

    @pl.when(i == 0)
    def _():
        rows(ctx_ref)

    @pl.when(i > 0)
    def _():
        rows(x_ref)

    raw = lax.dot_general(h_ref[...], wdt_ref[...], (((1,), (1,)), ((), ())),
                          preferred_element_type=F32) + bias_ref[...]
    dt = jnp.maximum(raw, 0.0) + jnp.log1p(jnp.exp(-jnp.abs(raw)))
    dtda_ref[:, 0:V7X_LANES] = dt
    dtda_ref[:, V7X_LANES:2 * V7X_LANES] = dt * (-jnp.exp(alog_ref[...]))


def _modulate_in(x2, ctx2, mods, wdt, dt_bias, a_log):
    l, d = x2.shape
    lc = ctx2.shape[0]
    tm = lc
    n_tiles = (l + lc) // tm
    return pl.pallas_call(
        _modin_kernel,
        out_shape=(jax.ShapeDtypeStruct((l + lc, d), BF16),
                   jax.ShapeDtypeStruct((l + lc, 2 * V7X_LANES), F32)),
        grid=(n_tiles,),
        in_specs=[pl.BlockSpec((tm, d), lambda i: (jnp.maximum(i - 1, 0), 0)),
                  pl.BlockSpec((tm, d), lambda i: (0, 0)),
                  pl.BlockSpec((1, V7X_SUBLANES, d), lambda i: (jnp.where(i == 0, 1, 0), 0, 0)),
                  pl.BlockSpec((V7X_LANES, d), lambda i: (0, 0)),
                  pl.BlockSpec((1, V7X_LANES), lambda i: (0, 0)),
                  pl.BlockSpec((1, V7X_LANES), lambda i: (0, 0))],
        out_specs=(pl.BlockSpec((tm, d), lambda i: (i, 0)),
                   pl.BlockSpec((tm, 2 * V7X_LANES), lambda i: (i, 0))),
        compiler_params=_cparams(("arbitrary",), 40),
        name="modulate_in",
    )(x2, ctx2, mods, wdt, dt_bias, a_log)


def _mm_nt_kernel(a_ref, wt_ref, o_ref, wbf_ref):
    @pl.when(pl.program_id(1) == 0)
    def _():
        wbf_ref[...] = wt_ref[...].astype(BF16)

    o_ref[...] = lax.dot_general(a_ref[...], wbf_ref[...], (((1,), (1,)), ((), ())),
                                 preferred_element_type=F32)


def _matmul_nt(a, wt3, n_out, tm, tn, name):
    m, k = a.shape
    return pl.pallas_call(
        _mm_nt_kernel,
        out_shape=jax.ShapeDtypeStruct((m, n_out), F32),
        grid=(n_out // tn, m // tm),
        in_specs=[pl.BlockSpec((tm, k), lambda j, i: (i, 0)),
                  pl.BlockSpec((None, tn, k), lambda j, i: (0, j, 0))],
        out_specs=pl.BlockSpec((tm, tn), lambda j, i: (i, j)),
        scratch_shapes=[pltpu.VMEM((tn, k), BF16)],
        compiler_params=_cparams(("arbitrary", "arbitrary"), 56),
        name=name,
    )(a, wt3)


def _mm2_kernel(a0_ref, a1_ref, w_ref, o_ref, wbf_ref):
    @pl.when(pl.program_id(1) == 0)
    def _():
        wbf_ref[...] = w_ref[...].astype(BF16)

    k0 = a0_ref.shape[1]
    acc = jnp.dot(a0_ref[...], wbf_ref[0:k0, :], preferred_element_type=F32)
    o_ref[...] = acc + jnp.dot(a1_ref[...], wbf_ref[k0:, :], preferred_element_type=F32)


def _matmul_concat(a0, a1, w, tm, tn, name):
    m, k0 = a0.shape
    k1 = a1.shape[1]
    n_out = w.shape[1]
    return pl.pallas_call(
        _mm2_kernel,
        out_shape=jax.ShapeDtypeStruct((m, n_out), F32),
        grid=(n_out // tn, m // tm),
        in_specs=[pl.BlockSpec((tm, k0), lambda j, i: (i, 0)),
                  pl.BlockSpec((tm, k1), lambda j, i: (i, 0)),
                  pl.BlockSpec((k0 + k1, tn), lambda j, i: (0, j))],
        out_specs=pl.BlockSpec((tm, tn), lambda j, i: (i, j)),
        scratch_shapes=[pltpu.VMEM((k0 + k1, tn), BF16)],
        compiler_params=_cparams(("arbitrary", "arbitrary"), 56),
        name=name,
    )(a0, a1, w)


def _conv_kernel(main_ref, prev_ref, next_ref, w_ref, b_ref, o_ref, ext_ref):
    i = pl.program_id(0)
    rb = main_ref.shape[0]
    halo = V7X_SUBLANES
    pad = (SSM_CONV - 1) // 2
    seq_start = jnp.logical_or(i == 0, i == 1)
    seq_end = jnp.logical_or(i == 0, i == pl.num_programs(0) - 1)
    ext_ref[0:halo, :] = jnp.where(seq_start, 0.0, prev_ref[...])
    ext_ref[halo:halo + rb, :] = main_ref[...]
    ext_ref[halo + rb:2 * halo + rb, :] = jnp.where(seq_end, 0.0, next_ref[...])
    acc = jnp.broadcast_to(b_ref[...], o_ref.shape)
    for k in range(SSM_CONV):
        lo = halo - pad + k
        acc = acc + w_ref[k:k + 1, :] * ext_ref[lo:lo + rb, :]
    o_ref[...] = acc * jax.nn.sigmoid(acc)


def _conv_silu(proj, conv_w8, conv_b, rb, tc):
    rows = proj.shape[0]
    n_row_tiles = rows // rb
    halo_per_tile = rb // V7X_SUBLANES
    n_halo_blocks = rows // V7X_SUBLANES
    col0 = COL_XBC // tc
    return pl.pallas_call(
        _conv_kernel,
        out_shape=jax.ShapeDtypeStruct((rows, SSM_XBC), F32),
        grid=(n_row_tiles, SSM_XBC // tc),
        in_specs=[pl.BlockSpec((rb, tc), lambda i, j: (i, col0 + j)),
                  pl.BlockSpec((V7X_SUBLANES, tc),
                               lambda i, j: (jnp.maximum(i * halo_per_tile - 1, 0), col0 + j)),
                  pl.BlockSpec((V7X_SUBLANES, tc),
                               lambda i, j: (jnp.minimum((i + 1) * halo_per_tile, n_halo_blocks - 1), col0 + j)),
                  pl.BlockSpec((V7X_SUBLANES, tc), lambda i, j: (0, j)),
                  pl.BlockSpec((1, tc), lambda i, j: (0, j))],
        out_specs=pl.BlockSpec((rb, tc), lambda i, j: (i, j)),
        scratch_shapes=[pltpu.VMEM((rb + 2 * V7X_SUBLANES, tc), F32)],
        compiler_params=_cparams(("arbitrary", "arbitrary"), 32),
        name="conv_silu",
    )(proj, proj, proj, conv_w8, conv_b)


def _gmlp_kernel(uv_ref, gain_ref, bias_ref, ws_ref, bst_ref, o_ref):
    g = jax.nn.gelu(uv_ref[...])
    u = g[:, 0:GM_WIDTH]
    v = _ln_rows(g[:, GM_WIDTH:2 * GM_WIDTH]) * gain_ref[...] + bias_ref[...]
    vb = v.astype(BF16)
    for h in range(GM_HEADS):
        cols = slice(h * GM_HEAD_DIM, (h + 1) * GM_HEAD_DIM)
        mixed = jnp.dot(ws_ref[h], vb[:, cols], preferred_element_type=F32) + bst_ref[:, h:h + 1]
        o_ref[:, cols] = (u[:, cols] * mixed).astype(BF16)


def _chunk_mlp(proj, lat_row0, n_chunks, v_gain, v_bias, ws_bf, bs_t):
    blk0 = lat_row0 // CHUNK
    return pl.pallas_call(
        _gmlp_kernel,
        out_shape=jax.ShapeDtypeStruct((n_chunks * CHUNK, GM_WIDTH), BF16),
        grid=(n_chunks,),
        in_specs=[pl.BlockSpec((CHUNK, 2 * GM_WIDTH), lambda c: (blk0 + c, 0)),
                  pl.BlockSpec((1, GM_WIDTH), lambda c: (0, 0)),
                  pl.BlockSpec((1, GM_WIDTH), lambda c: (0, 0)),
                  pl.BlockSpec((GM_HEADS, CHUNK, CHUNK), lambda c: (0, 0, 0)),
                  pl.BlockSpec((CHUNK, GM_HEADS), lambda c: (0, 0))],
        out_specs=pl.BlockSpec((CHUNK, GM_WIDTH), lambda c: (c, 0)),
        compiler_params=_cparams(("arbitrary",), 32),
        name="chunk_mlp",
    )(proj, v_gain, v_bias, ws_bf, bs_t)


def _ssd_chunk(direction, xs_ref, b_ref, c_ref, dtda_ref, mcum_ref, expand_ref, st_ref, emit_group):
    d = direction
    lane0 = d * SSM_HEADS
    dt = dtda_ref[:, 0:V7X_LANES]
    d_a = dtda_ref[:, V7X_LANES:2 * V7X_LANES]
    cs = jnp.dot(mcum_ref[...], d_a, preferred_element_type=F32, precision=lax.Precision.HIGHEST)
    cs_t = cs.T
    last = SSD_CHUNK - 1 if d == 0 else 0
    cs_end = cs[last:last + 1, :]
    dt_decay = dt * jnp.exp(cs_end - cs)
    exp_cs = jnp.exp(cs)
    chunk_decay = jnp.broadcast_to(jnp.exp(cs_end), (V7X_SUBLANES, V7X_LANES))
    q = jnp.concatenate([dt, dt_decay, chunk_decay], axis=0)
    q_hi = q.astype(BF16)
    q_lo = (q - q_hi.astype(F32)).astype(BF16)

    row_i = lax.broadcasted_iota(I32, (SSD_CHUNK, SSD_CHUNK), 0)
    col_i = lax.broadcasted_iota(I32, (SSD_CHUNK, SSD_CHUNK), 1)
    causal = (row_i >= col_i) if d == 0 else (row_i <= col_i)
    first_head = col_i < SSM_HEAD_DIM

    for g in range(SSM_GROUPS):
        gcols = slice(g * GROUP_WIDTH, (g + 1) * GROUP_WIDTH)
        ncols = slice(g * SSM_STATE, (g + 1) * SSM_STATE)
        c32 = c_ref[:, ncols]
        b_bf = b_ref[:, ncols].astype(BF16)
        c_bf = c32.astype(BF16)
        cb = lax.dot_general(c_bf, b_bf, (((1,), (1,)), ((), ())), preferred_element_type=F32)
        e_g = expand_ref[:, gcols]
        q_e = (jnp.dot(q_hi, e_g, preferred_element_type=F32)
               + jnp.dot(q_lo, e_g, preferred_element_type=F32))
        xs_g = xs_ref[:, gcols]
        x_dt = (xs_g * q_e[0:SSD_CHUNK]).astype(BF16)
        x_dec = (xs_g * q_e[SSD_CHUNK:2 * SSD_CHUNK]).astype(BF16)
        h_old = st_ref[g]
        h_bf = h_old.astype(BF16)
        y_pairs = []
        for pr in range(HEADS_PER_GROUP // 2):
            pcols = slice(pr * V7X_LANES, (pr + 1) * V7X_LANES)
            xp = x_dt[:, pcols]
            hp = h_bf[:, pcols]
            zero = jnp.zeros_like(xp)
            rhs = jnp.concatenate([jnp.where(first_head, xp, zero), jnp.where(first_head, hp, zero),
                                   jnp.where(first_head, zero, xp), jnp.where(first_head, zero, hp)], axis=0)
            parts = []
            for k in range(2):
                lane = lane0 + g * HEADS_PER_GROUP + 2 * pr + k
                seg = jnp.exp(jnp.where(causal, cs[:, lane:lane + 1] - cs_t[lane:lane + 1, :], -jnp.inf))
                parts.append((cb * seg).astype(BF16))
                parts.append((c32 * exp_cs[:, lane:lane + 1]).astype(BF16))
            lhs = jnp.concatenate(parts, axis=1)
            y_pairs.append(jnp.dot(lhs, rhs, preferred_element_type=F32))
        emit_group(g, xs_g, jnp.concatenate(y_pairs, axis=1))
        s_new = lax.dot_general(b_bf, x_dec, (((0,), (0,)), ((), ())), preferred_element_type=F32)
        st_ref[g] = h_old * q_e[2 * SSD_CHUNK:2 * SSD_CHUNK + 1] + s_new


def _ssd_fwd_kernel(xs_ref, b_ref, c_ref, dtda_ref, mcum_ref, expand_ref, y_ref, st_ref):
    @pl.when(pl.program_id(0) == 0)
    def _():
        st_ref[...] = jnp.zeros_like(st_ref)

    def emit(g, xs_g, y_g):
        y_ref[:, g * GROUP_WIDTH:(g + 1) * GROUP_WIDTH] = y_g

    _ssd_chunk(0, xs_ref, b_ref, c_ref, dtda_ref, mcum_ref, expand_ref, st_ref, emit)


def _ssd_bwd_kernel(xs_ref, b_ref, c_ref, dtda_ref, mcum_ref, expand_ref, yf_ref, z_ref,
                    dskip_ref, ngain_ref, o_ref, st_ref):
    @pl.when(pl.program_id(0) == 0)
    def _():
        st_ref[...] = jnp.zeros_like(st_ref)

    def emit(g, xs_g, y_g):
        gcols = slice(g * GROUP_WIDTH, (g + 1) * GROUP_WIDTH)
        y = y_g + yf_ref[:, gcols] + dskip_ref[:, gcols] * xs_g
        gated = y * jax.nn.silu(z_ref[:, gcols])
        ms = jnp.mean(gated * gated, axis=-1, keepdims=True)
        o_ref[:, gcols] = (gated * lax.rsqrt(ms + RMS_EPS) * ngain_ref[:, gcols]).astype(BF16)

    _ssd_chunk(1, xs_ref, b_ref, c_ref, dtda_ref, mcum_ref, expand_ref, st_ref, emit)


def _ssd_specs(chunk_of_step):
    xs_blk = SSM_WIDTH
    n_blk = SSM_GROUPS * SSM_STATE
    return [pl.BlockSpec((SSD_CHUNK, xs_blk), lambda i: (chunk_of_step(i), 0)),
            pl.BlockSpec((SSD_CHUNK, n_blk), lambda i: (chunk_of_step(i), SSM_WIDTH // n_blk)),
            pl.BlockSpec((SSD_CHUNK, n_blk), lambda i: (chunk_of_step(i), SSM_WIDTH // n_blk + 1)),
            pl.BlockSpec((SSD_CHUNK, 2 * V7X_LANES), lambda i: (chunk_of_step(i), 0)),
            pl.BlockSpec((SSD_CHUNK, SSD_CHUNK), lambda i: (0, 0)),
            pl.BlockSpec((V7X_LANES, SSM_WIDTH), lambda i: (0, 0))]


def _ssd_forward(conv, dtda, n_ctx_chunks, n_lat_chunks):
    n_steps = n_ctx_chunks + n_lat_chunks
    tri = (jnp.arange(SSD_CHUNK)[:, None] >= jnp.arange(SSD_CHUNK)[None, :]).astype(F32)
    expand = _head_expand(0)
    return pl.pallas_call(
        _ssd_fwd_kernel,
        out_shape=jax.ShapeDtypeStruct((n_lat_chunks * SSD_CHUNK, SSM_WIDTH), F32),
        grid=(n_steps,),
        in_specs=_ssd_specs(lambda i: i),
        out_specs=pl.BlockSpec((SSD_CHUNK, SSM_WIDTH), lambda i: (jnp.maximum(i - n_ctx_chunks, 0), 0)),
        scratch_shapes=[pltpu.VMEM((SSM_GROUPS, SSM_STATE, GROUP_WIDTH), F32)],
        compiler_params=_cparams(("arbitrary",), 40),
        name="ssd_forward",
    )(conv, conv, conv, dtda, tri, expand)


def _ssd_backward(conv, dtda, y_fwd, proj, d_skip, norm_gain, n_ctx_chunks, n_lat_chunks):
    n_steps = n_ctx_chunks + n_lat_chunks
    tri = (jnp.arange(SSD_CHUNK)[:, None] <= jnp.arange(SSD_CHUNK)[None, :]).astype(F32)
    expand = _head_expand(1)

    def chunk_of_step(i):
        return jnp.where(i < n_ctx_chunks, n_ctx_chunks - 1 - i, n_steps - 1 + n_ctx_chunks - i)

    def lat_of_step(i):
        return jnp.where(i < n_ctx_chunks, n_lat_chunks - 1, n_steps - 1 - i)

    return pl.pallas_call(
        _ssd_bwd_kernel,
        out_shape=jax.ShapeDtypeStruct((n_lat_chunks * SSD_CHUNK, SSM_WIDTH), BF16),
        grid=(n_steps,),
        in_specs=_ssd_specs(chunk_of_step) + [
            pl.BlockSpec((SSD_CHUNK, SSM_WIDTH), lambda i: (lat_of_step(i), 0)),
            pl.BlockSpec((SSD_CHUNK, SSM_WIDTH), lambda i: (chunk_of_step(i), COL_Z // SSM_WIDTH)),
            pl.BlockSpec((1, SSM_WIDTH), lambda i: (0, 0)),
            pl.BlockSpec((1, SSM_WIDTH), lambda i: (0, 0))],
        out_specs=pl.BlockSpec((SSD_CHUNK, SSM_WIDTH), lambda i: (lat_of_step(i), 0)),
        scratch_shapes=[pltpu.VMEM((SSM_GROUPS, SSM_STATE, GROUP_WIDTH), F32)],
        compiler_params=_cparams(("arbitrary",), 40),
        name="ssd_backward",
    )(conv, conv, conv, dtda, tri, expand, y_fwd, proj, d_skip, norm_gain)


def _head_expand(direction):
    lane = jnp.arange(V7X_LANES)[:, None]
    chan = jnp.arange(SSM_WIDTH)[None, :]
    return (lane == direction * SSM_HEADS + chan // SSM_HEAD_DIM).astype(BF16)


def _postmix_kernel(x_ref, mix_ref, p_ref, lng_ref, lnb_ref, wrt_ref, hfp_ref, acc_ref, aff_ref, hf_scr):
    gate_m = p_ref[0, 2:3, :]
    shift_f = p_ref[0, 3:4, :]
    scale1_f = 1.0 + p_ref[0, 4:5, :]
    tm = x_ref.shape[0]
    half = D_MODEL // 2

    def body(r, carry):
        sl = pl.ds(pl.multiple_of(r * ROW_STRIP, ROW_STRIP), ROW_STRIP)
        xn = _ln_rows(ALPHA * x_ref[sl, :] + gate_m * mix_ref[sl, :]) * lng_ref[...] + lnb_ref[...]
        acc_ref[sl, :] = ALPHA * xn
        hf = (_ln_rows(xn) * scale1_f + shift_f).astype(BF16)
        hf_scr[sl, :] = hf
        lo = lax.bitcast_convert_type(hf[:, 0:half].astype(F32), U32)
        hi = lax.bitcast_convert_type(hf[:, half:].astype(F32), U32)
        hfp_ref[sl, :] = (lo >> 16) | (hi & jnp.uint32(0xFFFF0000))
        return carry

    lax.fori_loop(0, tm // ROW_STRIP, body, 0)
    logits = lax.dot_general(wrt_ref[...], hf_scr[...], (((1,), (1,)), ((), ())),
                             preferred_element_type=F32)
    mx = jnp.max(logits, axis=0, keepdims=True)
    ex = jnp.exp(logits - mx)
    aff_ref[...] = ex / jnp.sum(ex, axis=0, keepdims=True)


def _post_mix(x2, mix, mods, ln_gain, ln_bias, wr_t, tm):
    l, d = x2.shape
    return pl.pallas_call(
        _postmix_kernel,
        out_shape=(jax.ShapeDtypeStruct((l, d // 2), U32),
                   jax.ShapeDtypeStruct((l, d), F32),
                   jax.ShapeDtypeStruct((N_EXPERTS, l), F32)),
        grid=(l // tm,),
        in_specs=[pl.BlockSpec((tm, d), lambda i: (i, 0)),
                  pl.BlockSpec((tm, d), lambda i: (i, 0)),
                  pl.BlockSpec((1, V7X_SUBLANES, d), lambda i: (0, 0, 0)),
                  pl.BlockSpec((1, d), lambda i: (0, 0)),
                  pl.BlockSpec((1, d), lambda i: (0, 0)),
                  pl.BlockSpec((N_EXPERTS, d), lambda i: (0, 0))],
        out_specs=(pl.BlockSpec((tm, d // 2), lambda i: (i, 0)),
                   pl.BlockSpec((tm, d), lambda i: (i, 0)),
                   pl.BlockSpec((N_EXPERTS, tm), lambda i: (0, i))),
        scratch_shapes=[pltpu.VMEM((tm, d), BF16)],
        compiler_params=_cparams(("arbitrary",), 48),
        name="post_mix",
    )(x2, mix, mods, ln_gain, ln_bias, wr_t)


def _topk_kernel(aff_ref, triu_ref, slow_ref, idx_ref, gate_ref, cum_scr, *, cap):
    n_e, n_blk, n_lane = aff_ref.shape
    aff = aff_ref[...]
    bits = lax.bitcast_convert_type(aff, I32)

    def count(mask):
        c = jnp.sum(jnp.where(mask, 1.0, 0.0), axis=1, keepdims=True)
        return jnp.sum(c, axis=2, keepdims=True)

    def search(k, thr):
        cand = thr | jnp.left_shift(jnp.int32(1), 30 - k)
        return jnp.where(count(bits >= cand) >= cap, cand, thr)

    thr = lax.fori_loop(0, 31, search, jnp.zeros((n_e, 1, 1), I32))
    above = bits > thr
    equal = bits == thr
    need = cap - count(above)

    def prefix(mask):
        m = jnp.where(mask, 1.0, 0.0).astype(BF16)
        within = jnp.dot(m.reshape(n_e * n_blk, n_lane), triu_ref[...],
                         preferred_element_type=F32).reshape(n_e, n_blk, n_lane)
        tot = jnp.broadcast_to(within[:, :, n_lane - 1:n_lane], (n_e, n_blk, n_lane)).astype(BF16)
        offs = [jnp.dot(slow_ref[...], tot[e], preferred_element_type=F32) for e in range(n_e)]
        return within + jnp.stack(offs, axis=0)

    cum_above = prefix(above)
    cum_equal = prefix(equal)
    chosen = jnp.logical_or(above, jnp.logical_and(equal, cum_equal <= need))
    cum = cum_above + jnp.minimum(cum_equal, need)
    cum_scr[...] = jnp.where(chosen, cum, -1.0)

    jt = SLOT_TILE
    lane_f =lax.broadcasted_iota(I32, (jt, n_lane), 1).astype(F32)

    def per_expert(e, carry):
        def per_tile(t, carry2):
            slot = (lax.broadcasted_iota(I32, (jt, n_lane), 0) + (t * jt + 1)).astype(F32)
            blk_acc = jnp.zeros((jt, n_lane), F32)
            gate_acc = jnp.zeros((jt, n_lane), F32)
            for b in range(n_blk):
                hit = cum_scr[e, b:b + 1, :] == slot
                blk_acc = blk_acc + jnp.where(hit, float(b + 1), 0.0)
                gate_acc = gate_acc + jnp.where(hit, aff_ref[e, b:b + 1, :], 0.0)
            tok = jnp.where(blk_acc > 0.0, (blk_acc - 1.0) * n_lane + lane_f, 0.0)
            tok = jnp.sum(tok, axis=1, keepdims=True)
            gate = jnp.sum(gate_acc, axis=1, keepdims=True)
            rows = pl.ds(pl.multiple_of(t * jt, jt), jt)
            idx_ref[e, rows, :] = jnp.broadcast_to(tok, (jt, n_lane)).astype(I32)
            gate_ref[e, rows, :] = jnp.broadcast_to(gate, (jt, n_lane))
            return carry2
        return lax.fori_loop(0, cap // jt, per_tile, carry)

    lax.fori_loop(0, n_e, per_expert, 0)


def _expert_choice_topk(aff_t, cap):
    n_e, l = aff_t.shape
    n_blk = l // V7X_LANES
    aff3 = aff_t.reshape(n_e, n_blk, V7X_LANES)
    triu = (jnp.arange(V7X_LANES)[:, None] <= jnp.arange(V7X_LANES)[None, :]).astype(BF16)
    slow = (jnp.arange(n_blk)[:, None] > jnp.arange(n_blk)[None, :]).astype(BF16)
    return pl.pallas_call(
        functools.partial(_topk_kernel, cap=cap),
        out_shape=(jax.ShapeDtypeStruct((n_e, cap, V7X_LANES), I32),
                   jax.ShapeDtypeStruct((n_e, cap, V7X_LANES), F32)),
        grid=(1,),
        in_specs=[pl.BlockSpec((n_e, n_blk, V7X_LANES), lambda i: (0, 0, 0)),
                  pl.BlockSpec((V7X_LANES, V7X_LANES), lambda i: (0, 0)),
                  pl.BlockSpec((n_blk, n_blk), lambda i: (0, 0))],
        out_specs=(pl.BlockSpec((n_e, cap, V7X_LANES), lambda i: (0, 0, 0)),
                   pl.BlockSpec((n_e, cap, V7X_LANES), lambda i: (0, 0, 0))),
        scratch_shapes=[pltpu.VMEM((n_e, n_blk, V7X_LANES), F32)],
        compiler_params=_cparams(("arbitrary",), 48),
        name="expert_choice_topk",
    )(aff3, triu, slow)


def _row_copy(src, dst, src_row, dst_row, sem):
    return pltpu.make_async_copy(src.at[pl.ds(src_row, 1), :], dst.at[pl.ds(dst_row, 1), :], sem)


def _wait_rows_in(hbm, buf, sem):
    pltpu.make_async_copy(hbm.at[pl.ds(0, buf.shape[0]), :], buf, sem).wait()


def _wait_rows_out(buf, hbm, sem):
    pltpu.make_async_copy(buf, hbm.at[pl.ds(0, buf.shape[0]), :], sem).wait()


def _ffn_kernel(idx_ref, hfp, wg_ref, wu_ref, wd_ref, gate_ref, gf_ref, o_ref,
                xe_scr, h_scr, stage, sems, *, n_ff_tiles, n_experts, n_steps, cap):
    e = pl.program_id(0)
    s = pl.program_id(1)
    rows = cap // n_steps
    half = D_MODEL // 2
    tf = wg_ref.shape[2]

    def issue(expert, chunk, slot):
        base = expert * cap + chunk * rows
        for j in range(rows):
            _row_copy(hfp, stage.at[slot], idx_ref[base + j], j, sems.at[slot]).start()

    def land(buf, chunk, slot):
        _wait_rows_in(hfp, stage.at[slot], sems.at[slot])
        w = stage[slot]
        first = chunk * rows
        r = pl.ds(first if isinstance(first, int) else pl.multiple_of(first, rows), rows)
        xe_scr[buf, r, 0:half] = lax.bitcast_convert_type(w << 16, F32).astype(BF16)
        xe_scr[buf, r, half:] = lax.bitcast_convert_type(w & jnp.uint32(0xFFFF0000), F32).astype(BF16)

    @pl.when(jnp.logical_and(e == 0, s == 0))
    def _():
        issue(0, 0, 0)
        for k in range(n_steps):
            if k + 1 < n_steps:
                issue(0, k + 1, (k + 1) % 2)
            land(0, k, k % 2)

    @pl.when(jnp.logical_and(e > 0, s == 0))
    def _():
        land(e % 2, n_steps - 1, (n_steps - 1) % 2)

    @pl.when(jnp.logical_and(e + 1 < n_experts, s > 0))
    def _():
        land((e + 1) % 2, s - 1, (s - 1) % 2)

    @pl.when(e + 1 < n_experts)
    def _():
        issue(e + 1, s, s % 2)

    @pl.when(s < n_ff_tiles)
    def _():
        xe = xe_scr[e % 2]
        a = jnp.dot(xe, wg_ref[0].astype(BF16), preferred_element_type=F32)
        u = jnp.dot(xe, wu_ref[0].astype(BF16), preferred_element_type=F32)
        h_scr[:, pl.ds(pl.multiple_of(s * tf, tf), tf)] = (jax.nn.silu(a) * u).astype(BF16)

    @pl.when(s >= n_ff_tiles)
    def _():
        y = jnp.dot(h_scr[...], wd_ref[0].astype(BF16), preferred_element_type=F32)
        o_ref[0] = (y * gate_ref[0, :, 0:1] * gf_ref[...]).astype(BF16)


def _expert_ffn(idx_flat, hf_packed, w_gate, w_up, w_down, gates, gate_f, cap, tf, tn):
    n_e, d, ff = w_gate.shape
    n_ff = ff // tf
    n_out = d // tn
    n_steps = n_ff + n_out
    rows = cap // n_steps
    col_a = lambda e, s, idx: (e, 0, jnp.minimum(s, n_ff - 1))
    col_b = lambda e, s, idx: (e, 0, jnp.maximum(s - n_ff, 0))
    return pl.pallas_call(
        functools.partial(_ffn_kernel, n_ff_tiles=n_ff, n_experts=n_e, n_steps=n_steps, cap=cap),
        out_shape=jax.ShapeDtypeStruct((n_e, cap, d), BF16),
        grid_spec=pltpu.PrefetchScalarGridSpec(
            num_scalar_prefetch=1, grid=(n_e, n_steps),
            in_specs=[pl.BlockSpec(memory_space=pl.ANY),
                      pl.BlockSpec((1, d, tf), col_a),
                      pl.BlockSpec((1, d, tf), col_a),
                      pl.BlockSpec((1, ff, tn), col_b),
                      pl.BlockSpec((1, cap, V7X_LANES), lambda e, s, idx: (e, 0, 0), pipeline_mode=pl.Buffered(1)),
                      pl.BlockSpec((1, tn), lambda e, s, idx: (0, jnp.maximum(s - n_ff, 0)))],
            out_specs=pl.BlockSpec((1, cap, tn), col_b),
            scratch_shapes=[pltpu.VMEM((2, cap, d), BF16), pltpu.VMEM((cap, ff), BF16),
                            pltpu.VMEM((2, rows, d // 2), U32), pltpu.SemaphoreType.DMA((2,))]),
        compiler_params=_cparams(("arbitrary", "arbitrary"), 60),
        name="expert_ffn",
    )(idx_flat, hf_packed, w_gate, w_up, w_down, gates, gate_f)


def _combine_kernel(idx_ref, ye_ref, acc_in, acc_out, buf, gsem, ssem, *, cap, n_tiles):
    e = pl.program_id(0)
    m = pl.program_id(1)
    n_buf, n_grp, grp, _ = buf.shape
    tm = n_grp * grp

    def for_rows(tile, fn):
        def body(g, carry):
            first = e * cap + tile * tm + g * grp
            for u in range(grp):
                fn(g, u, idx_ref[first + u])
            return carry
        lax.fori_loop(0, n_grp, body, 0)

    def fetch(tile, b):
        for_rows(tile, lambda g, u, tok: _row_copy(acc_in, buf.at[b, g], tok, u, gsem.at[b]).start())

    def put(tile, b):
        for_rows(tile, lambda g, u, tok: _row_copy(buf.at[b, g], acc_out, u, tok, ssem.at[b]).start())

    def fetch_wait(b):
        for g in range(n_grp):
            _wait_rows_in(acc_in, buf.at[b, g], gsem.at[b])

    def put_wait(b):
        for g in range(n_grp):
            _wait_rows_out(buf.at[b, g], acc_out, ssem.at[b])

    for t in range(n_tiles):
        @pl.when(m == t)
        def _(t=t):
            b = t % n_buf
            if t == 0:
                fetch(0, 0)
                if n_tiles > 1:
                    fetch(1, 1 % n_buf)
            elif t + 1 < n_tiles:
                nb = (t + 1) % n_buf
                if t + 1 >= n_buf:
                    put_wait(nb)
                fetch(t + 1, nb)
            fetch_wait(b)
            for g in range(n_grp):
                buf[b, g] = buf[b, g] + ye_ref[0, g * grp:(g + 1) * grp, :].astype(F32)
            put(t, b)
            if t == n_tiles - 1:
                for u in range(max(0, n_tiles - n_buf), n_tiles):
                    put_wait(u % n_buf)


def _combine(idx_flat, ye, acc, tm, n_buf):
    n_e, cap, d = ye.shape
    n_tiles = cap // tm
    return pl.pallas_call(
        functools.partial(_combine_kernel, cap=cap, n_tiles=n_tiles),
        out_shape=jax.ShapeDtypeStruct(acc.shape, acc.dtype),
        grid_spec=pltpu.PrefetchScalarGridSpec(
            num_scalar_prefetch=1, grid=(n_e, n_tiles),
            in_specs=[pl.BlockSpec((1, tm, d), lambda e, m, idx: (e, m, 0)),
                      pl.BlockSpec(memory_space=pl.ANY)],
            out_specs=pl.BlockSpec(memory_space=pl.ANY),
            scratch_shapes=[pltpu.VMEM((n_buf, tm // ROW_STRIP, ROW_STRIP, d), F32),
                            pltpu.SemaphoreType.DMA((n_buf,)), pltpu.SemaphoreType.DMA((n_buf,))]),
        input_output_aliases={2: 0},
        compiler_params=_cparams(("arbitrary", "arbitrary"), 32),
        name="combine",
    )(idx_flat, ye, acc)


def _final_ln_kernel(a_ref, g_ref, b_ref, o_ref):
    def body(r, carry):
        sl = pl.ds(pl.multiple_of(r * ROW_STRIP, ROW_STRIP), ROW_STRIP)
        o_ref[sl, :] = _ln_rows(a_ref[sl, :]) * g_ref[...] + b_ref[...]
        return carry
    lax.fori_loop(0, a_ref.shape[0] // ROW_STRIP, body, 0)


def _final_ln(acc, gain, bias, tm):
    l, d = acc.shape
    return pl.pallas_call(
        _final_ln_kernel,
        out_shape=jax.ShapeDtypeStruct((l, d), F32),
        grid=(l // tm,),
        in_specs=[pl.BlockSpec((tm, d), lambda i: (i, 0)),
                  pl.BlockSpec((1, d), lambda i: (0, 0)),
                  pl.BlockSpec((1, d), lambda i: (0, 0))],
        out_specs=pl.BlockSpec((tm, d), lambda i: (i, 0)),
        compiler_params=_cparams(("arbitrary",), 32),
        name="final_ln",
    )(acc, gain, bias)


def _pad_rows(a, rows):
    return jnp.pad(a, ((0, rows - a.shape[0]), (0, 0)))


def _pad_lanes(a, lanes):
    return jnp.pad(a, ((0, 0), (0, lanes - a.shape[1])))


def kernel(x, c, ctx, c_ctx, w_ada, b_ada, w_in, gm_v_gain, gm_v_bias, gm_w_s, gm_b_s, ssm_conv_w, ssm_conv_b, ssm_dt_bias, ssm_a_log, ssm_d, ssm_norm_gain, w_out, ln_mix_gain, ln_mix_bias, w_router, w_gate, w_up, w_down, ln_ffn_gain, ln_ffn_bias):
    b, l, d = x.shape
    lc = ctx.shape[1]
    assert b == 1 and d == D_MODEL and w_ada.shape[0] == DEPTH
    x2 = x[0]
    ctx2 = ctx[0]
    layer = 0

    cond = _pad_rows(jnp.concatenate([c, c_ctx[None, :]], axis=0), V7X_SUBLANES)
    ada = _ada_params(cond, w_ada[layer], b_ada[layer][None, :])
    mods = jnp.pad(ada[:2].reshape(2, 6, d), ((0, 0), (0, V7X_SUBLANES - 6), (0, 0)))

    w_in_t = jnp.swapaxes(w_in, 1, 2)
    wdt = _pad_rows(w_in_t[layer, COL_DT:, :], V7X_LANES).astype(BF16)
    dt_bias = _pad_lanes(ssm_dt_bias[layer].reshape(1, 2 * SSM_HEADS), V7X_LANES)
    a_log = _pad_lanes(ssm_a_log[layer].reshape(1, 2 * SSM_HEADS), V7X_LANES)
    h, dtda = _modulate_in(x2, ctx2, mods, wdt, dt_bias, a_log)
    proj = _matmul_nt(h, w_in_t, COL_DT, tm=1408, tn=512, name="in_proj")

    conv_w8 = _pad_rows(ssm_conv_w[layer], V7X_SUBLANES)
    conv = _conv_silu(proj, conv_w8, ssm_conv_b[layer][None, :], rb=lc, tc=1024)

    n_lat_chunks = l // CHUNK
    n_ctx_chunks = lc // SSD_CHUNK
    gm = _chunk_mlp(proj, lc, n_lat_chunks, gm_v_gain[layer][None, :], gm_v_bias[layer][None, :],
                    gm_w_s[layer].astype(BF16), gm_b_s[layer].T)

    y_fwd = _ssd_forward(conv, dtda, n_ctx_chunks, n_lat_chunks)
    d_skip = jnp.repeat(ssm_d[layer], SSM_HEAD_DIM)[None, :]
    ss = _ssd_backward(conv, dtda, y_fwd, proj, d_skip, ssm_norm_gain[layer][None, :],
                       n_ctx_chunks, n_lat_chunks)

    mix = _matmul_concat(gm, ss, w_out[layer], tm=1024, tn=512, name="out_proj")

    hf_packed, acc, aff_t = _post_mix(x2, mix, mods, ln_mix_gain[layer][None, :], ln_mix_bias[layer][None, :],
                                      w_router[layer].T.astype(BF16), tm=256)

    cap = EC_FACTOR * l // N_EXPERTS
    idx3, gates = _expert_choice_topk(aff_t, cap)
    idx_flat = idx3[:, :, 0].reshape(N_EXPERTS * cap)

    gate_f = mods[0, 5:6, :]
    ye = _expert_ffn(idx_flat, hf_packed, w_gate[layer], w_up[layer], w_down[layer], gates, gate_f,
                     cap, tf=256, tn=512)
    acc = _combine(idx_flat, ye, acc, tm=256, n_buf=3)
    out = _final_ln(acc, ln_ffn_gain[layer][None, :], ln_ffn_bias[layer][None, :], tm=256)
    return out[None]
```

```python
import functools

import jax
import jax.numpy as jnp
from jax import lax
from jax.experimental import pallas as pl
from jax.experimental.pallas import tpu as pltpu

F32 = jnp.float32
BF16 = jnp.bfloat16
I32 = jnp.int32
U32 = jnp.uint32

D_MODEL = 4096
GRID_W = 64
CHUNK = 128
GM_WIDTH = 2048
GM_HEADS = 8
GM_HEAD_DIM = GM_WIDTH // GM_HEADS
SSM_WIDTH = 2048
SSM_HEAD_DIM = 64
SSM_HEADS = SSM_WIDTH // SSM_HEAD_DIM
SSM_GROUPS = 8
SSM_STATE = 128
SSM_CONV = 5
SSM_XBC = SSM_WIDTH + 2 * SSM_GROUPS * SSM_STATE
SSD_CHUNK = 128
HEADS_PER_GROUP = SSM_HEADS // SSM_GROUPS
GROUP_WIDTH = SSM_WIDTH // SSM_GROUPS
N_EXPERTS = 16
EXPERT_FF = 2048
EC_FACTOR = 2
DEPTH = 1
ALPHA = (2 * DEPTH) ** 0.25
LN_EPS = 1e-5
RMS_EPS = 1e-5

COL_Z = 2 * GM_WIDTH
COL_XBC = COL_Z + SSM_WIDTH
COL_DT = COL_XBC + SSM_XBC

V7X_LANES = 128
V7X_SUBLANES = 8
V7X_VMEM_BYTES = 64 * 1024 * 1024
MIB = 1024 * 1024

ROW_STRIP = 32
SLOT_TILE = 128
K_SPLIT = 2


def _cparams(semantics, vmem_mib):
    return pltpu.CompilerParams(dimension_semantics=semantics,
                                vmem_limit_bytes=min(vmem_mib * MIB, V7X_VMEM_BYTES - 2 * MIB))


def _ln_rows(x):
    mu = jnp.mean(x, axis=-1, keepdims=True)
    xc = x - mu
    var = jnp.mean(xc * xc, axis=-1, keepdims=True)
    return xc * lax.rsqrt(var + LN_EPS)


def _ada_kernel(c_ref, w_ref, b_ref, o_ref):
    s = jax.nn.silu(c_ref[...]).astype(BF16)
    o_ref[...] = jnp.dot(s, w_ref[...].astype(BF16), preferred_element_type=F32) + b_ref[...]


def _ada_params(cond, w_ada, b_ada):
    k, n = w_ada.shape
    tn = 512
    return pl.pallas_call(
        _ada_kernel,
        out_shape=jax.ShapeDtypeStruct((V7X_SUBLANES, n), F32),
        grid=(n // tn,),
        in_specs=[pl.BlockSpec((V7X_SUBLANES, k), lambda j: (0, 0)),
                  pl.BlockSpec((k, tn), lambda j: (0, j)),
                  pl.BlockSpec((1, tn), lambda j: (0, j))],
        out_specs=pl.BlockSpec((V7X_SUBLANES, tn), lambda j: (0, j)),
        compiler_params=_cparams(("arbitrary",), 32),
        name="ada_params",
    )(cond, w_ada, b_ada)


def _modin_kernel(x_ref, ctx_ref, p_ref, wdt_ref, bias_ref, alog_ref, h_ref, dtda_ref):
    i = pl.program_id(0)
    shift = p_ref[0, 0:1, :]
    scale1 = 1.0 + p_ref[0, 1:2, :]
    tm = h_ref.shape[0]

    def rows(src_ref):
        def body(r, carry):
            sl = pl.ds(pl.multiple_of(r * ROW_STRIP, ROW_STRIP), ROW_STRIP)
            h_ref[sl, :] = (_ln_rows(src_ref[sl, :]) * scale1 + shift).astype(BF16)
            return carry
        lax.fori_loop(0, tm // ROW_STRIP, body, 0)

    @pl.when(i == 0)
    def _():
        rows(ctx_ref)

    @pl.when(i > 0)
    def _():
        rows(x_ref)

    raw = lax.dot_general(h_ref[...], wdt_ref[...], (((1,), (1,)), ((), ())),
                          preferred_element_type=F32) + bias_ref[...]
    dt = jnp.maximum(raw, 0.0) + jnp.log1p(jnp.exp(-jnp.abs(raw)))
    dtda_ref[:, 0:V7X_LANES] = dt
    dtda_ref[:, V7X_LANES:2 * V7X_LANES] = dt * (-jnp.exp(alog_ref[...]))


def _modulate_in(x2, ctx2, mods, wdt, dt_bias, a_log):
    l, d = x2.shape
    lc = ctx2.shape[0]
    tm = lc
    n_tiles = (l + lc) // tm
    return pl.pallas_call(
        _modin_kernel,
        out_shape=(jax.ShapeDtypeStruct((l + lc, d), BF16),
                   jax.ShapeDtypeStruct((l + lc, 2 * V7X_LANES), F32)),
        grid=(n_tiles,),
        in_specs=[pl.BlockSpec((tm, d), lambda i: (jnp.maximum(i - 1, 0), 0)),
                  pl.BlockSpec((tm, d), lambda i: (0, 0)),
                  pl.BlockSpec((1, V7X_SUBLANES, d), lambda i: (jnp.where(i == 0, 1, 0), 0, 0)),
                  pl.BlockSpec((V7X_LANES, d), lambda i: (0, 0)),
                  pl.BlockSpec((1, V7X_LANES), lambda i: (0, 0)),
                  pl.BlockSpec((1, V7X_LANES), lambda i: (0, 0))],
        out_specs=(pl.BlockSpec((tm, d), lambda i: (i, 0)),
                   pl.BlockSpec((tm, 2 * V7X_LANES), lambda i: (i, 0))),
        compiler_params=_cparams(("arbitrary",), 40),
        name="modulate_in",
    )(x2, ctx2, mods, wdt, dt_bias, a_log)


def _mm_nt_kernel(a_ref, wt_ref, o_ref, wbf_ref):
    @pl.when(pl.program_id(1) == 0)
    def _():
        wbf_ref[...] = wt_ref[...].astype(BF16)

    o_ref[...] = lax.dot_general(a_ref[...], wbf_ref[...], (((1,), (1,)), ((), ())),
                                 preferred_element_type=F32)


def _matmul_nt(a, wt3, n_out, tm, tn, name):
    m, k = a.shape
    return pl.pallas_call(
        _mm_nt_kernel,
        out_shape=jax.ShapeDtypeStruct((m, n_out), F32),
        grid=(n_out // tn, m // tm),
        in_specs=[pl.BlockSpec((tm, k), lambda j, i: (i, 0)),
                  pl.BlockSpec((None, tn, k), lambda j, i: (0, j, 0))],
        out_specs=pl.BlockSpec((tm, tn), lambda j, i: (i, j)),
        scratch_shapes=[pltpu.VMEM((tn, k), BF16)],
        compiler_params=_cparams(("arbitrary", "arbitrary"), 56),
        name=name,
    )(a, wt3)


def _mm2_kernel(a0_ref, a1_ref, w_ref, o_ref, wbf_ref):
    @pl.when(pl.program_id(1) == 0)
    def _():
        wbf_ref[...] = w_ref[...].astype(BF16)

    k0 = a0_ref.shape[1]
    acc = jnp.dot(a0_ref[...], wbf_ref[0:k0, :], preferred_element_type=F32)
    o_ref[...] = acc + jnp.dot(a1_ref[...], wbf_ref[k0:, :], preferred_element_type=F32)


def _matmul_concat(a0, a1, w, tm, tn, name):
    m, k0 = a0.shape
    k1 = a1.shape[1]
    n_out = w.shape[1]
    return pl.pallas_call(
        _mm2_kernel,
        out_shape=jax.ShapeDtypeStruct((m, n_out), F32),
        grid=(n_out // tn, m // tm),
        in_specs=[pl.BlockSpec((tm, k0), lambda j, i: (i, 0)),
                  pl.BlockSpec((tm, k1), lambda j, i: (i, 0)),
                  pl.BlockSpec((k0 + k1, tn), lambda j, i: (0, j))],
        out_specs=pl.BlockSpec((tm, tn), lambda j, i: (i, j)),
        scratch_shapes=[pltpu.VMEM((k0 + k1, tn), BF16)],
        compiler_params=_cparams(("arbitrary", "arbitrary"), 56),
        name=name,
    )(a0, a1, w)


def _conv_kernel(main_ref, prev_ref, next_ref, w_ref, b_ref, o_ref, ext_ref):
    i = pl.program_id(0)
    rb = main_ref.shape[0]
    halo = V7X_SUBLANES
    pad = (SSM_CONV - 1) // 2
    seq_start = jnp.logical_or(i == 0, i == 1)
    seq_end = jnp.logical_or(i == 0, i == pl.num_programs(0) - 1)
    ext_ref[0:halo, :] = jnp.where(seq_start, 0.0, prev_ref[...])
    ext_ref[halo:halo + rb, :] = main_ref[...]
    ext_ref[halo + rb:2 * halo + rb, :] = jnp.where(seq_end, 0.0, next_ref[...])
    acc = jnp.broadcast_to(b_ref[...], o_ref.shape)
    for k in range(SSM_CONV):
        lo = halo - pad + k
        acc = acc + w_ref[k:k + 1, :] * ext_ref[lo:lo + rb, :]
    o_ref[...] = acc * jax.nn.sigmoid(acc)


def _conv_silu(proj, conv_w8, conv_b, rb, tc):
    rows = proj.shape[0]
    n_row_tiles = rows // rb
    halo_per_tile = rb // V7X_SUBLANES
    n_halo_blocks = rows // V7X_SUBLANES
    col0 = COL_XBC // tc
    return pl.pallas_call(
        _conv_kernel,
        out_shape=jax.ShapeDtypeStruct((rows, SSM_XBC), F32),
        grid=(n_row_tiles, SSM_XBC // tc),
        in_specs=[pl.BlockSpec((rb, tc), lambda i, j: (i, col0 + j)),
                  pl.BlockSpec((V7X_SUBLANES, tc),
                               lambda i, j: (jnp.maximum(i * halo_per_tile - 1, 0), col0 + j)),
                  pl.BlockSpec((V7X_SUBLANES, tc),
                               lambda i, j: (jnp.minimum((i + 1) * halo_per_tile, n_halo_blocks - 1), col0 + j)),
                  pl.BlockSpec((V7X_SUBLANES, tc), lambda i, j: (0, j)),
                  pl.BlockSpec((1, tc), lambda i, j: (0, j))],
        out_specs=pl.BlockSpec((rb, tc), lambda i, j: (i, j)),
        scratch_shapes=[pltpu.VMEM((rb + 2 * V7X_SUBLANES, tc), F32)],
        compiler_params=_cparams(("arbitrary", "arbitrary"), 32),
        name="conv_silu",
    )(proj, proj, proj, conv_w8, conv_b)


def _gmlp_kernel(uv_ref, gain_ref, bias_ref, ws_ref, bst_ref, o_ref):
    g = jax.nn.gelu(uv_ref[...])
    u = g[:, 0:GM_WIDTH]
    v = _ln_rows(g[:, GM_WIDTH:2 * GM_WIDTH]) * gain_ref[...] + bias_ref[...]
    vb = v.astype(BF16)
    for h in range(GM_HEADS):
        cols = slice(h * GM_HEAD_DIM, (h + 1) * GM_HEAD_DIM)
        mixed = jnp.dot(ws_ref[h], vb[:, cols], preferred_element_type=F32) + bst_ref[:, h:h + 1]
        o_ref[:, cols] = (u[:, cols] * mixed).astype(BF16)


def _chunk_mlp(proj, lat_row0, n_chunks, v_gain, v_bias, ws_bf, bs_t):
    blk0 = lat_row0 // CHUNK
    return pl.pallas_call(
        _gmlp_kernel,
        out_shape=jax.ShapeDtypeStruct((n_chunks * CHUNK, GM_WIDTH), BF16),
        grid=(n_chunks,),
        in_specs=[pl.BlockSpec((CHUNK, 2 * GM_WIDTH), lambda c: (blk0 + c, 0)),
                  pl.BlockSpec((1, GM_WIDTH), lambda c: (0, 0)),
                  pl.BlockSpec((1, GM_WIDTH), lambda c: (0, 0)),
                  pl.BlockSpec((GM_HEADS, CHUNK, CHUNK), lambda c: (0, 0, 0)),
                  pl.BlockSpec((CHUNK, GM_HEADS), lambda c: (0, 0))],
        out_specs=pl.BlockSpec((CHUNK, GM_WIDTH), lambda c: (c, 0)),
        compiler_params=_cparams(("arbitrary",), 32),
        name="chunk_mlp",
    )(proj, v_gain, v_bias, ws_bf, bs_t)


def _ssd_chunk(direction, xs_ref, b_ref, c_ref, dtda_ref, mcum_ref, expand_ref, st_ref, emit_group):
    d = direction
    lane0 = d * SSM_HEADS
    dt = dtda_ref[:, 0:V7X_LANES]
    d_a = dtda_ref[:, V7X_LANES:2 * V7X_LANES]
    cs = jnp.dot(mcum_ref[...], d_a, preferred_element_type=F32, precision=lax.Precision.HIGHEST)
    cs_t = cs.T
    last = SSD_CHUNK - 1 if d == 0 else 0
    cs_end = cs[last:last + 1, :]
    dt_decay = dt * jnp.exp(cs_end - cs)
    exp_cs = jnp.exp(cs)
    chunk_decay = jnp.broadcast_to(jnp.exp(cs_end), (V7X_SUBLANES, V7X_LANES))
    q = jnp.concatenate([dt, dt_decay, chunk_decay], axis=0)
    q_hi = q.astype(BF16)
    q_lo = (q - q_hi.astype(F32)).astype(BF16)

    row_i = lax.broadcasted_iota(I32, (SSD_CHUNK, SSD_CHUNK), 0)
    col_i = lax.broadcasted_iota(I32, (SSD_CHUNK, SSD_CHUNK), 1)
    causal = (row_i >= col_i) if d == 0 else (row_i <= col_i)
    first_head = col_i < SSM_HEAD_DIM

    for g in range(SSM_GROUPS):
        gcols = slice(g * GROUP_WIDTH, (g + 1) * GROUP_WIDTH)
        ncols = slice(g * SSM_STATE, (g + 1) * SSM_STATE)
        c32 = c_ref[:, ncols]
        b_bf = b_ref[:, ncols].astype(BF16)
        c_bf = c32.astype(BF16)
        cb = lax.dot_general(c_bf, b_bf, (((1,), (1,)), ((), ())), preferred_element_type=F32)
        e_g = expand_ref[:, gcols]
        q_e = (jnp.dot(q_hi, e_g, preferred_element_type=F32)
               + jnp.dot(q_lo, e_g, preferred_element_type=F32))
        xs_g = xs_ref[:, gcols]
        x_dt = (xs_g * q_e[0:SSD_CHUNK]).astype(BF16)
        x_dec = (xs_g * q_e[SSD_CHUNK:2 * SSD_CHUNK]).astype(BF16)
        h_old = st_ref[g]
        h_bf = h_old.astype(BF16)
        y_pairs = []
        for pr in range(HEADS_PER_GROUP // 2):
            pcols = slice(pr * V7X_LANES, (pr + 1) * V7X_LANES)
            xp = x_dt[:, pcols]
            hp = h_bf[:, pcols]
            zero = jnp.zeros_like(xp)
            rhs = jnp.concatenate([jnp.where(first_head, xp, zero), jnp.where(first_head, hp, zero),
                                   jnp.where(first_head, zero, xp), jnp.where(first_head, zero, hp)], axis=0)
            parts = []
            for k in range(2):
                lane = lane0 + g * HEADS_PER_GROUP + 2 * pr + k
                seg = jnp.exp(jnp.where(causal, cs[:, lane:lane + 1] - cs_t[lane:lane + 1, :], -jnp.inf))
                parts.append((cb * seg).astype(BF16))
                parts.append((c32 * exp_cs[:, lane:lane + 1]).astype(BF16))
            lhs = jnp.concatenate(parts, axis=1)
            y_pairs.append(jnp.dot(lhs, rhs, preferred_element_type=F32))
        emit_group(g, xs_g, jnp.concatenate(y_pairs, axis=1))
        s_new = lax.dot_general(b_bf, x_dec, (((0,), (0,)), ((), ())), preferred_element_type=F32)
        st_ref[g] = h_old * q_e[2 * SSD_CHUNK:2 * SSD_CHUNK + 1] + s_new


def _ssd_fwd_kernel(xs_ref, b_ref, c_ref, dtda_ref, mcum_ref, expand_ref, y_ref, st_ref):
    @pl.when(pl.program_id(0) == 0)
    def _():
        st_ref[...] = jnp.zeros_like(st_ref)

    def emit(g, xs_g, y_g):
        y_ref[:, g * GROUP_WIDTH:(g + 1) * GROUP_WIDTH] = y_g

    _ssd_chunk(0, xs_ref, b_ref, c_ref, dtda_ref, mcum_ref, expand_ref, st_ref, emit)


def _ssd_bwd_kernel(xs_ref, b_ref, c_ref, dtda_ref, mcum_ref, expand_ref, yf_ref, z_ref,
                    dskip_ref, ngain_ref, o_ref, st_ref):
    @pl.when(pl.program_id(0) == 0)
    def _():
        st_ref[...] = jnp.zeros_like(st_ref)

    def emit(g, xs_g, y_g):
        gcols = slice(g * GROUP_WIDTH, (g + 1) * GROUP_WIDTH)
        y = y_g + yf_ref[:, gcols] + dskip_ref[:, gcols] * xs_g
        gated = y * jax.nn.silu(z_ref[:, gcols])
        ms = jnp.mean(gated * gated, axis=-1, keepdims=True)
        o_ref[:, gcols] = (gated * lax.rsqrt(ms + RMS_EPS) * ngain_ref[:, gcols]).astype(BF16)

    _ssd_chunk(1, xs_ref, b_ref, c_ref, dtda_ref, mcum_ref, expand_ref, st_ref, emit)


def _ssd_specs(chunk_of_step):
    xs_blk = SSM_WIDTH
    n_blk = SSM_GROUPS * SSM_STATE
    return [pl.BlockSpec((SSD_CHUNK, xs_blk), lambda i: (chunk_of_step(i), 0)),
            pl.BlockSpec((SSD_CHUNK, n_blk), lambda i: (chunk_of_step(i), SSM_WIDTH // n_blk)),
            pl.BlockSpec((SSD_CHUNK, n_blk), lambda i: (chunk_of_step(i), SSM_WIDTH // n_blk + 1)),
            pl.BlockSpec((SSD_CHUNK, 2 * V7X_LANES), lambda i: (chunk_of_step(i), 0)),
            pl.BlockSpec((SSD_CHUNK, SSD_CHUNK), lambda i: (0, 0)),
            pl.BlockSpec((V7X_LANES, SSM_WIDTH), lambda i: (0, 0))]


def _ssd_forward(conv, dtda, n_ctx_chunks, n_lat_chunks):
    n_steps = n_ctx_chunks + n_lat_chunks
    tri = (jnp.arange(SSD_CHUNK)[:, None] >= jnp.arange(SSD_CHUNK)[None, :]).astype(F32)
    expand = _head_expand(0)
    return pl.pallas_call(
        _ssd_fwd_kernel,
        out_shape=jax.ShapeDtypeStruct((n_lat_chunks * SSD_CHUNK, SSM_WIDTH), F32),
        grid=(n_steps,),
        in_specs=_ssd_specs(lambda i: i),
        out_specs=pl.BlockSpec((SSD_CHUNK, SSM_WIDTH), lambda i: (jnp.maximum(i - n_ctx_chunks, 0), 0)),
        scratch_shapes=[pltpu.VMEM((SSM_GROUPS, SSM_STATE, GROUP_WIDTH), F32)],
        compiler_params=_cparams(("arbitrary",), 40),
        name="ssd_forward",
    )(conv, conv, conv, dtda, tri, expand)


def _ssd_backward(conv, dtda, y_fwd, proj, d_skip, norm_gain, n_ctx_chunks, n_lat_chunks):
    n_steps = n_ctx_chunks + n_lat_chunks
    tri = (jnp.arange(SSD_CHUNK)[:, None] <= jnp.arange(SSD_CHUNK)[None, :]).astype(F32)
    expand = _head_expand(1)

    def chunk_of_step(i):
        return jnp.where(i < n_ctx_chunks, n_ctx_chunks - 1 - i, n_steps - 1 + n_ctx_chunks - i)

    def lat_of_step(i):
        return jnp.where(i < n_ctx_chunks, n_lat_chunks - 1, n_steps - 1 - i)

    return pl.pallas_call(
        _ssd_bwd_kernel,
        out_shape=jax.ShapeDtypeStruct((n_lat_chunks * SSD_CHUNK, SSM_WIDTH), BF16),
        grid=(n_steps,),
        in_specs=_ssd_specs(chunk_of_step) + [
            pl.BlockSpec((SSD_CHUNK, SSM_WIDTH), lambda i: (lat_of_step(i), 0)),
            pl.BlockSpec((SSD_CHUNK, SSM_WIDTH), lambda i: (chunk_of_step(i), COL_Z // SSM_WIDTH)),
            pl.BlockSpec((1, SSM_WIDTH), lambda i: (0, 0)),
            pl.BlockSpec((1, SSM_WIDTH), lambda i: (0, 0))],
        out_specs=pl.BlockSpec((SSD_CHUNK, SSM_WIDTH), lambda i: (lat_of_step(i), 0)),
        scratch_shapes=[pltpu.VMEM((SSM_GROUPS, SSM_STATE, GROUP_WIDTH), F32)],
        compiler_params=_cparams(("arbitrary",), 40),
        name="ssd_backward",
    )(conv, conv, conv, dtda, tri, expand, y_fwd, proj, d_skip, norm_gain)


def _head_expand(direction):
    lane = jnp.arange(V7X_LANES)[:, None]
    chan = jnp.arange(SSM_WIDTH)[None, :]
    return (lane == direction * SSM_HEADS + chan // SSM_HEAD_DIM).astype(BF16)


def _postmix_kernel(x_ref, mix_ref, p_ref, lng_ref, lnb_ref, wrt_ref, hfp_ref, acc_ref, aff_ref, hf_scr):
    gate_m = p_ref[0, 2:3, :]
    shift_f = p_ref[0, 3:4, :]
    scale1_f = 1.0 + p_ref[0, 4:5, :]
    tm = x_ref.shape[0]
    half = D_MODEL // 2

    def body(r, carry):
        sl = pl.ds(pl.multiple_of(r * ROW_STRIP, ROW_STRIP), ROW_STRIP)
        xn = _ln_rows(ALPHA * x_ref[sl, :] + gate_m * mix_ref[sl, :]) * lng_ref[...] + lnb_ref[...]
        acc_ref[sl, :] = ALPHA * xn
        hf = (_ln_rows(xn) * scale1_f + shift_f).astype(BF16)
        hf_scr[sl, :] = hf
        lo = lax.bitcast_convert_type(hf[:, 0:half].astype(F32), U32)
        hi = lax.bitcast_convert_type(hf[:, half:].astype(F32), U32)
        hfp_ref[sl, :] = (lo >> 16) | (hi & jnp.uint32(0xFFFF0000))
        return carry

    lax.fori_loop(0, tm // ROW_STRIP, body, 0)
    logits = lax.dot_general(wrt_ref[...], hf_scr[...], (((1,), (1,)), ((), ())),
                             preferred_element_type=F32)
    mx = jnp.max(logits, axis=0, keepdims=True)
    ex = jnp.exp(logits - mx)
    aff_ref[...] = ex / jnp.sum(ex, axis=0, keepdims=True)


def _post_mix(x2, mix, mods, ln_gain, ln_bias, wr_t, tm):
    l, d = x2.shape
    return pl.pallas_call(
        _postmix_kernel,
        out_shape=(jax.ShapeDtypeStruct((l, d // 2), U32),
                   jax.ShapeDtypeStruct((l, d), F32),
                   jax.ShapeDtypeStruct((N_EXPERTS, l), F32)),
        grid=(l // tm,),
        in_specs=[pl.BlockSpec((tm, d), lambda i: (i, 0)),
                  pl.BlockSpec((tm, d), lambda i: (i, 0)),
                  pl.BlockSpec((1, V7X_SUBLANES, d), lambda i: (0, 0, 0)),
                  pl.BlockSpec((1, d), lambda i: (0, 0)),
                  pl.BlockSpec((1, d), lambda i: (0, 0)),
                  pl.BlockSpec((N_EXPERTS, d), lambda i: (0, 0))],
        out_specs=(pl.BlockSpec((tm, d // 2), lambda i: (i, 0)),
                   pl.BlockSpec((tm, d), lambda i: (i, 0)),
                   pl.BlockSpec((N_EXPERTS, tm), lambda i: (0, i))),
        scratch_shapes=[pltpu.VMEM((tm, d), BF16)],
        compiler_params=_cparams(("arbitrary",), 48),
        name="post_mix",
    )(x2, mix, mods, ln_gain, ln_bias, wr_t)


def _topk_kernel(aff_ref, triu_ref, slow_ref, idx_ref, gate_ref, cum_scr, bend_v, bend_s, sem, *, cap):
    n_e, n_blk, n_lane = aff_ref.shape
    aff = aff_ref[...]
    bits = lax.bitcast_convert_type(aff, I32)

    def count(mask):
        c = jnp.sum(jnp.where(mask, 1.0, 0.0), axis=1, keepdims=True)
        return jnp.sum(c, axis=2, keepdims=True)

    def search(k, thr):
        cand = thr | jnp.left_shift(jnp.int32(1), 30 - k)
        return jnp.where(count(bits >= cand) >= cap, cand, thr)

    thr = lax.fori_loop(0, 31, search, jnp.zeros((n_e, 1, 1), I32))
    above = bits > thr
    equal = bits == thr
    need = cap - count(above)

    def prefix(mask):
        m = jnp.where(mask, 1.0, 0.0).astype(BF16)
        within = jnp.dot(m.reshape(n_e * n_blk, n_lane), triu_ref[...],
                         preferred_element_type=F32).reshape(n_e, n_blk, n_lane)
        tot = jnp.broadcast_to(within[:, :, n_lane - 1:n_lane], (n_e, n_blk, n_lane)).astype(BF16)
        offs = [jnp.dot(slow_ref[...], tot[e], preferred_element_type=F32) for e in range(n_e)]
        return within + jnp.stack(offs, axis=0)

    cum_above = prefix(above)
    cum_equal = prefix(equal)
    chosen = jnp.logical_or(above, jnp.logical_and(equal, cum_equal <= need))
    cum = cum_above + jnp.minimum(cum_equal, need)
    cum_scr[...] = jnp.where(chosen, cum, -1.0)

    last_lane = jnp.where(lax.broadcasted_iota(I32, (V7X_SUBLANES, n_lane), 1) == n_lane - 1, 1.0, 0.0)
    bend_v[...] = jnp.zeros_like(bend_v)
    for e in range(n_e):
        ends = lax.dot_general(last_lane, cum[e], (((1,), (1,)), ((), ())),
                               preferred_element_type=F32, precision=lax.Precision.HIGHEST)
        bend_v[e:e + 1, 0:n_blk] = ends[0:1, :].astype(I32)
    to_smem = pltpu.make_async_copy(bend_v, bend_s, sem)
    to_smem.start()
    to_smem.wait()

    jt = SLOT_TILE
    lane_f = lax.broadcasted_iota(I32, (jt, n_lane), 1).astype(F32)

    def per_expert(e, carry):
        def per_tile(t, b_start):
            first = t * jt
            b_lo = lax.while_loop(lambda b: jnp.logical_and(b < n_blk - 1, bend_s[e, b] <= first),
                                  lambda b: b + 1, b_start)
            b_hi = lax.while_loop(lambda b: jnp.logical_and(b < n_blk - 1, bend_s[e, b] < first + jt),
                                  lambda b: b + 1, b_lo)
            slot = (lax.broadcasted_iota(I32, (jt, n_lane), 0) + (first + 1)).astype(F32)

            def visit(b, accs):
                hit = cum_scr[e, pl.ds(b, 1), :] == slot
                return (accs[0] + jnp.where(hit, (b + 1).astype(F32), 0.0),
                        accs[1] + jnp.where(hit, aff_ref[e, pl.ds(b, 1), :], 0.0))

            zeros = jnp.zeros((jt, n_lane), F32)
            blk_acc, gate_acc = lax.fori_loop(b_lo, b_hi + 1, visit, (zeros, zeros))
            tok = jnp.where(blk_acc > 0.0, (blk_acc - 1.0) * n_lane + lane_f, 0.0)
            tok = jnp.sum(tok, axis=1, keepdims=True)
            gate = jnp.sum(gate_acc, axis=1, keepdims=True)
            rows = pl.ds(pl.multiple_of(t * jt, jt), jt)
            idx_ref[e, rows, :] = jnp.broadcast_to(tok, (jt, n_lane)).astype(I32)
            gate_ref[e, rows, :] = jnp.broadcast_to(gate, (jt, n_lane))
            return b_lo
        lax.fori_loop(0, cap // jt, per_tile, jnp.int32(0))
        return carry

    lax.fori_loop(0, n_e, per_expert, 0)


def _expert_choice_topk(aff_t, cap):
    n_e, l = aff_t.shape
    n_blk = l // V7X_LANES
    aff3 = aff_t.reshape(n_e, n_blk, V7X_LANES)
    triu = (jnp.arange(V7X_LANES)[:, None] <= jnp.arange(V7X_LANES)[None, :]).astype(BF16)
    slow = (jnp.arange(n_blk)[:, None] > jnp.arange(n_blk)[None, :]).astype(BF16)
    return pl.pallas_call(
        functools.partial(_topk_kernel, cap=cap),
        out_shape=(jax.ShapeDtypeStruct((n_e, cap, V7X_LANES), I32),
                   jax.ShapeDtypeStruct((n_e, cap, V7X_LANES), F32)),
        grid=(1,),
        in_specs=[pl.BlockSpec((n_e, n_blk, V7X_LANES), lambda i: (0, 0, 0)),
                  pl.BlockSpec((V7X_LANES, V7X_LANES), lambda i: (0, 0)),
                  pl.BlockSpec((n_blk, n_blk), lambda i: (0, 0))],
        out_specs=(pl.BlockSpec((n_e, cap, V7X_LANES), lambda i: (0, 0, 0)),
                   pl.BlockSpec((n_e, cap, V7X_LANES), lambda i: (0, 0, 0))),
        scratch_shapes=[pltpu.VMEM((n_e, n_blk, V7X_LANES), F32),
                        pltpu.VMEM((n_e, V7X_LANES), I32), pltpu.SMEM((n_e, V7X_LANES), I32),
                        pltpu.SemaphoreType.DMA(())],
        compiler_params=_cparams(("arbitrary",), 48),
        name="expert_choice_topk",
    )(aff3, triu, slow)


def _row_copy(src, dst, src_row, dst_row, sem):
    return pltpu.make_async_copy(src.at[pl.ds(src_row, 1), :], dst.at[pl.ds(dst_row, 1), :], sem)


def _wait_rows_in(hbm, buf, sem):
    pltpu.make_async_copy(hbm.at[pl.ds(0, buf.shape[0]), :], buf, sem).wait()


def _wait_rows_out(buf, hbm, sem):
    pltpu.make_async_copy(buf, hbm.at[pl.ds(0, buf.shape[0]), :], sem).wait()


def _ffn_kernel(idx_ref, hfp, wg_ref, wu_ref, wd_ref, gate_ref, gf_ref, o_ref,
                xe_scr, h_scr, a_scr, u_scr, stage, sems, *, n_ff_tiles, n_experts, n_steps, cap):
    e = pl.program_id(0)
    s = pl.program_id(1)
    rows = cap // n_steps
    half = D_MODEL // 2
    tf = wg_ref.shape[2]

    def issue(expert, chunk, slot):
        base = expert * cap + chunk * rows
        for j in range(rows):
            _row_copy(hfp, stage.at[slot], idx_ref[base + j], j, sems.at[slot]).start()

    def land(buf, chunk, slot):
        _wait_rows_in(hfp, stage.at[slot], sems.at[slot])
        w = stage[slot]
        first = chunk * rows
        r = pl.ds(first if isinstance(first, int) else pl.multiple_of(first, rows), rows)
        xe_scr[buf, r, 0:half] = lax.bitcast_convert_type(w << 16, F32).astype(BF16)
        xe_scr[buf, r, half:] = lax.bitcast_convert_type(w & jnp.uint32(0xFFFF0000), F32).astype(BF16)

    @pl.when(jnp.logical_and(e == 0, s == 0))
    def _():
        issue(0, 0, 0)
        for k in range(n_steps):
            if k + 1 < n_steps:
                issue(0, k + 1, (k + 1) % 2)
            land(0, k, k % 2)

    @pl.when(jnp.logical_and(e > 0, s == 0))
    def _():
        land(e % 2, n_steps - 1, (n_steps - 1) % 2)

    @pl.when(jnp.logical_and(e + 1 < n_experts, s > 0))
    def _():
        land((e + 1) % 2, s - 1, (s - 1) % 2)

    @pl.when(e + 1 < n_experts)
    def _():
        issue(e + 1, s, s % 2)

    tk = wg_ref.shape[1]
    k_part = s % K_SPLIT

    @pl.when(s < n_ff_tiles * K_SPLIT)
    def _():
        xe = xe_scr[e % 2, :, pl.ds(pl.multiple_of(k_part * tk, tk), tk)]
        a = jnp.dot(xe, wg_ref[0].astype(BF16), preferred_element_type=F32)
        u = jnp.dot(xe, wu_ref[0].astype(BF16), preferred_element_type=F32)

        @pl.when(k_part == 0)
        def _():
            a_scr[...] = a
            u_scr[...] = u

        @pl.when(k_part == K_SPLIT - 1)
        def _():
            a_all = a_scr[...] + a
            u_all = u_scr[...] + u
            col = pl.multiple_of((s // K_SPLIT) * tf, tf)
            h_scr[:, pl.ds(col, tf)] = (jax.nn.silu(a_all) * u_all).astype(BF16)

    @pl.when(s >= n_ff_tiles * K_SPLIT)
    def _():
        y = jnp.dot(h_scr[...], wd_ref[0].astype(BF16), preferred_element_type=F32)
        o_ref[0] = (y * gate_ref[0, :, 0:1] * gf_ref[...]).astype(BF16)


def _expert_ffn(idx_flat, hf_packed, w_gate, w_up, w_down, gates, gate_f, cap, tf, tn):
    n_e, d, ff = w_gate.shape
    n_ff = ff // tf
    n_out = d // tn
    n_a = n_ff * K_SPLIT
    n_steps = n_a + n_out
    rows = cap // n_steps
    col_a = lambda e, s, idx: (e, jnp.minimum(s, n_a - 1) % K_SPLIT, jnp.minimum(s, n_a - 1) // K_SPLIT)
    col_b = lambda e, s, idx: (e, 0, jnp.maximum(s - n_a, 0))
    return pl.pallas_call(
        functools.partial(_ffn_kernel, n_ff_tiles=n_ff, n_experts=n_e, n_steps=n_steps, cap=cap),
        out_shape=jax.ShapeDtypeStruct((n_e, cap, d), BF16),
        grid_spec=pltpu.PrefetchScalarGridSpec(
            num_scalar_prefetch=1, grid=(n_e, n_steps),
            in_specs=[pl.BlockSpec(memory_space=pl.ANY),
                      pl.BlockSpec((1, d // K_SPLIT, tf), col_a),
                      pl.BlockSpec((1, d // K_SPLIT, tf), col_a),
                      pl.BlockSpec((1, ff, tn), col_b),
                      pl.BlockSpec((1, cap, V7X_LANES), lambda e, s, idx: (e, 0, 0), pipeline_mode=pl.Buffered(1)),
                      pl.BlockSpec((1, tn), lambda e, s, idx: (0, jnp.maximum(s - n_a, 0)))],
            out_specs=pl.BlockSpec((1, cap, tn), col_b),
            scratch_shapes=[pltpu.VMEM((2, cap, d), BF16), pltpu.VMEM((cap, ff), BF16),
                            pltpu.VMEM((cap, tf), F32), pltpu.VMEM((cap, tf), F32),
                            pltpu.VMEM((2, rows, d // 2), U32), pltpu.SemaphoreType.DMA((2,))]),
        compiler_params=_cparams(("arbitrary", "arbitrary"), 60),
        name="expert_ffn",
    )(idx_flat, hf_packed, w_gate, w_up, w_down, gates, gate_f)


def _combine_kernel(idx_ref, ye_ref, acc_in, acc_out, buf, gsem, ssem, *, cap, n_tiles):
    e = pl.program_id(0)
    m = pl.program_id(1)
    n_buf, n_grp, grp, _ = buf.shape
    tm = n_grp * grp

    def for_rows(tile, fn):
        def body(g, carry):
            first = e * cap + tile * tm + g * grp
            for u in range(grp):
                fn(g, u, idx_ref[first + u])
            return carry
        lax.fori_loop(0, n_grp, body, 0)

    def fetch(tile, b):
        for_rows(tile, lambda g, u, tok: _row_copy(acc_in, buf.at[b, g], tok, u, gsem.at[b]).start())

    def put(tile, b):
        for_rows(tile, lambda g, u, tok: _row_copy(buf.at[b, g], acc_out, u, tok, ssem.at[b]).start())

    def fetch_wait(b):
        for g in range(n_grp):
            _wait_rows_in(acc_in, buf.at[b, g], gsem.at[b])

    def put_wait(b):
        for g in range(n_grp):
            _wait_rows_out(buf.at[b, g], acc_out, ssem.at[b])

    for t in range(n_tiles):
        @pl.when(m == t)
        def _(t=t):
            b = t % n_buf
            if t == 0:
                fetch(0, 0)
                if n_tiles > 1:
                    fetch(1, 1 % n_buf)
            elif t + 1 < n_tiles:
                nb = (t + 1) % n_buf
                if t + 1 >= n_buf:
                    put_wait(nb)
                fetch(t + 1, nb)
            fetch_wait(b)
            for g in range(n_grp):
                buf[b, g] = buf[b, g] + ye_ref[0, g * grp:(g + 1) * grp, :].astype(F32)
            put(t, b)
            if t == n_tiles - 1:
                for u in range(max(0, n_tiles - n_buf), n_tiles):
                    put_wait(u % n_buf)


def _combine(idx_flat, ye, acc, tm, n_buf):
    n_e, cap, d = ye.shape
    n_tiles = cap // tm
    return pl.pallas_call(
        functools.partial(_combine_kernel, cap=cap, n_tiles=n_tiles),
        out_shape=jax.ShapeDtypeStruct(acc.shape, acc.dtype),
        grid_spec=pltpu.PrefetchScalarGridSpec(
            num_scalar_prefetch=1, grid=(n_e, n_tiles),
            in_specs=[pl.BlockSpec((1, tm, d), lambda e, m, idx: (e, m, 0)),
                      pl.BlockSpec(memory_space=pl.ANY)],
            out_specs=pl.BlockSpec(memory_space=pl.ANY),
            scratch_shapes=[pltpu.VMEM((n_buf, tm // ROW_STRIP, ROW_STRIP, d), F32),
                            pltpu.SemaphoreType.DMA((n_buf,)), pltpu.SemaphoreType.DMA((n_buf,))]),
        input_output_aliases={2: 0},
        compiler_params=_cparams(("arbitrary", "arbitrary"), 32),
        name="combine",
    )(idx_flat, ye, acc)


def _final_ln_kernel(a_ref, g_ref, b_ref, o_ref):
    def body(r, carry):
        sl = pl.ds(pl.multiple_of(r * ROW_STRIP, ROW_STRIP), ROW_STRIP)
        o_ref[sl, :] = _ln_rows(a_ref[sl, :]) * g_ref[...] + b_ref[...]
        return carry
    lax.fori_loop(0, a_ref.shape[0] // ROW_STRIP, body, 0)


def _final_ln(acc, gain, bias, tm):
    l, d = acc.shape
    return pl.pallas_call(
        _final_ln_kernel,
        out_shape=jax.ShapeDtypeStruct((l, d), F32),
        grid=(l // tm,),
        in_specs=[pl.BlockSpec((tm, d), lambda i: (i, 0)),
                  pl.BlockSpec((1, d), lambda i: (0, 0)),
                  pl.BlockSpec((1, d), lambda i: (0, 0))],
        out_specs=pl.BlockSpec((tm, d), lambda i: (i, 0)),
        compiler_params=_cparams(("arbitrary",), 32),
        name="final_ln",
    )(acc, gain, bias)


def _pad_rows(a, rows):
    return jnp.pad(a, ((0, rows - a.shape[0]), (0, 0)))


def _pad_lanes(a, lanes):
    return jnp.pad(a, ((0, 0), (0, lanes - a.shape[1])))


def kernel(x, c, ctx, c_ctx, w_ada, b_ada, w_in, gm_v_gain, gm_v_bias, gm_w_s, gm_b_s, ssm_conv_w, ssm_conv_b, ssm_dt_bias, ssm_a_log, ssm_d, ssm_norm_gain, w_out, ln_mix_gain, ln_mix_bias, w_router, w_gate, w_up, w_down, ln_ffn_gain, ln_ffn_bias):
    b, l, d = x.shape
    lc = ctx.shape[1]
    assert b == 1 and d == D_MODEL and w_ada.shape[0] == DEPTH
    x2 = x[0]
    ctx2 = ctx[0]
    layer = 0

    cond = _pad_rows(jnp.concatenate([c, c_ctx[None, :]], axis=0), V7X_SUBLANES)
    ada = _ada_params(cond, w_ada[layer], b_ada[layer][None, :])
    mods = jnp.pad(ada[:2].reshape(2, 6, d), ((0, 0), (0, V7X_SUBLANES - 6), (0, 0)))

    w_in_t = jnp.swapaxes(w_in, 1, 2)
    wdt = _pad_rows(w_in_t[layer, COL_DT:, :], V7X_LANES).astype(BF16)
    dt_bias = _pad_lanes(ssm_dt_bias[layer].reshape(1, 2 * SSM_HEADS), V7X_LANES)
    a_log = _pad_lanes(ssm_a_log[layer].reshape(1, 2 * SSM_HEADS), V7X_LANES)
    h, dtda = _modulate_in(x2, ctx2, mods, wdt, dt_bias, a_log)
    proj = _matmul_nt(h, w_in_t, COL_DT, tm=1408, tn=512, name="in_proj")

    conv_w8 = _pad_rows(ssm_conv_w[layer], V7X_SUBLANES)
    conv = _conv_silu(proj, conv_w8, ssm_conv_b[layer][None, :], rb=lc, tc=1024)

    n_lat_chunks = l // CHUNK
    n_ctx_chunks = lc // SSD_CHUNK
    gm = _chunk_mlp(proj, lc, n_lat_chunks, gm_v_gain[layer][None, :], gm_v_bias[layer][None, :],
                    gm_w_s[layer].astype(BF16), gm_b_s[layer].T)

    y_fwd = _ssd_forward(conv, dtda, n_ctx_chunks, n_lat_chunks)
    d_skip = jnp.repeat(ssm_d[layer], SSM_HEAD_DIM)[None, :]
    ss = _ssd_backward(conv, dtda, y_fwd, proj, d_skip, ssm_norm_gain[layer][None, :],
                       n_ctx_chunks, n_lat_chunks)

    mix = _matmul_concat(gm, ss, w_out[layer], tm=1024, tn=512, name="out_proj")

    hf_packed, acc, aff_t = _post_mix(x2, mix, mods, ln_mix_gain[layer][None, :], ln_mix_bias[layer][None, :],
                                      w_router[layer].T.astype(BF16), tm=256)

    cap = EC_FACTOR * l // N_EXPERTS
    idx3, gates = _expert_choice_topk(aff_t, cap)
    idx_flat = idx3[:, :, 0].reshape(N_EXPERTS * cap)

    gate_f = mods[0, 5:6, :]
    ye = _expert_ffn(idx_flat, hf_packed, w_gate[layer], w_up[layer], w_down[layer], gates, gate_f,
                     cap, tf=512, tn=512)
    acc = _combine(idx_flat, ye, acc, tm=256, n_buf=3)
    out = _final_ln(acc, ln_ffn_gain[layer][None, :], ln_ffn_bias[layer][None, :], tm=256)
    return out[None]
```

```python
import functools

import jax
import jax.numpy as jnp
from jax import lax
from jax.experimental import pallas as pl
from jax.experimental.pallas import tpu as pltpu

F32 = jnp.float32
BF16 = jnp.bfloat16
I32 = jnp.int32
U32 = jnp.uint32

D_MODEL = 4096
GRID_W = 64
CHUNK = 128
GM_WIDTH = 2048
GM_HEADS = 8
GM_HEAD_DIM = GM_WIDTH // GM_HEADS
SSM_WIDTH = 2048
SSM_HEAD_DIM = 64
SSM_HEADS = SSM_WIDTH // SSM_HEAD_DIM
SSM_GROUPS = 8
SSM_STATE = 128
SSM_CONV = 5
SSM_XBC = SSM_WIDTH + 2 * SSM_GROUPS * SSM_STATE
SSD_CHUNK = 128
HEADS_PER_GROUP = SSM_HEADS // SSM_GROUPS
GROUP_WIDTH = SSM_WIDTH // SSM_GROUPS
N_EXPERTS = 16
EXPERT_FF = 2048
EC_FACTOR = 2
DEPTH = 1
ALPHA = (2 * DEPTH) ** 0.25
LN_EPS = 1e-5
RMS_EPS = 1e-5

COL_Z = 2 * GM_WIDTH
COL_XBC = COL_Z + SSM_WIDTH
COL_DT = COL_XBC + SSM_XBC

V7X_LANES = 128
V7X_SUBLANES = 8
V7X_VMEM_BYTES = 64 * 1024 * 1024
MIB = 1024 * 1024

ROW_STRIP = 64
SLOT_TILE = 128
DMA_GROUP = 32


def _cparams(semantics, vmem_mib):
    return pltpu.CompilerParams(dimension_semantics=semantics,
                                vmem_limit_bytes=min(vmem_mib * MIB, V7X_VMEM_BYTES - 2 * MIB))


def _ln_rows(x):
    mu = jnp.mean(x, axis=-1, keepdims=True)
    xc = x - mu
    var = jnp.mean(xc * xc, axis=-1, keepdims=True)
    return xc * lax.rsqrt(var + LN_EPS)


def _ada_kernel(c_ref, w_ref, b_ref, o_ref):
    s = jax.nn.silu(c_ref[...]).astype(BF16)
    o_ref[...] = jnp.dot(s, w_ref[...].astype(BF16), preferred_element_type=F32) + b_ref[...]


def _ada_params(cond, w_ada, b_ada):
    k, n = w_ada.shape
    tn = 512
    return pl.pallas_call(
        _ada_kernel,
        out_shape=jax.ShapeDtypeStruct((V7X_SUBLANES, n), F32),
        grid=(n // tn,),
        in_specs=[pl.BlockSpec((V7X_SUBLANES, k), lambda j: (0, 0)),
                  pl.BlockSpec((k, tn), lambda j: (0, j)),
                  pl.BlockSpec((1, tn), lambda j: (0, j))],
        out_specs=pl.BlockSpec((V7X_SUBLANES, tn), lambda j: (0, j)),
        compiler_params=_cparams(("arbitrary",), 32),
        name="ada_params",
    )(cond, w_ada, b_ada)


def _modin_kernel(x_ref, ctx_ref, p_ref, wdt_ref, bias_ref, alog_ref, h_ref, dtda_ref):
    i = pl.program_id(0)
    shift = p_ref[0, 0:1, :]
    scale1 = 1.0 + p_ref[0, 1:2, :]
    tm = h_ref.shape[0]

    def rows(src_ref):
        def body(r, carry):
            sl = pl.ds(pl.multiple_of(r * ROW_STRIP, ROW_STRIP), ROW_STRIP)
            h_ref[sl, :] = (_ln_rows(src_ref[sl, :]) * scale1 + shift).astype(BF16)
            return carry
        lax.fori_loop(0, tm // ROW_STRIP, body, 0)

    @pl.when(i == 0)
    def _():
        rows(ctx_ref)

    @pl.when(i > 0)
    def _():
        rows(x_ref)

    raw = lax.dot_general(h_ref[...], wdt_ref[...], (((1,), (1,)), ((), ())),
                          preferred_element_type=F32) + bias_ref[...]
    dt = jnp.maximum(raw, 0.0) + jnp.log1p(jnp.exp(-jnp.abs(raw)))
    dtda_ref[:, 0:V7X_LANES] = dt
    dtda_ref[:, V7X_LANES:2 * V7X_LANES] = dt * (-jnp.exp(alog_ref[...]))


def _modulate_in(x2, ctx2, mods, wdt, dt_bias, a_log):
    l, d = x2.shape
    lc = ctx2.shape[0]
    tm = lc
    n_tiles = (l + lc) // tm
    return pl.pallas_call(
        _modin_kernel,
        out_shape=(jax.ShapeDtypeStruct((l + lc, d), BF16),
                   jax.ShapeDtypeStruct((l + lc, 2 * V7X_LANES), F32)),
        grid=(n_tiles,),
        in_specs=[pl.BlockSpec((tm, d), lambda i: (jnp.maximum(i - 1, 0), 0)),
                  pl.BlockSpec((tm, d), lambda i: (0, 0)),
                  pl.BlockSpec((1, V7X_SUBLANES, d), lambda i: (jnp.where(i == 0, 1, 0), 0, 0)),
                  pl.BlockSpec((V7X_LANES, d), lambda i: (0, 0)),
                  pl.BlockSpec((1, V7X_LANES), lambda i: (0, 0)),
                  pl.BlockSpec((1, V7X_LANES), lambda i: (0, 0))],
        out_specs=(pl.BlockSpec((tm, d), lambda i: (i, 0)),
                   pl.BlockSpec((tm, 2 * V7X_LANES), lambda i: (i, 0))),
        compiler_params=_cparams(("arbitrary",), 40),
        name="modulate_in",
    )(x2, ctx2, mods, wdt, dt_bias, a_log)


def _mm_nt_kernel(a_ref, wt_ref, o_ref, wbf_ref):
    @pl.when(pl.program_id(1) == 0)
    def _():
        wbf_ref[...] = wt_ref[...].astype(BF16)

    o_ref[...] = lax.dot_general(a_ref[...], wbf_ref[...], (((1,), (1,)), ((), ())),
                                 preferred_element_type=F32)


def _matmul_nt(a, wt3, n_out, tm, tn, name):
    m, k = a.shape
    return pl.pallas_call(
        _mm_nt_kernel,
        out_shape=jax.ShapeDtypeStruct((m, n_out), F32),
        grid=(n_out // tn, m // tm),
        in_specs=[pl.BlockSpec((tm, k), lambda j, i: (i, 0)),
                  pl.BlockSpec((None, tn, k), lambda j, i: (0, j, 0))],
        out_specs=pl.BlockSpec((tm, tn), lambda j, i: (i, j)),
        scratch_shapes=[pltpu.VMEM((tn, k), BF16)],
        compiler_params=_cparams(("arbitrary", "arbitrary"), 56),
        name=name,
    )(a, wt3)


def _mm2_kernel(a0_ref, a1_ref, w_ref, o_ref, wbf_ref):
    @pl.when(pl.program_id(1) == 0)
    def _():
        wbf_ref[...] = w_ref[...].astype(BF16)

    k0 = a0_ref.shape[1]
    acc = jnp.dot(a0_ref[...], wbf_ref[0:k0, :], preferred_element_type=F32)
    o_ref[...] = acc + jnp.dot(a1_ref[...], wbf_ref[k0:, :], preferred_element_type=F32)


def _matmul_concat(a0, a1, w, tm, tn, name):
    m, k0 = a0.shape
    k1 = a1.shape[1]
    n_out = w.shape[1]
    return pl.pallas_call(
        _mm2_kernel,
        out_shape=jax.ShapeDtypeStruct((m, n_out), F32),
        grid=(n_out // tn, m // tm),
        in_specs=[pl.BlockSpec((tm, k0), lambda j, i: (i, 0)),
                  pl.BlockSpec((tm, k1), lambda j, i: (i, 0)),
                  pl.BlockSpec((k0 + k1, tn), lambda j, i: (0, j))],
        out_specs=pl.BlockSpec((tm, tn), lambda j, i: (i, j)),
        scratch_shapes=[pltpu.VMEM((k0 + k1, tn), BF16)],
        compiler_params=_cparams(("arbitrary", "arbitrary"), 56),
        name=name,
    )(a0, a1, w)


def _conv_kernel(main_ref, prev_ref, next_ref, w_ref, b_ref, o_ref, ext_ref):
    i = pl.program_id(0)
    rb = main_ref.shape[0]
    halo = V7X_SUBLANES
    pad = (SSM_CONV - 1) // 2
    seq_start = jnp.logical_or(i == 0, i == 1)
    seq_end = jnp.logical_or(i == 0, i == pl.num_programs(0) - 1)
    ext_ref[0:halo, :] = jnp.where(seq_start, 0.0, prev_ref[...])
    ext_ref[halo:halo + rb, :] = main_ref[...]
    ext_ref[halo + rb:2 * halo + rb, :] = jnp.where(seq_end, 0.0, next_ref[...])
    acc = jnp.broadcast_to(b_ref[...], o_ref.shape)
    for k in range(SSM_CONV):
        lo = halo - pad + k
        acc = acc + w_ref[k:k + 1, :] * ext_ref[lo:lo + rb, :]
    o_ref[...] = acc * jax.nn.sigmoid(acc)


def _conv_silu(proj, conv_w8, conv_b, rb, tc):
    rows = proj.shape[0]
    n_row_tiles = rows // rb
    halo_per_tile = rb // V7X_SUBLANES
    n_halo_blocks = rows // V7X_SUBLANES
    col0 = COL_XBC // tc
    return pl.pallas_call(
        _conv_kernel,
        out_shape=jax.ShapeDtypeStruct((rows, SSM_XBC), F32),
        grid=(n_row_tiles, SSM_XBC // tc),
        in_specs=[pl.BlockSpec((rb, tc), lambda i, j: (i, col0 + j)),
                  pl.BlockSpec((V7X_SUBLANES, tc),
                               lambda i, j: (jnp.maximum(i * halo_per_tile - 1, 0), col0 + j)),
                  pl.BlockSpec((V7X_SUBLANES, tc),
                               lambda i, j: (jnp.minimum((i + 1) * halo_per_tile, n_halo_blocks - 1), col0 + j)),
                  pl.BlockSpec((V7X_SUBLANES, tc), lambda i, j: (0, j)),
                  pl.BlockSpec((1, tc), lambda i, j: (0, j))],
        out_specs=pl.BlockSpec((rb, tc), lambda i, j: (i, j)),
        scratch_shapes=[pltpu.VMEM((rb + 2 * V7X_SUBLANES, tc), F32)],
        compiler_params=_cparams(("arbitrary", "arbitrary"), 32),
        name="conv_silu",
    )(proj, proj, proj, conv_w8, conv_b)


def _gmlp_kernel(uv_ref, gain_ref, bias_ref, ws_ref, bst_ref, o_ref):
    g = jax.nn.gelu(uv_ref[...])
    u = g[:, 0:GM_WIDTH]
    v = _ln_rows(g[:, GM_WIDTH:2 * GM_WIDTH]) * gain_ref[...] + bias_ref[...]
    vb = v.astype(BF16)
    for h in range(GM_HEADS):
        cols = slice(h * GM_HEAD_DIM, (h + 1) * GM_HEAD_DIM)
        mixed = jnp.dot(ws_ref[h], vb[:, cols], preferred_element_type=F32) + bst_ref[:, h:h + 1]
        o_ref[:, cols] = (u[:, cols] * mixed).astype(BF16)


def _chunk_mlp(proj, lat_row0, n_chunks, v_gain, v_bias, ws_bf, bs_t):
    blk0 = lat_row0 // CHUNK
    return pl.pallas_call(
        _gmlp_kernel,
        out_shape=jax.ShapeDtypeStruct((n_chunks * CHUNK, GM_WIDTH), BF16),
        grid=(n_chunks,),
        in_specs=[pl.BlockSpec((CHUNK, 2 * GM_WIDTH), lambda c: (blk0 + c, 0)),
                  pl.BlockSpec((1, GM_WIDTH), lambda c: (0, 0)),
                  pl.BlockSpec((1, GM_WIDTH), lambda c: (0, 0)),
                  pl.BlockSpec((GM_HEADS, CHUNK, CHUNK), lambda c: (0, 0, 0)),
                  pl.BlockSpec((CHUNK, GM_HEADS), lambda c: (0, 0))],
        out_specs=pl.BlockSpec((CHUNK, GM_WIDTH), lambda c: (c, 0)),
        compiler_params=_cparams(("arbitrary",), 32),
        name="chunk_mlp",
    )(proj, v_gain, v_bias, ws_bf, bs_t)


def _ssd_chunk(direction, xs_ref, b_ref, c_ref, dtda_ref, mcum_ref, expand_ref, st_ref, emit_group):
    d = direction
    lane0 = d * SSM_HEADS
    dt = dtda_ref[:, 0:V7X_LANES]
    d_a = dtda_ref[:, V7X_LANES:2 * V7X_LANES]
    cs = jnp.dot(mcum_ref[...], d_a, preferred_element_type=F32, precision=lax.Precision.HIGHEST)
    cs_t = cs.T
    last = SSD_CHUNK - 1 if d == 0 else 0
    cs_end = cs[last:last + 1, :]
    dt_decay = dt * jnp.exp(cs_end - cs)
    exp_cs = jnp.exp(cs)
    chunk_decay = jnp.broadcast_to(jnp.exp(cs_end), (V7X_SUBLANES, V7X_LANES))
    q = jnp.concatenate([dt, dt_decay, chunk_decay], axis=0)
    q_hi = q.astype(BF16)
    q_lo = (q - q_hi.astype(F32)).astype(BF16)

    row_i = lax.broadcasted_iota(I32, (SSD_CHUNK, SSD_CHUNK), 0)
    col_i = lax.broadcasted_iota(I32, (SSD_CHUNK, SSD_CHUNK), 1)
    causal = (row_i >= col_i) if d == 0 else (row_i <= col_i)
    first_head = col_i < SSM_HEAD_DIM

    for g in range(SSM_GROUPS):
        gcols = slice(g * GROUP_WIDTH, (g + 1) * GROUP_WIDTH)
        ncols = slice(g * SSM_STATE, (g + 1) * SSM_STATE)
        c32 = c_ref[:, ncols]
        b_bf = b_ref[:, ncols].astype(BF16)
        c_bf = c32.astype(BF16)
        cb = lax.dot_general(c_bf, b_bf, (((1,), (1,)), ((), ())), preferred_element_type=F32)
        e_g = expand_ref[:, gcols]
        q_e = (jnp.dot(q_hi, e_g, preferred_element_type=F32)
               + jnp.dot(q_lo, e_g, preferred_element_type=F32))
        xs_g = xs_ref[:, gcols]
        x_dt = (xs_g * q_e[0:SSD_CHUNK]).astype(BF16)
        x_dec = (xs_g * q_e[SSD_CHUNK:2 * SSD_CHUNK]).astype(BF16)
        h_old = st_ref[g]
        h_bf = h_old.astype(BF16)
        y_pairs = []
        for pr in range(HEADS_PER_GROUP // 2):
            pcols = slice(pr * V7X_LANES, (pr + 1) * V7X_LANES)
            xp = x_dt[:, pcols]
            hp = h_bf[:, pcols]
            zero = jnp.zeros_like(xp)
            rhs = jnp.concatenate([jnp.where(first_head, xp, zero), jnp.where(first_head, hp, zero),
                                   jnp.where(first_head, zero, xp), jnp.where(first_head, zero, hp)], axis=0)
            parts = []
            for k in range(2):
                lane = lane0 + g * HEADS_PER_GROUP + 2 * pr + k
                seg = jnp.exp(jnp.where(causal, cs[:, lane:lane + 1] - cs_t[lane:lane + 1, :], -jnp.inf))
                parts.append((cb * seg).astype(BF16))
                parts.append((c32 * exp_cs[:, lane:lane + 1]).astype(BF16))
            lhs = jnp.concatenate(parts, axis=1)
            y_pairs.append(jnp.dot(lhs, rhs, preferred_element_type=F32))
        emit_group(g, xs_g, jnp.concatenate(y_pairs, axis=1))
        s_new = lax.dot_general(b_bf, x_dec, (((0,), (0,)), ((), ())), preferred_element_type=F32)
        st_ref[g] = h_old * q_e[2 * SSD_CHUNK:2 * SSD_CHUNK + 1] + s_new


def _ssd_fwd_kernel(xs_ref, b_ref, c_ref, dtda_ref, mcum_ref, expand_ref, y_ref, st_ref):
    @pl.when(pl.program_id(0) == 0)
    def _():
        st_ref[...] = jnp.zeros_like(st_ref)

    def emit(g, xs_g, y_g):
        y_ref[:, g * GROUP_WIDTH:(g + 1) * GROUP_WIDTH] = y_g

    _ssd_chunk(0, xs_ref, b_ref, c_ref, dtda_ref, mcum_ref, expand_ref, st_ref, emit)


def _ssd_bwd_kernel(xs_ref, b_ref, c_ref, dtda_ref, mcum_ref, expand_ref, yf_ref, z_ref,
                    dskip_ref, ngain_ref, o_ref, st_ref):
    @pl.when(pl.program_id(0) == 0)
    def _():
        st_ref[...] = jnp.zeros_like(st_ref)

    def emit(g, xs_g, y_g):
        gcols = slice(g * GROUP_WIDTH, (g + 1) * GROUP_WIDTH)
        y = y_g + yf_ref[:, gcols] + dskip_ref[:, gcols] * xs_g
        gated = y * jax.nn.silu(z_ref[:, gcols])
        ms = jnp.mean(gated * gated, axis=-1, keepdims=True)
        o_ref[:, gcols] = (gated * lax.rsqrt(ms + RMS_EPS) * ngain_ref[:, gcols]).astype(BF16)

    _ssd_chunk(1, xs_ref, b_ref, c_ref, dtda_ref, mcum_ref, expand_ref, st_ref, emit)


def _ssd_specs(chunk_of_step):
    xs_blk = SSM_WIDTH
    n_blk = SSM_GROUPS * SSM_STATE
    return [pl.BlockSpec((SSD_CHUNK, xs_blk), lambda i: (chunk_of_step(i), 0)),
            pl.BlockSpec((SSD_CHUNK, n_blk), lambda i: (chunk_of_step(i), SSM_WIDTH // n_blk)),
            pl.BlockSpec((SSD_CHUNK, n_blk), lambda i: (chunk_of_step(i), SSM_WIDTH // n_blk + 1)),
            pl.BlockSpec((SSD_CHUNK, 2 * V7X_LANES), lambda i: (chunk_of_step(i), 0)),
            pl.BlockSpec((SSD_CHUNK, SSD_CHUNK), lambda i: (0, 0)),
            pl.BlockSpec((V7X_LANES, SSM_WIDTH), lambda i: (0, 0))]


def _ssd_forward(conv, dtda, n_ctx_chunks, n_lat_chunks):
    n_steps = n_ctx_chunks + n_lat_chunks
    tri = (jnp.arange(SSD_CHUNK)[:, None] >= jnp.arange(SSD_CHUNK)[None, :]).astype(F32)
    expand = _head_expand(0)
    return pl.pallas_call(
        _ssd_fwd_kernel,
        out_shape=jax.ShapeDtypeStruct((n_lat_chunks * SSD_CHUNK, SSM_WIDTH), F32),
        grid=(n_steps,),
        in_specs=_ssd_specs(lambda i: i),
        out_specs=pl.BlockSpec((SSD_CHUNK, SSM_WIDTH), lambda i: (jnp.maximum(i - n_ctx_chunks, 0), 0)),
        scratch_shapes=[pltpu.VMEM((SSM_GROUPS, SSM_STATE, GROUP_WIDTH), F32)],
        compiler_params=_cparams(("arbitrary",), 40),
        name="ssd_forward",
    )(conv, conv, conv, dtda, tri, expand)


def _ssd_backward(conv, dtda, y_fwd, proj, d_skip, norm_gain, n_ctx_chunks, n_lat_chunks):
    n_steps = n_ctx_chunks + n_lat_chunks
    tri = (jnp.arange(SSD_CHUNK)[:, None] <= jnp.arange(SSD_CHUNK)[None, :]).astype(F32)
    expand = _head_expand(1)

    def chunk_of_step(i):
        return jnp.where(i < n_ctx_chunks, n_ctx_chunks - 1 - i, n_steps - 1 + n_ctx_chunks - i)

    def lat_of_step(i):
        return jnp.where(i < n_ctx_chunks, n_lat_chunks - 1, n_steps - 1 - i)

    return pl.pallas_call(
        _ssd_bwd_kernel,
        out_shape=jax.ShapeDtypeStruct((n_lat_chunks * SSD_CHUNK, SSM_WIDTH), BF16),
        grid=(n_steps,),
        in_specs=_ssd_specs(chunk_of_step) + [
            pl.BlockSpec((SSD_CHUNK, SSM_WIDTH), lambda i: (lat_of_step(i), 0)),
            pl.BlockSpec((SSD_CHUNK, SSM_WIDTH), lambda i: (chunk_of_step(i), COL_Z // SSM_WIDTH)),
            pl.BlockSpec((1, SSM_WIDTH), lambda i: (0, 0)),
            pl.BlockSpec((1, SSM_WIDTH), lambda i: (0, 0))],
        out_specs=pl.BlockSpec((SSD_CHUNK, SSM_WIDTH), lambda i: (lat_of_step(i), 0)),
        scratch_shapes=[pltpu.VMEM((SSM_GROUPS, SSM_STATE, GROUP_WIDTH), F32)],
        compiler_params=_cparams(("arbitrary",), 40),
        name="ssd_backward",
    )(conv, conv, conv, dtda, tri, expand, y_fwd, proj, d_skip, norm_gain)


def _head_expand(direction):
    lane = jnp.arange(V7X_LANES)[:, None]
    chan = jnp.arange(SSM_WIDTH)[None, :]
    return (lane == direction * SSM_HEADS + chan // SSM_HEAD_DIM).astype(BF16)


def _postmix_kernel(x_ref, mix_ref, p_ref, lng_ref, lnb_ref, wrt_ref, hfp_ref, acc_ref, aff_ref, hf_scr):
    gate_m = p_ref[0, 2:3, :]
    shift_f = p_ref[0, 3:4, :]
    scale1_f = 1.0 + p_ref[0, 4:5, :]
    tm = x_ref.shape[0]
    half = D_MODEL // 2

    def body(r, carry):
        sl = pl.ds(pl.multiple_of(r * ROW_STRIP, ROW_STRIP), ROW_STRIP)
        xn = _ln_rows(ALPHA * x_ref[sl, :] + gate_m * mix_ref[sl, :]) * lng_ref[...] + lnb_ref[...]
        acc_ref[sl, :] = ALPHA * xn
        hf = (_ln_rows(xn) * scale1_f + shift_f).astype(BF16)
        hf_scr[sl, :] = hf
        lo = lax.bitcast_convert_type(hf[:, 0:half].astype(F32), U32)
        hi = lax.bitcast_convert_type(hf[:, half:].astype(F32), U32)
        hfp_ref[sl, :] = (lo >> 16) | (hi & jnp.uint32(0xFFFF0000))
        return carry

    lax.fori_loop(0, tm // ROW_STRIP, body, 0)
    logits = lax.dot_general(wrt_ref[...], hf_scr[...], (((1,), (1,)), ((), ())),
                             preferred_element_type=F32)
    mx = jnp.max(logits, axis=0, keepdims=True)
    ex = jnp.exp(logits - mx)
    aff_ref[...] = ex / jnp.sum(ex, axis=0, keepdims=True)


def _post_mix(x2, mix, mods, ln_gain, ln_bias, wr_t, tm):
    l, d = x2.shape
    return pl.pallas_call(
        _postmix_kernel,
        out_shape=(jax.ShapeDtypeStruct((l, d // 2), U32),
                   jax.ShapeDtypeStruct((l, d), F32),
                   jax.ShapeDtypeStruct((N_EXPERTS, l), F32)),
        grid=(l // tm,),
        in_specs=[pl.BlockSpec((tm, d), lambda i: (i, 0)),
                  pl.BlockSpec((tm, d), lambda i: (i, 0)),
                  pl.BlockSpec((1, V7X_SUBLANES, d), lambda i: (0, 0, 0)),
                  pl.BlockSpec((1, d), lambda i: (0, 0)),
                  pl.BlockSpec((1, d), lambda i: (0, 0)),
                  pl.BlockSpec((N_EXPERTS, d), lambda i: (0, 0))],
        out_specs=(pl.BlockSpec((tm, d // 2), lambda i: (i, 0)),
                   pl.BlockSpec((tm, d), lambda i: (i, 0)),
                   pl.BlockSpec((N_EXPERTS, tm), lambda i: (0, i))),
        scratch_shapes=[pltpu.VMEM((tm, d), BF16)],
        compiler_params=_cparams(("arbitrary",), 48),
        name="post_mix",
    )(x2, mix, mods, ln_gain, ln_bias, wr_t)


def _topk_kernel(aff_ref, triu_ref, slow_ref, idx_ref, gate_ref, cum_scr, bend_v, bend_s, sem, *, cap):
    n_e, n_blk, n_lane = aff_ref.shape
    aff = aff_ref[...]
    bits = lax.bitcast_convert_type(aff, I32)

    def count(mask):
        c = jnp.sum(jnp.where(mask, 1.0, 0.0), axis=1, keepdims=True)
        return jnp.sum(c, axis=2, keepdims=True)

    def search(k, thr):
        cand = thr | jnp.left_shift(jnp.int32(1), 30 - k)
        return jnp.where(count(bits >= cand) >= cap, cand, thr)

    thr = lax.fori_loop(0, 31, search, jnp.zeros((n_e, 1, 1), I32))
    above = bits > thr
    equal = bits == thr
    need = cap - count(above)

    def prefix(mask):
        m = jnp.where(mask, 1.0, 0.0).astype(BF16)
        within = jnp.dot(m.reshape(n_e * n_blk, n_lane), triu_ref[...],
                         preferred_element_type=F32).reshape(n_e, n_blk, n_lane)
        tot = jnp.broadcast_to(within[:, :, n_lane - 1:n_lane], (n_e, n_blk, n_lane)).astype(BF16)
        offs = [jnp.dot(slow_ref[...], tot[e], preferred_element_type=F32) for e in range(n_e)]
        return within + jnp.stack(offs, axis=0)

    cum_above = prefix(above)
    cum_equal = prefix(equal)
    chosen = jnp.logical_or(above, jnp.logical_and(equal, cum_equal <= need))
    cum = cum_above + jnp.minimum(cum_equal, need)
    cum_scr[...] = jnp.where(chosen, cum, -1.0)

    last_lane = jnp.where(lax.broadcasted_iota(I32, (V7X_SUBLANES, n_lane), 1) == n_lane - 1, 1.0, 0.0)
    bend_v[...] = jnp.zeros_like(bend_v)
    for e in range(n_e):
        ends = lax.dot_general(last_lane, cum[e], (((1,), (1,)), ((), ())),
                               preferred_element_type=F32, precision=lax.Precision.HIGHEST)
        bend_v[e:e + 1, 0:n_blk] = ends[0:1, :].astype(I32)
    to_smem = pltpu.make_async_copy(bend_v, bend_s, sem)
    to_smem.start()
    to_smem.wait()

    jt = SLOT_TILE
    lane_f = lax.broadcasted_iota(I32, (jt, n_lane), 1).astype(F32)

    def per_expert(e, carry):
        def per_tile(t, b_start):
            first = t * jt
            b_lo = lax.while_loop(lambda b: jnp.logical_and(b < n_blk - 1, bend_s[e, b] <= first),
                                  lambda b: b + 1, b_start)
            b_hi = lax.while_loop(lambda b: jnp.logical_and(b < n_blk - 1, bend_s[e, b] < first + jt),
                                  lambda b: b + 1, b_lo)
            slot = (lax.broadcasted_iota(I32, (jt, n_lane), 0) + (first + 1)).astype(F32)

            def visit(b, accs):
                hit = cum_scr[e, pl.ds(b, 1), :] == slot
                return (accs[0] + jnp.where(hit, (b + 1).astype(F32), 0.0),
                        accs[1] + jnp.where(hit, aff_ref[e, pl.ds(b, 1), :], 0.0))

            zeros = jnp.zeros((jt, n_lane), F32)
            blk_acc, gate_acc = lax.fori_loop(b_lo, b_hi + 1, visit, (zeros, zeros))
            tok = jnp.where(blk_acc > 0.0, (blk_acc - 1.0) * n_lane + lane_f, 0.0)
            tok = jnp.sum(tok, axis=1, keepdims=True)
            gate = jnp.sum(gate_acc, axis=1, keepdims=True)
            rows = pl.ds(pl.multiple_of(t * jt, jt), jt)
            idx_ref[e, rows, :] = jnp.broadcast_to(tok, (jt, n_lane)).astype(I32)
            gate_ref[e, rows, :] = jnp.broadcast_to(gate, (jt, n_lane))
            return b_lo
        lax.fori_loop(0, cap // jt, per_tile, jnp.int32(0))
        return carry

    lax.fori_loop(0, n_e, per_expert, 0)


def _expert_choice_topk(aff_t, cap):
    n_e, l = aff_t.shape
    n_blk = l // V7X_LANES
    aff3 = aff_t.reshape(n_e, n_blk, V7X_LANES)
    triu = (jnp.arange(V7X_LANES)[:, None] <= jnp.arange(V7X_LANES)[None, :]).astype(BF16)
    slow = (jnp.arange(n_blk)[:, None] > jnp.arange(n_blk)[None, :]).astype(BF16)
    return pl.pallas_call(
        functools.partial(_topk_kernel, cap=cap),
        out_shape=(jax.ShapeDtypeStruct((n_e, cap, V7X_LANES), I32),
                   jax.ShapeDtypeStruct((n_e, cap, V7X_LANES), F32)),
        grid=(1,),
        in_specs=[pl.BlockSpec((n_e, n_blk, V7X_LANES), lambda i: (0, 0, 0)),
                  pl.BlockSpec((V7X_LANES, V7X_LANES), lambda i: (0, 0)),
                  pl.BlockSpec((n_blk, n_blk), lambda i: (0, 0))],
        out_specs=(pl.BlockSpec((n_e, cap, V7X_LANES), lambda i: (0, 0, 0)),
                   pl.BlockSpec((n_e, cap, V7X_LANES), lambda i: (0, 0, 0))),
        scratch_shapes=[pltpu.VMEM((n_e, n_blk, V7X_LANES), F32),
                        pltpu.VMEM((n_e, V7X_LANES), I32), pltpu.SMEM((n_e, V7X_LANES), I32),
                        pltpu.SemaphoreType.DMA(())],
        compiler_params=_cparams(("arbitrary",), 48),
        name="expert_choice_topk",
    )(aff3, triu, slow)


def _row_copy(src, dst, src_row, dst_row, sem):
    return pltpu.make_async_copy(src.at[pl.ds(src_row, 1), :], dst.at[pl.ds(dst_row, 1), :], sem)


def _wait_rows_in(hbm, buf, sem):
    pltpu.make_async_copy(hbm.at[pl.ds(0, buf.shape[0]), :], buf, sem).wait()


def _wait_rows_out(buf, hbm, sem):
    pltpu.make_async_copy(buf, hbm.at[pl.ds(0, buf.shape[0]), :], sem).wait()


def _ffn_kernel(idx_ref, hfp, wg_ref, wu_ref, wd_ref, gate_ref, gf_ref, o_ref,
                xe_scr, h_scr, stage, sems, *, n_ff_tiles, n_experts, n_steps, cap):
    e = pl.program_id(0)
    s = pl.program_id(1)
    rows = cap // n_steps
    half = D_MODEL // 2
    tf = wg_ref.shape[2]

    def issue(expert, chunk, slot):
        base = expert * cap + chunk * rows
        for j in range(rows):
            _row_copy(hfp, stage.at[slot], idx_ref[base + j], j, sems.at[slot]).start()

    def land(buf, chunk, slot):
        _wait_rows_in(hfp, stage.at[slot], sems.at[slot])
        w = stage[slot]
        first = chunk * rows
        r = pl.ds(first if isinstance(first, int) else pl.multiple_of(first, rows), rows)
        xe_scr[buf, r, 0:half] = lax.bitcast_convert_type(w << 16, F32).astype(BF16)
        xe_scr[buf, r, half:] = lax.bitcast_convert_type(w & jnp.uint32(0xFFFF0000), F32).astype(BF16)

    @pl.when(jnp.logical_and(e == 0, s == 0))
    def _():
        issue(0, 0, 0)
        for k in range(n_steps - 1):
            issue(0, k + 1, (k + 1) % 2)
            land(0, k, k % 2)

    def step_io():
        prev = (s + n_steps - 1) % n_steps
        owner = e + jnp.where(s > 0, 1, 0)
        land(owner % 2, prev, prev % 2)
        issue((e + 1) % n_experts, s, s % 2)

    @pl.when(s < n_ff_tiles)
    def _():
        step_io()
        xe = xe_scr[e % 2]
        a = jnp.dot(xe, wg_ref[0].astype(BF16), preferred_element_type=F32)
        u = jnp.dot(xe, wu_ref[0].astype(BF16), preferred_element_type=F32)
        h_scr[:, pl.ds(pl.multiple_of(s * tf, tf), tf)] = (jax.nn.silu(a) * u).astype(BF16)

    @pl.when(s >= n_ff_tiles)
    def _():
        step_io()
        y = jnp.dot(h_scr[...], wd_ref[0].astype(BF16), preferred_element_type=F32)
        o_ref[0] = (y * gate_ref[0, :, 0:1] * gf_ref[...]).astype(BF16)

    @pl.when(jnp.logical_and(e == n_experts - 1, s == n_steps - 1))
    def _():
        last = (n_steps - 1) % 2
        _wait_rows_in(hfp, stage.at[last], sems.at[last])


def _expert_ffn(idx_flat, hf_packed, w_gate, w_up, w_down, gates, gate_f, cap, tf, tn):
    n_e, d, ff = w_gate.shape
    n_ff = ff // tf
    n_out = d // tn
    n_steps = n_ff + n_out
    assert cap % n_steps == 0 and ff % tf == 0 and d % tn == 0
    rows = cap // n_steps
    col_a = lambda e, s, idx: (e, 0, jnp.minimum(s, n_ff - 1))
    col_b = lambda e, s, idx: (e, 0, jnp.maximum(s - n_ff, 0))
    return pl.pallas_call(
        functools.partial(_ffn_kernel, n_ff_tiles=n_ff, n_experts=n_e, n_steps=n_steps, cap=cap),
        out_shape=jax.ShapeDtypeStruct((n_e, cap, d), BF16),
        grid_spec=pltpu.PrefetchScalarGridSpec(
            num_scalar_prefetch=1, grid=(n_e, n_steps),
            in_specs=[pl.BlockSpec(memory_space=pl.ANY),
                      pl.BlockSpec((1, d, tf), col_a),
                      pl.BlockSpec((1, d, tf), col_a),
                      pl.BlockSpec((1, ff, tn), col_b),
                      pl.BlockSpec((1, cap, V7X_LANES), lambda e, s, idx: (e, 0, 0), pipeline_mode=pl.Buffered(1)),
                      pl.BlockSpec((1, tn), lambda e, s, idx: (0, jnp.maximum(s - n_ff, 0)))],
            out_specs=pl.BlockSpec((1, cap, tn), col_b),
            scratch_shapes=[pltpu.VMEM((2, cap, d), BF16), pltpu.VMEM((cap, ff), BF16),
                            pltpu.VMEM((2, rows, d // 2), U32), pltpu.SemaphoreType.DMA((2,))]),
        compiler_params=_cparams(("arbitrary", "arbitrary"), 60),
        name="expert_ffn",
    )(idx_flat, hf_packed, w_gate, w_up, w_down, gates, gate_f)


def _combine_kernel(idx_ref, ye_ref, acc_in, acc_out, buf, gsem, ssem, *, cap, n_tiles):
    e = pl.program_id(0)
    m = pl.program_id(1)
    n_buf, n_grp, grp, _ = buf.shape
    tm = n_grp * grp

    def for_rows(tile, fn):
        def body(g, carry):
            first = e * cap + tile * tm + g * grp
            for u in range(grp):
                fn(g, u, idx_ref[first + u])
            return carry
        lax.fori_loop(0, n_grp, body, 0)

    def fetch(tile, b):
        for_rows(tile, lambda g, u, tok: _row_copy(acc_in, buf.at[b, g], tok, u, gsem.at[b]).start())

    def put(tile, b):
        for_rows(tile, lambda g, u, tok: _row_copy(buf.at[b, g], acc_out, u, tok, ssem.at[b]).start())

    def fetch_wait(b):
        for g in range(n_grp):
            _wait_rows_in(acc_in, buf.at[b, g], gsem.at[b])

    def put_wait(b):
        for g in range(n_grp):
            _wait_rows_out(buf.at[b, g], acc_out, ssem.at[b])

    for t in range(n_tiles):
        @pl.when(m == t)
        def _(t=t):
            b = t % n_buf
            if t == 0:
                fetch(0, 0)
                if n_tiles > 1:
                    fetch(1, 1 % n_buf)
            elif t + 1 < n_tiles:
                nb = (t + 1) % n_buf
                if t + 1 >= n_buf:
                    put_wait(nb)
                fetch(t + 1, nb)
            fetch_wait(b)
            for g in range(n_grp):
                buf[b, g] = buf[b, g] + ye_ref[0, g * grp:(g + 1) * grp, :].astype(F32)
            put(t, b)
            if t == n_tiles - 1:
                for u in range(max(0, n_tiles - n_buf), n_tiles):
                    put_wait(u % n_buf)


def _combine(idx_flat, ye, acc, tm, n_buf):
    n_e, cap, d = ye.shape
    n_tiles = cap // tm
    return pl.pallas_call(
        functools.partial(_combine_kernel, cap=cap, n_tiles=n_tiles),
        out_shape=jax.ShapeDtypeStruct(acc.shape, acc.dtype),
        grid_spec=pltpu.PrefetchScalarGridSpec(
            num_scalar_prefetch=1, grid=(n_e, n_tiles),
            in_specs=[pl.BlockSpec((1, tm, d), lambda e, m, idx: (e, m, 0)),
                      pl.BlockSpec(memory_space=pl.ANY)],
            out_specs=pl.BlockSpec(memory_space=pl.ANY),
            scratch_shapes=[pltpu.VMEM((n_buf, tm // DMA_GROUP, DMA_GROUP, d), F32),
                            pltpu.SemaphoreType.DMA((n_buf,)), pltpu.SemaphoreType.DMA((n_buf,))]),
        input_output_aliases={2: 0},
        compiler_params=_cparams(("arbitrary", "arbitrary"), 32),
        name="combine",
    )(idx_flat, ye, acc)


def _final_ln_kernel(a_ref, g_ref, b_ref, o_ref):
    def body(r, carry):
        sl = pl.ds(pl.multiple_of(r * ROW_STRIP, ROW_STRIP), ROW_STRIP)
        o_ref[sl, :] = _ln_rows(a_ref[sl, :]) * g_ref[...] + b_ref[...]
        return carry
    lax.fori_loop(0, a_ref.shape[0] // ROW_STRIP, body, 0)


def _final_ln(acc, gain, bias, tm):
    l, d = acc.shape
    return pl.pallas_call(
        _final_ln_kernel,
        out_shape=jax.ShapeDtypeStruct((l, d), F32),
        grid=(l // tm,),
        in_specs=[pl.BlockSpec((tm, d), lambda i: (i, 0)),
                  pl.BlockSpec((1, d), lambda i: (0, 0)),
                  pl.BlockSpec((1, d), lambda i: (0, 0))],
        out_specs=pl.BlockSpec((tm, d), lambda i: (i, 0)),
        compiler_params=_cparams(("arbitrary",), 32),
        name="final_ln",
    )(acc, gain, bias)


def _pad_rows(a, rows):
    return jnp.pad(a, ((0, rows - a.shape[0]), (0, 0)))


def _pad_lanes(a, lanes):
    return jnp.pad(a, ((0, 0), (0, lanes - a.shape[1])))


def kernel(x, c, ctx, c_ctx, w_ada, b_ada, w_in, gm_v_gain, gm_v_bias, gm_w_s, gm_b_s, ssm_conv_w, ssm_conv_b, ssm_dt_bias, ssm_a_log, ssm_d, ssm_norm_gain, w_out, ln_mix_gain, ln_mix_bias, w_router, w_gate, w_up, w_down, ln_ffn_gain, ln_ffn_bias):
    b, l, d = x.shape
    lc = ctx.shape[1]
    assert b == 1 and d == D_MODEL and w_ada.shape[0] == DEPTH
    x2 = x[0]
    ctx2 = ctx[0]
    layer = 0

    cond = _pad_rows(jnp.concatenate([c, c_ctx[None, :]], axis=0), V7X_SUBLANES)
    ada = _ada_params(cond, w_ada[layer], b_ada[layer][None, :])
    mods = jnp.pad(ada[:2].reshape(2, 6, d), ((0, 0), (0, V7X_SUBLANES - 6), (0, 0)))

    w_in_t = jnp.swapaxes(w_in, 1, 2)
    wdt = _pad_rows(w_in_t[layer, COL_DT:, :], V7X_LANES).astype(BF16)
    dt_bias = _pad_lanes(ssm_dt_bias[layer].reshape(1, 2 * SSM_HEADS), V7X_LANES)
    a_log = _pad_lanes(ssm_a_log[layer].reshape(1, 2 * SSM_HEADS), V7X_LANES)
    h, dtda = _modulate_in(x2, ctx2, mods, wdt, dt_bias, a_log)
    proj = _matmul_nt(h, w_in_t, COL_DT, tm=1408, tn=512, name="in_proj")

    conv_w8 = _pad_rows(ssm_conv_w[layer], V7X_SUBLANES)
    conv = _conv_silu(proj, conv_w8, ssm_conv_b[layer][None, :], rb=lc, tc=1024)

    n_lat_chunks = l // CHUNK
    n_ctx_chunks = lc // SSD_CHUNK
    gm = _chunk_mlp(proj, lc, n_lat_chunks, gm_v_gain[layer][None, :], gm_v_bias[layer][None, :],
                    gm_w_s[layer].astype(BF16), gm_b_s[layer].T)

    y_fwd = _ssd_forward(conv, dtda, n_ctx_chunks, n_lat_chunks)
    d_skip = jnp.repeat(ssm_d[layer], SSM_HEAD_DIM)[None, :]
    ss = _ssd_backward(conv, dtda, y_fwd, proj, d_skip, ssm_norm_gain[layer][None, :],
                       n_ctx_chunks, n_lat_chunks)

    mix = _matmul_concat(gm, ss, w_out[layer], tm=1024, tn=512, name="out_proj")

    hf_packed, acc, aff_t = _post_mix(x2, mix, mods, ln_mix_gain[layer][None, :], ln_mix_bias[layer][None, :],
                                      w_router[layer].T.astype(BF16), tm=256)

    cap = EC_FACTOR * l // N_EXPERTS
    idx3, gates = _expert_choice_topk(aff_t, cap)
    idx_flat = idx3[:, :, 0].reshape(N_EXPERTS * cap)

    gate_f = mods[0, 5:6, :]
    ye = _expert_ffn(idx_flat, hf_packed, w_gate[layer], w_up[layer], w_down[layer], gates, gate_f,
                     cap, tf=256, tn=512)
    acc = _combine(idx_flat, ye, acc, tm=256, n_buf=3)
    out = _final_ln(acc, ln_ffn_gain[layer][None, :], ln_ffn_bias[layer][None, :], tm=256)
    return out[None]
```

```python
import functools

import jax
import jax.numpy as jnp
from jax import lax
from jax.experimental import pallas as pl
from jax.experimental.pallas import tpu as pltpu

F32 = jnp.float32
BF16 = jnp.bfloat16
I32 = jnp.int32

D_MODEL = 4096
GRID_W = 64
CHUNK = 128
GM_WIDTH = 2048
GM_HEADS = 8
GM_HEAD_DIM = GM_WIDTH // GM_HEADS
SSM_WIDTH = 2048
SSM_HEAD_DIM = 64
SSM_HEADS = SSM_WIDTH // SSM_HEAD_DIM
SSM_GROUPS = 8
SSM_STATE = 128
SSM_CONV = 5
SSM_XBC = SSM_WIDTH + 2 * SSM_GROUPS * SSM_STATE
SSD_CHUNK = 128
HEADS_PER_GROUP = SSM_HEADS // SSM_GROUPS
GROUP_WIDTH = SSM_WIDTH // SSM_GROUPS
N_EXPERTS = 16
EXPERT_FF = 2048
EC_FACTOR = 2
DEPTH = 1
ALPHA = (2 * DEPTH) ** 0.25
LN_EPS = 1e-5
RMS_EPS = 1e-5

COL_Z = 2 * GM_WIDTH
COL_XBC = COL_Z + SSM_WIDTH
COL_DT = COL_XBC + SSM_XBC

V7X_LANES = 128
V7X_SUBLANES = 8
V7X_VMEM_BYTES = 64 * 1024 * 1024
MIB = 1024 * 1024

ROW_STRIP = 64
SLOT_TILE = 128
DMA_GROUP = 32
STAGE_SLOTS = 4


def _cparams(semantics, vmem_mib):
    return pltpu.CompilerParams(dimension_semantics=semantics,
                                vmem_limit_bytes=min(vmem_mib * MIB, V7X_VMEM_BYTES - 2 * MIB))


def _ln_rows(x):
    mu = jnp.mean(x, axis=-1, keepdims=True)
    xc = x - mu
    var = jnp.mean(xc * xc, axis=-1, keepdims=True)
    return xc * lax.rsqrt(var + LN_EPS)


def _ada_kernel(c_ref, w_ref, b_ref, o_ref):
    s = jax.nn.silu(c_ref[...]).astype(BF16)
    o_ref[...] = jnp.dot(s, w_ref[...].astype(BF16), preferred_element_type=F32) + b_ref[...]


def _ada_params(cond, w_ada, b_ada):
    k, n = w_ada.shape
    tn = 512
    return pl.pallas_call(
        _ada_kernel,
        out_shape=jax.ShapeDtypeStruct((V7X_SUBLANES, n), F32),
        grid=(n // tn,),
        in_specs=[pl.BlockSpec((V7X_SUBLANES, k), lambda j: (0, 0)),
                  pl.BlockSpec((k, tn), lambda j: (0, j)),
                  pl.BlockSpec((1, tn), lambda j: (0, j))],
        out_specs=pl.BlockSpec((V7X_SUBLANES, tn), lambda j: (0, j)),
        compiler_params=_cparams(("arbitrary",), 32),
        name="ada_params",
    )(cond, w_ada, b_ada)


def _modin_kernel(x_ref, ctx_ref, p_ref, wdt_ref, bias_ref, alog_ref, h_ref, dtda_ref):
    i = pl.program_id(0)
    shift = p_ref[0, 0:1, :]
    scale1 = 1.0 + p_ref[0, 1:2, :]
    tm = h_ref.shape[0]

    def rows(src_ref):
        def body(r, carry):
            sl = pl.ds(pl.multiple_of(r * ROW_STRIP, ROW_STRIP), ROW_STRIP)
            h_ref[sl, :] = (_ln_rows(src_ref[sl, :]) * scale1 + shift).astype(BF16)
            return carry
        lax.fori_loop(0, tm // ROW_STRIP, body, 0)

    @pl.when(i == 0)
    def _():
        rows(ctx_ref)

    @pl.when(i > 0)
    def _():
        rows(x_ref)

    raw = lax.dot_general(h_ref[...], wdt_ref[...], (((1,), (1,)), ((), ())),
                          preferred_element_type=F32) + bias_ref[...]
    dt = jnp.maximum(raw, 0.0) + jnp.log1p(jnp.exp(-jnp.abs(raw)))
    dtda_ref[:, 0:V7X_LANES] = dt
    dtda_ref[:, V7X_LANES:2 * V7X_LANES] = dt * (-jnp.exp(alog_ref[...]))


def _modulate_in(x2, ctx2, mods, wdt, dt_bias, a_log):
    l, d = x2.shape
    lc = ctx2.shape[0]
    tm = lc
    n_tiles = (l + lc) // tm
    return pl.pallas_call(
        _modin_kernel,
        out_shape=(jax.ShapeDtypeStruct((l + lc, d), BF16),
                   jax.ShapeDtypeStruct((l + lc, 2 * V7X_LANES), F32)),
        grid=(n_tiles,),
        in_specs=[pl.BlockSpec((tm, d), lambda i: (jnp.maximum(i - 1, 0), 0)),
                  pl.BlockSpec((tm, d), lambda i: (0, 0)),
                  pl.BlockSpec((1, V7X_SUBLANES, d), lambda i: (jnp.where(i == 0, 1, 0), 0, 0)),
                  pl.BlockSpec((V7X_LANES, d), lambda i: (0, 0)),
                  pl.BlockSpec((1, V7X_LANES), lambda i: (0, 0)),
                  pl.BlockSpec((1, V7X_LANES), lambda i: (0, 0))],
        out_specs=(pl.BlockSpec((tm, d), lambda i: (i, 0)),
                   pl.BlockSpec((tm, 2 * V7X_LANES), lambda i: (i, 0))),
        compiler_params=_cparams(("arbitrary",), 40),
        name="modulate_in",
    )(x2, ctx2, mods, wdt, dt_bias, a_log)


def _mm_nt_kernel(a_ref, wt_ref, o_ref, wbf_ref):
    @pl.when(pl.program_id(1) == 0)
    def _():
        wbf_ref[...] = wt_ref[...].astype(BF16)

    o_ref[...] = lax.dot_general(a_ref[...], wbf_ref[...], (((1,), (1,)), ((), ())),
                                 preferred_element_type=F32)


def _matmul_nt(a, wt3, n_out, tm, tn, name):
    m, k = a.shape
    return pl.pallas_call(
        _mm_nt_kernel,
        out_shape=jax.ShapeDtypeStruct((m, n_out), F32),
        grid=(n_out // tn, m // tm),
        in_specs=[pl.BlockSpec((tm, k), lambda j, i: (i, 0)),
                  pl.BlockSpec((None, tn, k), lambda j, i: (0, j, 0))],
        out_specs=pl.BlockSpec((tm, tn), lambda j, i: (i, j)),
        scratch_shapes=[pltpu.VMEM((tn, k), BF16)],
        compiler_params=_cparams(("arbitrary", "arbitrary"), 56),
        name=name,
    )(a, wt3)


def _mm2_kernel(a0_ref, a1_ref, w_ref, o_ref, wbf_ref):
    @pl.when(pl.program_id(1) == 0)
    def _():
        wbf_ref[...] = w_ref[...].astype(BF16)

    k0 = a0_ref.shape[1]
    acc = jnp.dot(a0_ref[...], wbf_ref[0:k0, :], preferred_element_type=F32)
    o_ref[...] = acc + jnp.dot(a1_ref[...], wbf_ref[k0:, :], preferred_element_type=F32)


def _matmul_concat(a0, a1, w, tm, tn, name):
    m, k0 = a0.shape
    k1 = a1.shape[1]
    n_out = w.shape[1]
    return pl.pallas_call(
        _mm2_kernel,
        out_shape=jax.ShapeDtypeStruct((m, n_out), F32),
        grid=(n_out // tn, m // tm),
        in_specs=[pl.BlockSpec((tm, k0), lambda j, i: (i, 0)),
                  pl.BlockSpec((tm, k1), lambda j, i: (i, 0)),
                  pl.BlockSpec((k0 + k1, tn), lambda j, i: (0, j))],
        out_specs=pl.BlockSpec((tm, tn), lambda j, i: (i, j)),
        scratch_shapes=[pltpu.VMEM((k0 + k1, tn), BF16)],
        compiler_params=_cparams(("arbitrary", "arbitrary"), 56),
        name=name,
    )(a0, a1, w)


def _conv_kernel(main_ref, prev_ref, next_ref, w_ref, b_ref, o_ref, ext_ref):
    i = pl.program_id(0)
    rb = main_ref.shape[0]
    halo = V7X_SUBLANES
    pad = (SSM_CONV - 1) // 2
    seq_start = jnp.logical_or(i == 0, i == 1)
    seq_end = jnp.logical_or(i == 0, i == pl.num_programs(0) - 1)
    ext_ref[0:halo, :] = jnp.where(seq_start, 0.0, prev_ref[...])
    ext_ref[halo:halo + rb, :] = main_ref[...]
    ext_ref[halo + rb:2 * halo + rb, :] = jnp.where(seq_end, 0.0, next_ref[...])
    acc = jnp.broadcast_to(b_ref[...], o_ref.shape)
    for k in range(SSM_CONV):
        lo = halo - pad + k
        acc = acc + w_ref[k:k + 1, :] * ext_ref[lo:lo + rb, :]
    o_ref[...] = acc * jax.nn.sigmoid(acc)


def _conv_silu(proj, conv_w8, conv_b, rb, tc):
    rows = proj.shape[0]
    n_row_tiles = rows // rb
    halo_per_tile = rb // V7X_SUBLANES
    n_halo_blocks = rows // V7X_SUBLANES
    col0 = COL_XBC // tc
    return pl.pallas_call(
        _conv_kernel,
        out_shape=jax.ShapeDtypeStruct((rows, SSM_XBC), F32),
        grid=(n_row_tiles, SSM_XBC // tc),
        in_specs=[pl.BlockSpec((rb, tc), lambda i, j: (i, col0 + j)),
                  pl.BlockSpec((V7X_SUBLANES, tc),
                               lambda i, j: (jnp.maximum(i * halo_per_tile - 1, 0), col0 + j)),
                  pl.BlockSpec((V7X_SUBLANES, tc),
                               lambda i, j: (jnp.minimum((i + 1) * halo_per_tile, n_halo_blocks - 1), col0 + j)),
                  pl.BlockSpec((V7X_SUBLANES, tc), lambda i, j: (0, j)),
                  pl.BlockSpec((1, tc), lambda i, j: (0, j))],
        out_specs=pl.BlockSpec((rb, tc), lambda i, j: (i, j)),
        scratch_shapes=[pltpu.VMEM((rb + 2 * V7X_SUBLANES, tc), F32)],
        compiler_params=_cparams(("arbitrary", "arbitrary"), 32),
        name="conv_silu",
    )(proj, proj, proj, conv_w8, conv_b)


def _gmlp_kernel(uv_ref, gain_ref, bias_ref, ws_ref, bst_ref, o_ref):
    g = jax.nn.gelu(uv_ref[...])
    u = g[:, 0:GM_WIDTH]
    v = _ln_rows(g[:, GM_WIDTH:2 * GM_WIDTH]) * gain_ref[...] + bias_ref[...]
    vb = v.astype(BF16)
    for h in range(GM_HEADS):
        cols = slice(h * GM_HEAD_DIM, (h + 1) * GM_HEAD_DIM)
        mixed = jnp.dot(ws_ref[h], vb[:, cols], preferred_element_type=F32) + bst_ref[:, h:h + 1]
        o_ref[:, cols] = (u[:, cols] * mixed).astype(BF16)


def _chunk_mlp(proj, lat_row0, n_chunks, v_gain, v_bias, ws_bf, bs_t):
    blk0 = lat_row0 // CHUNK
    return pl.pallas_call(
        _gmlp_kernel,
        out_shape=jax.ShapeDtypeStruct((n_chunks * CHUNK, GM_WIDTH), BF16),
        grid=(n_chunks,),
        in_specs=[pl.BlockSpec((CHUNK, 2 * GM_WIDTH), lambda c: (blk0 + c, 0)),
                  pl.BlockSpec((1, GM_WIDTH), lambda c: (0, 0)),
                  pl.BlockSpec((1, GM_WIDTH), lambda c: (0, 0)),
                  pl.BlockSpec((GM_HEADS, CHUNK, CHUNK), lambda c: (0, 0, 0)),
                  pl.BlockSpec((CHUNK, GM_HEADS), lambda c: (0, 0))],
        out_specs=pl.BlockSpec((CHUNK, GM_WIDTH), lambda c: (c, 0)),
        compiler_params=_cparams(("arbitrary",), 32),
        name="chunk_mlp",
    )(proj, v_gain, v_bias, ws_bf, bs_t)


def _ssd_chunk(direction, xs_ref, b_ref, c_ref, dtda_ref, mcum_ref, expand_ref, st_ref, emit_group):
    d = direction
    lane0 = d * SSM_HEADS
    dt = dtda_ref[:, 0:V7X_LANES]
    d_a = dtda_ref[:, V7X_LANES:2 * V7X_LANES]
    cs = jnp.dot(mcum_ref[...], d_a, preferred_element_type=F32, precision=lax.Precision.HIGHEST)
    cs_t = cs.T
    last = SSD_CHUNK - 1 if d == 0 else 0
    cs_end = cs[last:last + 1, :]
    dt_decay = dt * jnp.exp(cs_end - cs)
    exp_cs = jnp.exp(cs)
    chunk_decay = jnp.broadcast_to(jnp.exp(cs_end), (V7X_SUBLANES, V7X_LANES))
    q = jnp.concatenate([dt, dt_decay, chunk_decay], axis=0)
    q_hi = q.astype(BF16)
    q_lo = (q - q_hi.astype(F32)).astype(BF16)

    row_i = lax.broadcasted_iota(I32, (SSD_CHUNK, SSD_CHUNK), 0)
    col_i = lax.broadcasted_iota(I32, (SSD_CHUNK, SSD_CHUNK), 1)
    causal = (row_i >= col_i) if d == 0 else (row_i <= col_i)
    first_head = col_i < SSM_HEAD_DIM

    for g in range(SSM_GROUPS):
        gcols = slice(g * GROUP_WIDTH, (g + 1) * GROUP_WIDTH)
        ncols = slice(g * SSM_STATE, (g + 1) * SSM_STATE)
        c32 = c_ref[:, ncols]
        b_bf = b_ref[:, ncols].astype(BF16)
        c_bf = c32.astype(BF16)
        cb = lax.dot_general(c_bf, b_bf, (((1,), (1,)), ((), ())), preferred_element_type=F32)
        e_g = expand_ref[:, gcols]
        q_e = (jnp.dot(q_hi, e_g, preferred_element_type=F32)
               + jnp.dot(q_lo, e_g, preferred_element_type=F32))
        xs_g = xs_ref[:, gcols]
        x_dt = (xs_g * q_e[0:SSD_CHUNK]).astype(BF16)
        x_dec = (xs_g * q_e[SSD_CHUNK:2 * SSD_CHUNK]).astype(BF16)
        h_old = st_ref[g]
        h_bf = h_old.astype(BF16)
        y_pairs = []
        for pr in range(HEADS_PER_GROUP // 2):
            pcols = slice(pr * V7X_LANES, (pr + 1) * V7X_LANES)
            xp = x_dt[:, pcols]
            hp = h_bf[:, pcols]
            zero = jnp.zeros_like(xp)
            rhs = jnp.concatenate([jnp.where(first_head, xp, zero), jnp.where(first_head, hp, zero),
                                   jnp.where(first_head, zero, xp), jnp.where(first_head, zero, hp)], axis=0)
            parts = []
            for k in range(2):
                lane = lane0 + g * HEADS_PER_GROUP + 2 * pr + k
                seg = jnp.exp(jnp.where(causal, cs[:, lane:lane + 1] - cs_t[lane:lane + 1, :], -jnp.inf))
                parts.append((cb * seg).astype(BF16))
                parts.append((c32 * exp_cs[:, lane:lane + 1]).astype(BF16))
            lhs = jnp.concatenate(parts, axis=1)
            y_pairs.append(jnp.dot(lhs, rhs, preferred_element_type=F32))
        emit_group(g, xs_g, jnp.concatenate(y_pairs, axis=1))
        s_new = lax.dot_general(b_bf, x_dec, (((0,), (0,)), ((), ())), preferred_element_type=F32)
        st_ref[g] = h_old * q_e[2 * SSD_CHUNK:2 * SSD_CHUNK + 1] + s_new


def _ssd_fwd_kernel(xs_ref, b_ref, c_ref, dtda_ref, mcum_ref, expand_ref, y_ref, st_ref):
    @pl.when(pl.program_id(0) == 0)
    def _():
        st_ref[...] = jnp.zeros_like(st_ref)

    def emit(g, xs_g, y_g):
        y_ref[:, g * GROUP_WIDTH:(g + 1) * GROUP_WIDTH] = y_g

    _ssd_chunk(0, xs_ref, b_ref, c_ref, dtda_ref, mcum_ref, expand_ref, st_ref, emit)


def _ssd_bwd_kernel(xs_ref, b_ref, c_ref, dtda_ref, mcum_ref, expand_ref, yf_ref, z_ref,
                    dskip_ref, ngain_ref, o_ref, st_ref):
    @pl.when(pl.program_id(0) == 0)
    def _():
        st_ref[...] = jnp.zeros_like(st_ref)

    def emit(g, xs_g, y_g):
        gcols = slice(g * GROUP_WIDTH, (g + 1) * GROUP_WIDTH)
        y = y_g + yf_ref[:, gcols] + dskip_ref[:, gcols] * xs_g
        gated = y * jax.nn.silu(z_ref[:, gcols])
        ms = jnp.mean(gated * gated, axis=-1, keepdims=True)
        o_ref[:, gcols] = (gated * lax.rsqrt(ms + RMS_EPS) * ngain_ref[:, gcols]).astype(BF16)

    _ssd_chunk(1, xs_ref, b_ref, c_ref, dtda_ref, mcum_ref, expand_ref, st_ref, emit)


def _ssd_specs(chunk_of_step):
    xs_blk = SSM_WIDTH
    n_blk = SSM_GROUPS * SSM_STATE
    return [pl.BlockSpec((SSD_CHUNK, xs_blk), lambda i: (chunk_of_step(i), 0)),
            pl.BlockSpec((SSD_CHUNK, n_blk), lambda i: (chunk_of_step(i), SSM_WIDTH // n_blk)),
            pl.BlockSpec((SSD_CHUNK, n_blk), lambda i: (chunk_of_step(i), SSM_WIDTH // n_blk + 1)),
            pl.BlockSpec((SSD_CHUNK, 2 * V7X_LANES), lambda i: (chunk_of_step(i), 0)),
            pl.BlockSpec((SSD_CHUNK, SSD_CHUNK), lambda i: (0, 0)),
            pl.BlockSpec((V7X_LANES, SSM_WIDTH), lambda i: (0, 0))]


def _ssd_forward(conv, dtda, n_ctx_chunks, n_lat_chunks):
    n_steps = n_ctx_chunks + n_lat_chunks
    tri = (jnp.arange(SSD_CHUNK)[:, None] >= jnp.arange(SSD_CHUNK)[None, :]).astype(F32)
    expand = _head_expand(0)
    return pl.pallas_call(
        _ssd_fwd_kernel,
        out_shape=jax.ShapeDtypeStruct((n_lat_chunks * SSD_CHUNK, SSM_WIDTH), F32),
        grid=(n_steps,),
        in_specs=_ssd_specs(lambda i: i),
        out_specs=pl.BlockSpec((SSD_CHUNK, SSM_WIDTH), lambda i: (jnp.maximum(i - n_ctx_chunks, 0), 0)),
        scratch_shapes=[pltpu.VMEM((SSM_GROUPS, SSM_STATE, GROUP_WIDTH), F32)],
        compiler_params=_cparams(("arbitrary",), 40),
        name="ssd_forward",
    )(conv, conv, conv, dtda, tri, expand)


def _ssd_backward(conv, dtda, y_fwd, proj, d_skip, norm_gain, n_ctx_chunks, n_lat_chunks):
    n_steps = n_ctx_chunks + n_lat_chunks
    tri = (jnp.arange(SSD_CHUNK)[:, None] <= jnp.arange(SSD_CHUNK)[None, :]).astype(F32)
    expand = _head_expand(1)

    def chunk_of_step(i):
        return jnp.where(i < n_ctx_chunks, n_ctx_chunks - 1 - i, n_steps - 1 + n_ctx_chunks - i)

    def lat_of_step(i):
        return jnp.where(i < n_ctx_chunks, n_lat_chunks - 1, n_steps - 1 - i)

    return pl.pallas_call(
        _ssd_bwd_kernel,
        out_shape=jax.ShapeDtypeStruct((n_lat_chunks * SSD_CHUNK, SSM_WIDTH), BF16),
        grid=(n_steps,),
        in_specs=_ssd_specs(chunk_of_step) + [
            pl.BlockSpec((SSD_CHUNK, SSM_WIDTH), lambda i: (lat_of_step(i), 0)),
            pl.BlockSpec((SSD_CHUNK, SSM_WIDTH), lambda i: (chunk_of_step(i), COL_Z // SSM_WIDTH)),
            pl.BlockSpec((1, SSM_WIDTH), lambda i: (0, 0)),
            pl.BlockSpec((1, SSM_WIDTH), lambda i: (0, 0))],
        out_specs=pl.BlockSpec((SSD_CHUNK, SSM_WIDTH), lambda i: (lat_of_step(i), 0)),
        scratch_shapes=[pltpu.VMEM((SSM_GROUPS, SSM_STATE, GROUP_WIDTH), F32)],
        compiler_params=_cparams(("arbitrary",), 40),
        name="ssd_backward",
    )(conv, conv, conv, dtda, tri, expand, y_fwd, proj, d_skip, norm_gain)


def _head_expand(direction):
    lane = jnp.arange(V7X_LANES)[:, None]
    chan = jnp.arange(SSM_WIDTH)[None, :]
    return (lane == direction * SSM_HEADS + chan // SSM_HEAD_DIM).astype(BF16)


def _postmix_kernel(x_ref, mix_ref, p_ref, lng_ref, lnb_ref, wrt_ref, hf_ref, acc_ref, aff_ref, hf_scr):
    gate_m = p_ref[0, 2:3, :]
    shift_f = p_ref[0, 3:4, :]
    scale1_f = 1.0 + p_ref[0, 4:5, :]
    tm = x_ref.shape[0]

    def body(r, carry):
        sl = pl.ds(pl.multiple_of(r * ROW_STRIP, ROW_STRIP), ROW_STRIP)
        xn = _ln_rows(ALPHA * x_ref[sl, :] + gate_m * mix_ref[sl, :]) * lng_ref[...] + lnb_ref[...]
        acc_ref[sl, :] = ALPHA * xn
        hf = _ln_rows(xn) * scale1_f + shift_f
        hf_ref[sl, :] = hf
        hf_scr[sl, :] = hf.astype(BF16)
        return carry

    lax.fori_loop(0, tm // ROW_STRIP, body, 0)
    logits = lax.dot_general(wrt_ref[...], hf_scr[...], (((1,), (1,)), ((), ())),
                             preferred_element_type=F32)
    mx = jnp.max(logits, axis=0, keepdims=True)
    ex = jnp.exp(logits - mx)
    aff_ref[...] = ex / jnp.sum(ex, axis=0, keepdims=True)


def _post_mix(x2, mix, mods, ln_gain, ln_bias, wr_t, tm):
    l, d = x2.shape
    return pl.pallas_call(
        _postmix_kernel,
        out_shape=(jax.ShapeDtypeStruct((l, d), F32),
                   jax.ShapeDtypeStruct((l, d), F32),
                   jax.ShapeDtypeStruct((N_EXPERTS, l), F32)),
        grid=(l // tm,),
        in_specs=[pl.BlockSpec((tm, d), lambda i: (i, 0)),
                  pl.BlockSpec((tm, d), lambda i: (i, 0)),
                  pl.BlockSpec((1, V7X_SUBLANES, d), lambda i: (0, 0, 0)),
                  pl.BlockSpec((1, d), lambda i: (0, 0)),
                  pl.BlockSpec((1, d), lambda i: (0, 0)),
                  pl.BlockSpec((N_EXPERTS, d), lambda i: (0, 0))],
        out_specs=(pl.BlockSpec((tm, d), lambda i: (i, 0)),
                   pl.BlockSpec((tm, d), lambda i: (i, 0)),
                   pl.BlockSpec((N_EXPERTS, tm), lambda i: (0, i))),
        scratch_shapes=[pltpu.VMEM((tm, d), BF16)],
        compiler_params=_cparams(("arbitrary",), 48),
        name="post_mix",
    )(x2, mix, mods, ln_gain, ln_bias, wr_t)


def _topk_kernel(aff_ref, triu_ref, slow_ref, idx_ref, gate_ref, cum_scr, bend_v, bend_s, sem, *, cap):
    n_e, n_blk, n_lane = aff_ref.shape
    aff = aff_ref[...]
    def count(mask):
        c = jnp.sum(jnp.where(mask, 1.0, 0.0), axis=1, keepdims=True)
        return jnp.sum(c, axis=2, keepdims=True)

    def as_float(bits):
        return lax.bitcast_convert_type(bits, F32)

    def search(k, thr):
        cand = thr | jnp.left_shift(jnp.int32(1), 30 - k)
        return jnp.where(count(aff >= as_float(cand)) >= cap, cand, thr)

    thr = lax.fori_loop(0, 31, search, jnp.zeros((n_e, 1, 1), I32))
    above = aff >= as_float(thr + 1)
    equal = jnp.logical_and(aff >= as_float(thr), jnp.logical_not(above))
    need = cap - count(above)

    def prefix(mask):
        m = jnp.where(mask, 1.0, 0.0).astype(BF16)
        within = jnp.dot(m.reshape(n_e * n_blk, n_lane), triu_ref[...],
                         preferred_element_type=F32).reshape(n_e, n_blk, n_lane)
        tot = jnp.broadcast_to(within[:, :, n_lane - 1:n_lane], (n_e, n_blk, n_lane)).astype(BF16)
        offs = [jnp.dot(slow_ref[...], tot[e], preferred_element_type=F32) for e in range(n_e)]
        return within + jnp.stack(offs, axis=0)

    cum_above = prefix(above)
    cum_equal = prefix(equal)
    chosen = jnp.logical_or(above, jnp.logical_and(equal, cum_equal <= need))
    cum = cum_above + jnp.minimum(cum_equal, need)
    cum_scr[...] = jnp.where(chosen, cum, -1.0)

    last_lane = jnp.where(lax.broadcasted_iota(I32, (V7X_SUBLANES, n_lane), 1) == n_lane - 1, 1.0, 0.0)
    bend_v[...] = jnp.zeros_like(bend_v)
    for e in range(n_e):
        ends = lax.dot_general(last_lane, cum[e], (((1,), (1,)), ((), ())),
                               preferred_element_type=F32, precision=lax.Precision.HIGHEST)
        bend_v[e:e + 1, 0:n_blk] = ends[0:1, :].astype(I32)
    to_smem = pltpu.make_async_copy(bend_v, bend_s, sem)
    to_smem.start()
    to_smem.wait()

    jt = SLOT_TILE
    lane_f = lax.broadcasted_iota(I32, (jt, n_lane), 1).astype(F32)

    def per_expert(e, carry):
        def per_tile(t, b_start):
            first = t * jt
            b_lo = lax.while_loop(lambda b: jnp.logical_and(b < n_blk - 1, bend_s[e, b] <= first),
                                  lambda b: b + 1, b_start)
            b_hi = lax.while_loop(lambda b: jnp.logical_and(b < n_blk - 1, bend_s[e, b] < first + jt),
                                  lambda b: b + 1, b_lo)
            slot = (lax.broadcasted_iota(I32, (jt, n_lane), 0) + (first + 1)).astype(F32)

            def visit(b, accs):
                hit = cum_scr[e, pl.ds(b, 1), :] == slot
                return (accs[0] + jnp.where(hit, (b + 1).astype(F32), 0.0),
                        accs[1] + jnp.where(hit, aff_ref[e, pl.ds(b, 1), :], 0.0))

            zeros = jnp.zeros((jt, n_lane), F32)
            blk_acc, gate_acc = lax.fori_loop(b_lo, b_hi + 1, visit, (zeros, zeros))
            tok = jnp.where(blk_acc > 0.0, (blk_acc - 1.0) * n_lane + lane_f, 0.0)
            tok = jnp.sum(tok, axis=1, keepdims=True)
            gate = jnp.sum(gate_acc, axis=1, keepdims=True)
            rows = pl.ds(pl.multiple_of(t * jt, jt), jt)
            idx_ref[e, rows, :] = jnp.broadcast_to(tok, (jt, n_lane)).astype(I32)
            gate_ref[e, rows, :] = jnp.broadcast_to(gate, (jt, n_lane))
            return b_lo
        lax.fori_loop(0, cap // jt, per_tile, jnp.int32(0))
        return carry

    lax.fori_loop(0, n_e, per_expert, 0)


def _expert_choice_topk(aff_t, cap):
    n_e, l = aff_t.shape
    n_blk = l // V7X_LANES
    aff3 = aff_t.reshape(n_e, n_blk, V7X_LANES)
    triu = (jnp.arange(V7X_LANES)[:, None] <= jnp.arange(V7X_LANES)[None, :]).astype(BF16)
    slow = (jnp.arange(n_blk)[:, None] > jnp.arange(n_blk)[None, :]).astype(BF16)
    return pl.pallas_call(
        functools.partial(_topk_kernel, cap=cap),
        out_shape=(jax.ShapeDtypeStruct((n_e, cap, V7X_LANES), I32),
                   jax.ShapeDtypeStruct((n_e, cap, V7X_LANES), F32)),
        grid=(1,),
        in_specs=[pl.BlockSpec((n_e, n_blk, V7X_LANES), lambda i: (0, 0, 0)),
                  pl.BlockSpec((V7X_LANES, V7X_LANES), lambda i: (0, 0)),
                  pl.BlockSpec((n_blk, n_blk), lambda i: (0, 0))],
        out_specs=(pl.BlockSpec((n_e, cap, V7X_LANES), lambda i: (0, 0, 0)),
                   pl.BlockSpec((n_e, cap, V7X_LANES), lambda i: (0, 0, 0))),
        scratch_shapes=[pltpu.VMEM((n_e, n_blk, V7X_LANES), F32),
                        pltpu.VMEM((n_e, V7X_LANES), I32), pltpu.SMEM((n_e, V7X_LANES), I32),
                        pltpu.SemaphoreType.DMA(())],
        compiler_params=_cparams(("arbitrary",), 48),
        name="expert_choice_topk",
    )(aff3, triu, slow)


def _row_copy(src, dst, src_row, dst_row, sem):
    return pltpu.make_async_copy(src.at[pl.ds(src_row, 1), :], dst.at[pl.ds(dst_row, 1), :], sem)


def _wait_rows_in(hbm, buf, sem):
    pltpu.make_async_copy(hbm.at[pl.ds(0, buf.shape[0]), :], buf, sem).wait()


def _wait_rows_out(buf, hbm, sem):
    pltpu.make_async_copy(buf, hbm.at[pl.ds(0, buf.shape[0]), :], sem).wait()


def _ffn_kernel(idx_ref, hf, wg_ref, wu_ref, wd_ref, gate_ref, gf_ref, o_ref,
                xe_scr, h_scr, stage, sems, *, n_ff_tiles, n_experts, n_steps, cap):
    e = pl.program_id(0)
    s = pl.program_id(1)
    rows = cap // n_steps
    tf = wg_ref.shape[2]

    def issue(expert, chunk, slot):
        base = expert * cap + chunk * rows
        for j in range(rows):
            _row_copy(hf, stage.at[slot], idx_ref[base + j], j, sems.at[slot]).start()

    def land(buf, chunk, slot):
        _wait_rows_in(hf, stage.at[slot], sems.at[slot])
        first = chunk * rows
        r = pl.ds(first if isinstance(first, int) else pl.multiple_of(first, rows), rows)
        xe_scr[buf, r, :] = stage[slot].astype(BF16)

    @pl.when(jnp.logical_and(e == 0, s == 0))
    def _():
        issue(0, 0, 0)
        for k in range(n_steps - 1):
            issue(0, k + 1, (k + 1) % STAGE_SLOTS)
            land(0, k, k % STAGE_SLOTS)
        issue(1 % n_experts, 0, 0)

    def step_io():
        due = (s + n_steps - 1) % n_steps
        owner = e + jnp.where(s > 0, 1, 0)
        land(owner % 2, due, due % STAGE_SLOTS)
        ahead = (s + 1) % n_steps
        target = e + 1 + jnp.where(s == n_steps - 1, 1, 0)
        issue(target % n_experts, ahead, ahead % STAGE_SLOTS)

    @pl.when(s < n_ff_tiles)
    def _():
        step_io()
        xe = xe_scr[e % 2]
        a = jnp.dot(xe, wg_ref[0].astype(BF16), preferred_element_type=F32)
        u = jnp.dot(xe, wu_ref[0].astype(BF16), preferred_element_type=F32)
        h_scr[:, pl.ds(pl.multiple_of(s * tf, tf), tf)] = (jax.nn.silu(a) * u).astype(BF16)

    @pl.when(s >= n_ff_tiles)
    def _():
        step_io()
        y = jnp.dot(h_scr[...], wd_ref[0].astype(BF16), preferred_element_type=F32)
        o_ref[0] = (y * gate_ref[0, :, 0:1] * gf_ref[...]).astype(BF16)

    @pl.when(jnp.logical_and(e == n_experts - 1, s == n_steps - 1))
    def _():
        for chunk in (n_steps - 1, 0):
            _wait_rows_in(hf, stage.at[chunk % STAGE_SLOTS], sems.at[chunk % STAGE_SLOTS])


def _expert_ffn(idx_flat, hf_rows, w_gate, w_up, w_down, gates, gate_f, cap, tf, tn):
    n_e, d, ff = w_gate.shape
    n_ff = ff // tf
    n_out = d // tn
    n_steps = n_ff + n_out
    assert cap % n_steps == 0 and ff % tf == 0 and d % tn == 0 and n_steps % STAGE_SLOTS == 0
    rows = cap // n_steps
    col_a = lambda e, s, idx: (e, 0, jnp.minimum(s, n_ff - 1))
    col_b = lambda e, s, idx: (e, 0, jnp.maximum(s - n_ff, 0))
    return pl.pallas_call(
        functools.partial(_ffn_kernel, n_ff_tiles=n_ff, n_experts=n_e, n_steps=n_steps, cap=cap),
        out_shape=jax.ShapeDtypeStruct((n_e, cap, d), BF16),
        grid_spec=pltpu.PrefetchScalarGridSpec(
            num_scalar_prefetch=1, grid=(n_e, n_steps),
            in_specs=[pl.BlockSpec(memory_space=pl.ANY),
                      pl.BlockSpec((1, d, tf), col_a),
                      pl.BlockSpec((1, d, tf), col_a),
                      pl.BlockSpec((1, ff, tn), col_b),
                      pl.BlockSpec((1, cap, V7X_LANES), lambda e, s, idx: (e, 0, 0), pipeline_mode=pl.Buffered(1)),
                      pl.BlockSpec((1, tn), lambda e, s, idx: (0, jnp.maximum(s - n_ff, 0)))],
            out_specs=pl.BlockSpec((1, cap, tn), col_b),
            scratch_shapes=[pltpu.VMEM((2, cap, d), BF16), pltpu.VMEM((cap, ff), BF16),
                            pltpu.VMEM((STAGE_SLOTS, rows, d), F32),
                            pltpu.SemaphoreType.DMA((STAGE_SLOTS,))]),
        compiler_params=_cparams(("arbitrary", "arbitrary"), 60),
        name="expert_ffn",
    )(idx_flat, hf_rows, w_gate, w_up, w_down, gates, gate_f)


def _combine_kernel(idx_ref, ye_ref, acc_in, acc_out, buf, gsem, ssem, *, cap, n_tiles):
    e = pl.program_id(0)
    m = pl.program_id(1)
    n_buf, n_grp, grp, _ = buf.shape
    tm = n_grp * grp

    def for_rows(tile, fn):
        def body(g, carry):
            first = e * cap + tile * tm + g * grp
            for u in range(grp):
                fn(g, u, idx_ref[first + u])
            return carry
        lax.fori_loop(0, n_grp, body, 0)

    def fetch(tile, b):
        for_rows(tile, lambda g, u, tok: _row_copy(acc_in, buf.at[b, g], tok, u, gsem.at[b]).start())

    def put(tile, b):
        for_rows(tile, lambda g, u, tok: _row_copy(buf.at[b, g], acc_out, u, tok, ssem.at[b]).start())

    def fetch_wait(b):
        for g in range(n_grp):
            _wait_rows_in(acc_in, buf.at[b, g], gsem.at[b])

    def put_wait(b):
        for g in range(n_grp):
            _wait_rows_out(buf.at[b, g], acc_out, ssem.at[b])

    for t in range(n_tiles):
        @pl.when(m == t)
        def _(t=t):
            b = t % n_buf
            if t == 0:
                fetch(0, 0)
                if n_tiles > 1:
                    fetch(1, 1 % n_buf)
            elif t + 1 < n_tiles:
                nb = (t + 1) % n_buf
                if t + 1 >= n_buf:
                    put_wait(nb)
                fetch(t + 1, nb)
            fetch_wait(b)
            for g in range(n_grp):
                buf[b, g] = buf[b, g] + ye_ref[0, g * grp:(g + 1) * grp, :].astype(F32)
            put(t, b)
            if t == n_tiles - 1:
                for u in range(max(0, n_tiles - n_buf), n_tiles):
                    put_wait(u % n_buf)


def _combine(idx_flat, ye, acc, tm, n_buf):
    n_e, cap, d = ye.shape
    n_tiles = cap // tm
    return pl.pallas_call(
        functools.partial(_combine_kernel, cap=cap, n_tiles=n_tiles),
        out_shape=jax.ShapeDtypeStruct(acc.shape, acc.dtype),
        grid_spec=pltpu.PrefetchScalarGridSpec(
            num_scalar_prefetch=1, grid=(n_e, n_tiles),
            in_specs=[pl.BlockSpec((1, tm, d), lambda e, m, idx: (e, m, 0)),
                      pl.BlockSpec(memory_space=pl.ANY)],
            out_specs=pl.BlockSpec(memory_space=pl.ANY),
            scratch_shapes=[pltpu.VMEM((n_buf, tm // DMA_GROUP, DMA_GROUP, d), F32),
                            pltpu.SemaphoreType.DMA((n_buf,)), pltpu.SemaphoreType.DMA((n_buf,))]),
        input_output_aliases={2: 0},
        compiler_params=_cparams(("arbitrary", "arbitrary"), 32),
        name="combine",
    )(idx_flat, ye, acc)


def _final_ln_kernel(a_ref, g_ref, b_ref, o_ref):
    def body(r, carry):
        sl = pl.ds(pl.multiple_of(r * ROW_STRIP, ROW_STRIP), ROW_STRIP)
        o_ref[sl, :] = _ln_rows(a_ref[sl, :]) * g_ref[...] + b_ref[...]
        return carry
    lax.fori_loop(0, a_ref.shape[0] // ROW_STRIP, body, 0)


def _final_ln(acc, gain, bias, tm):
    l, d = acc.shape
    return pl.pallas_call(
        _final_ln_kernel,
        out_shape=jax.ShapeDtypeStruct((l, d), F32),
        grid=(l // tm,),
        in_specs=[pl.BlockSpec((tm, d), lambda i: (i, 0)),
                  pl.BlockSpec((1, d), lambda i: (0, 0)),
                  pl.BlockSpec((1, d), lambda i: (0, 0))],
        out_specs=pl.BlockSpec((tm, d), lambda i: (i, 0)),
        compiler_params=_cparams(("arbitrary",), 32),
        name="final_ln",
    )(acc, gain, bias)


def _pad_rows(a, rows):
    return jnp.pad(a, ((0, rows - a.shape[0]), (0, 0)))


def _pad_lanes(a, lanes):
    return jnp.pad(a, ((0, 0), (0, lanes - a.shape[1])))


def kernel(x, c, ctx, c_ctx, w_ada, b_ada, w_in, gm_v_gain, gm_v_bias, gm_w_s, gm_b_s, ssm_conv_w, ssm_conv_b, ssm_dt_bias, ssm_a_log, ssm_d, ssm_norm_gain, w_out, ln_mix_gain, ln_mix_bias, w_router, w_gate, w_up, w_down, ln_ffn_gain, ln_ffn_bias):
    b, l, d = x.shape
    lc = ctx.shape[1]
    assert b == 1 and d == D_MODEL and w_ada.shape[0] == DEPTH
    x2 = x[0]
    ctx2 = ctx[0]
    layer = 0

    cond = _pad_rows(jnp.concatenate([c, c_ctx[None, :]], axis=0), V7X_SUBLANES)
    ada = _ada_params(cond, w_ada[layer], b_ada[layer][None, :])
    mods = jnp.pad(ada[:2].reshape(2, 6, d), ((0, 0), (0, V7X_SUBLANES - 6), (0, 0)))

    w_in_t = jnp.swapaxes(w_in, 1, 2)
    wdt = _pad_rows(w_in_t[layer, COL_DT:, :], V7X_LANES).astype(BF16)
    dt_bias = _pad_lanes(ssm_dt_bias[layer].reshape(1, 2 * SSM_HEADS), V7X_LANES)
    a_log = _pad_lanes(ssm_a_log[layer].reshape(1, 2 * SSM_HEADS), V7X_LANES)
    h, dtda = _modulate_in(x2, ctx2, mods, wdt, dt_bias, a_log)
    proj = _matmul_nt(h, w_in_t, COL_DT, tm=1408, tn=512, name="in_proj")

    conv_w8 = _pad_rows(ssm_conv_w[layer], V7X_SUBLANES)
    conv = _conv_silu(proj, conv_w8, ssm_conv_b[layer][None, :], rb=lc, tc=1024)

    n_lat_chunks = l // CHUNK
    n_ctx_chunks = lc // SSD_CHUNK
    gm = _chunk_mlp(proj, lc, n_lat_chunks, gm_v_gain[layer][None, :], gm_v_bias[layer][None, :],
                    gm_w_s[layer].astype(BF16), gm_b_s[layer].T)

    y_fwd = _ssd_forward(conv, dtda, n_ctx_chunks, n_lat_chunks)
    d_skip = jnp.repeat(ssm_d[layer], SSM_HEAD_DIM)[None, :]
    ss = _ssd_backward(conv, dtda, y_fwd, proj, d_skip, ssm_norm_gain[layer][None, :],
                       n_ctx_chunks, n_lat_chunks)

    mix = _matmul_concat(gm, ss, w_out[layer], tm=1024, tn=512, name="out_proj")

    hf_rows, acc, aff_t = _post_mix(x2, mix, mods, ln_mix_gain[layer][None, :], ln_mix_bias[layer][None, :],
                                      w_router[layer].T.astype(BF16), tm=256)

    cap = EC_FACTOR * l // N_EXPERTS
    idx3, gates = _expert_choice_topk(aff_t, cap)
    idx_flat = idx3[:, :, 0].reshape(N_EXPERTS * cap)

    gate_f = mods[0, 5:6, :]
    ye = _expert_ffn(idx_flat, hf_rows, w_gate[layer], w_up[layer], w_down[layer], gates, gate_f,
                     cap, tf=256, tn=512)
    acc = _combine(idx_flat, ye, acc, tm=256, n_buf=3)
    out = _final_ln(acc, ln_ffn_gain[layer][None, :], ln_ffn_bias[layer][None, :], tm=256)
    return out[None]
```

```python
import functools

import jax
import jax.numpy as jnp
from jax import lax
from jax.experimental import pallas as pl
from jax.experimental.pallas import tpu as pltpu

F32 = jnp.float32
BF16 = jnp.bfloat16
I32 = jnp.int32

D_MODEL = 4096
GRID_W = 64
CHUNK = 128
GM_WIDTH = 2048
GM_HEADS = 8
GM_HEAD_DIM = GM_WIDTH // GM_HEADS
SSM_WIDTH = 2048
SSM_HEAD_DIM = 64
SSM_HEADS = SSM_WIDTH // SSM_HEAD_DIM
SSM_GROUPS = 8
SSM_STATE = 128
SSM_CONV = 5
SSM_XBC = SSM_WIDTH + 2 * SSM_GROUPS * SSM_STATE
SSD_CHUNK = 128
HEADS_PER_GROUP = SSM_HEADS // SSM_GROUPS
GROUP_WIDTH = SSM_WIDTH // SSM_GROUPS
N_EXPERTS = 16
EXPERT_FF = 2048
EC_FACTOR = 2
DEPTH = 1
ALPHA = (2 * DEPTH) ** 0.25
LN_EPS = 1e-5
RMS_EPS = 1e-5

COL_Z = 2 * GM_WIDTH
COL_XBC = COL_Z + SSM_WIDTH
COL_DT = COL_XBC + SSM_XBC

V7X_LANES = 128
V7X_SUBLANES = 8
V7X_VMEM_BYTES = 64 * 1024 * 1024
MIB = 1024 * 1024

ROW_STRIP = 128
SLOT_TILE = 128
DMA_GROUP = 32
STAGE_SLOTS = 4


def _cparams(semantics, vmem_mib):
    return pltpu.CompilerParams(dimension_semantics=semantics,
                                vmem_limit_bytes=min(vmem_mib * MIB, V7X_VMEM_BYTES - 2 * MIB))


def _ln_rows(x):
    mu = jnp.mean(x, axis=-1, keepdims=True)
    xc = x - mu
    var = jnp.mean(xc * xc, axis=-1, keepdims=True)
    return xc * lax.rsqrt(var + LN_EPS)


def _ada_kernel(c_ref, w_ref, b_ref, o_ref):
    s = jax.nn.silu(c_ref[...]).astype(BF16)
    o_ref[...] = jnp.dot(s, w_ref[...].astype(BF16), preferred_element_type=F32) + b_ref[...]


def _ada_params(cond, w_ada, b_ada):
    k, n = w_ada.shape
    tn = 512
    return pl.pallas_call(
        _ada_kernel,
        out_shape=jax.ShapeDtypeStruct((V7X_SUBLANES, n), F32),
        grid=(n // tn,),
        in_specs=[pl.BlockSpec((V7X_SUBLANES, k), lambda j: (0, 0)),
                  pl.BlockSpec((k, tn), lambda j: (0, j)),
                  pl.BlockSpec((1, tn), lambda j: (0, j))],
        out_specs=pl.BlockSpec((V7X_SUBLANES, tn), lambda j: (0, j)),
        compiler_params=_cparams(("arbitrary",), 32),
        name="ada_params",
    )(cond, w_ada, b_ada)


def _modin_kernel(x_ref, ctx_ref, p_ref, wdt_ref, bias_ref, alog_ref, h_ref, dtda_ref):
    i = pl.program_id(0)
    shift = p_ref[0, 0:1, :]
    scale1 = 1.0 + p_ref[0, 1:2, :]
    tm = h_ref.shape[0]

    def rows(src_ref):
        def body(r, carry):
            sl = pl.ds(pl.multiple_of(r * ROW_STRIP, ROW_STRIP), ROW_STRIP)
            h_ref[sl, :] = (_ln_rows(src_ref[sl, :]) * scale1 + shift).astype(BF16)
            return carry
        lax.fori_loop(0, tm // ROW_STRIP, body, 0)

    @pl.when(i == 0)
    def _():
        rows(ctx_ref)

    @pl.when(i > 0)
    def _():
        rows(x_ref)

    raw = lax.dot_general(h_ref[...], wdt_ref[...], (((1,), (1,)), ((), ())),
                          preferred_element_type=F32) + bias_ref[...]
    dt = jnp.maximum(raw, 0.0) + jnp.log1p(jnp.exp(-jnp.abs(raw)))
    dtda_ref[:, 0:V7X_LANES] = dt
    dtda_ref[:, V7X_LANES:2 * V7X_LANES] = dt * (-jnp.exp(alog_ref[...]))


def _modulate_in(x2, ctx2, mods, wdt, dt_bias, a_log):
    l, d = x2.shape
    lc = ctx2.shape[0]
    tm = lc
    n_tiles = (l + lc) // tm
    return pl.pallas_call(
        _modin_kernel,
        out_shape=(jax.ShapeDtypeStruct((l + lc, d), BF16),
                   jax.ShapeDtypeStruct((l + lc, 2 * V7X_LANES), F32)),
        grid=(n_tiles,),
        in_specs=[pl.BlockSpec((tm, d), lambda i: (jnp.maximum(i - 1, 0), 0)),
                  pl.BlockSpec((tm, d), lambda i: (0, 0)),
                  pl.BlockSpec((1, V7X_SUBLANES, d), lambda i: (jnp.where(i == 0, 1, 0), 0, 0)),
                  pl.BlockSpec((V7X_LANES, d), lambda i: (0, 0)),
                  pl.BlockSpec((1, V7X_LANES), lambda i: (0, 0)),
                  pl.BlockSpec((1, V7X_LANES), lambda i: (0, 0))],
        out_specs=(pl.BlockSpec((tm, d), lambda i: (i, 0)),
                   pl.BlockSpec((tm, 2 * V7X_LANES), lambda i: (i, 0))),
        compiler_params=_cparams(("arbitrary",), 40),
        name="modulate_in",
    )(x2, ctx2, mods, wdt, dt_bias, a_log)


def _mm_nt_kernel(a_ref, wt_ref, o_ref, wbf_ref):
    @pl.when(pl.program_id(1) == 0)
    def _():
        wbf_ref[...] = wt_ref[...].astype(BF16)

    o_ref[...] = lax.dot_general(a_ref[...], wbf_ref[...], (((1,), (1,)), ((), ())),
                                 preferred_element_type=F32)


def _matmul_nt(a, wt3, n_out, tm, tn, name):
    m, k = a.shape
    return pl.pallas_call(
        _mm_nt_kernel,
        out_shape=jax.ShapeDtypeStruct((m, n_out), F32),
        grid=(n_out // tn, m // tm),
        in_specs=[pl.BlockSpec((tm, k), lambda j, i: (i, 0)),
                  pl.BlockSpec((None, tn, k), lambda j, i: (0, j, 0))],
        out_specs=pl.BlockSpec((tm, tn), lambda j, i: (i, j)),
        scratch_shapes=[pltpu.VMEM((tn, k), BF16)],
        compiler_params=_cparams(("arbitrary", "arbitrary"), 56),
        name=name,
    )(a, wt3)


def _mm2_kernel(a0_ref, a1_ref, w_ref, o_ref, wbf_ref):
    @pl.when(pl.program_id(1) == 0)
    def _():
        wbf_ref[...] = w_ref[...].astype(BF16)

    k0 = a0_ref.shape[1]
    acc = jnp.dot(a0_ref[...], wbf_ref[0:k0, :], preferred_element_type=F32)
    o_ref[...] = acc + jnp.dot(a1_ref[...], wbf_ref[k0:, :], preferred_element_type=F32)


def _matmul_concat(a0, a1, w, tm, tn, name):
    m, k0 = a0.shape
    k1 = a1.shape[1]
    n_out = w.shape[1]
    return pl.pallas_call(
        _mm2_kernel,
        out_shape=jax.ShapeDtypeStruct((m, n_out), F32),
        grid=(n_out // tn, m // tm),
        in_specs=[pl.BlockSpec((tm, k0), lambda j, i: (i, 0)),
                  pl.BlockSpec((tm, k1), lambda j, i: (i, 0)),
                  pl.BlockSpec((k0 + k1, tn), lambda j, i: (0, j))],
        out_specs=pl.BlockSpec((tm, tn), lambda j, i: (i, j)),
        scratch_shapes=[pltpu.VMEM((k0 + k1, tn), BF16)],
        compiler_params=_cparams(("arbitrary", "arbitrary"), 56),
        name=name,
    )(a0, a1, w)


def _conv_kernel(main_ref, prev_ref, next_ref, w_ref, b_ref, o_ref, ext_ref):
    i = pl.program_id(0)
    rb = main_ref.shape[0]
    halo = V7X_SUBLANES
    pad = (SSM_CONV - 1) // 2
    seq_start = jnp.logical_or(i == 0, i == 1)
    seq_end = jnp.logical_or(i == 0, i == pl.num_programs(0) - 1)
    ext_ref[0:halo, :] = jnp.where(seq_start, 0.0, prev_ref[...])
    ext_ref[halo:halo + rb, :] = main_ref[...]
    ext_ref[halo + rb:2 * halo + rb, :] = jnp.where(seq_end, 0.0, next_ref[...])
    ext = ext_ref[...]
    n_ext = ext.shape[0]
    acc = jnp.broadcast_to(b_ref[...], o_ref.shape)
    for k in range(SSM_CONV):
        shifted = ext if k == pad else pltpu.roll(ext, (pad - k) % n_ext, axis=0)
        acc = acc + w_ref[k:k + 1, :] * shifted[halo:halo + rb, :]
    o_ref[...] = acc * jax.nn.sigmoid(acc)


def _conv_silu(proj, conv_w8, conv_b, rb, tc):
    rows = proj.shape[0]
    n_row_tiles = rows // rb
    halo_per_tile = rb // V7X_SUBLANES
    n_halo_blocks = rows // V7X_SUBLANES
    col0 = COL_XBC // tc
    return pl.pallas_call(
        _conv_kernel,
        out_shape=jax.ShapeDtypeStruct((rows, SSM_XBC), F32),
        grid=(n_row_tiles, SSM_XBC // tc),
        in_specs=[pl.BlockSpec((rb, tc), lambda i, j: (i, col0 + j)),
                  pl.BlockSpec((V7X_SUBLANES, tc),
                               lambda i, j: (jnp.maximum(i * halo_per_tile - 1, 0), col0 + j)),
                  pl.BlockSpec((V7X_SUBLANES, tc),
                               lambda i, j: (jnp.minimum((i + 1) * halo_per_tile, n_halo_blocks - 1), col0 + j)),
                  pl.BlockSpec((V7X_SUBLANES, tc), lambda i, j: (0, j)),
                  pl.BlockSpec((1, tc), lambda i, j: (0, j))],
        out_specs=pl.BlockSpec((rb, tc), lambda i, j: (i, j)),
        scratch_shapes=[pltpu.VMEM((rb + 2 * V7X_SUBLANES, tc), F32)],
        compiler_params=_cparams(("arbitrary", "arbitrary"), 32),
        name="conv_silu",
    )(proj, proj, proj, conv_w8, conv_b)


def _gmlp_kernel(uv_ref, gain_ref, bias_ref, ws_ref, bst_ref, o_ref):
    g = jax.nn.gelu(uv_ref[...])
    u = g[:, 0:GM_WIDTH]
    v = _ln_rows(g[:, GM_WIDTH:2 * GM_WIDTH]) * gain_ref[...] + bias_ref[...]
    vb = v.astype(BF16)
    for h in range(GM_HEADS):
        cols = slice(h * GM_HEAD_DIM, (h + 1) * GM_HEAD_DIM)
        mixed = jnp.dot(ws_ref[h], vb[:, cols], preferred_element_type=F32) + bst_ref[:, h:h + 1]
        o_ref[:, cols] = (u[:, cols] * mixed).astype(BF16)


def _chunk_mlp(proj, lat_row0, n_chunks, v_gain, v_bias, ws_bf, bs_t):
    blk0 = lat_row0 // CHUNK
    return pl.pallas_call(
        _gmlp_kernel,
        out_shape=jax.ShapeDtypeStruct((n_chunks * CHUNK, GM_WIDTH), BF16),
        grid=(n_chunks,),
        in_specs=[pl.BlockSpec((CHUNK, 2 * GM_WIDTH), lambda c: (blk0 + c, 0)),
                  pl.BlockSpec((1, GM_WIDTH), lambda c: (0, 0)),
                  pl.BlockSpec((1, GM_WIDTH), lambda c: (0, 0)),
                  pl.BlockSpec((GM_HEADS, CHUNK, CHUNK), lambda c: (0, 0, 0)),
                  pl.BlockSpec((CHUNK, GM_HEADS), lambda c: (0, 0))],
        out_specs=pl.BlockSpec((CHUNK, GM_WIDTH), lambda c: (c, 0)),
        compiler_params=_cparams(("arbitrary",), 32),
        name="chunk_mlp",
    )(proj, v_gain, v_bias, ws_bf, bs_t)


def _ssd_chunk(direction, xs_ref, b_ref, c_ref, dtda_ref, mcum_ref, expand_ref, st_ref, emit_group):
    d = direction
    lane0 = d * SSM_HEADS
    dt = dtda_ref[:, 0:V7X_LANES]
    d_a = dtda_ref[:, V7X_LANES:2 * V7X_LANES]
    cs = jnp.dot(mcum_ref[...], d_a, preferred_element_type=F32, precision=lax.Precision.HIGHEST)
    cs_t = cs.T
    last = SSD_CHUNK - 1 if d == 0 else 0
    cs_end = cs[last:last + 1, :]
    dt_decay = dt * jnp.exp(cs_end - cs)
    exp_cs = jnp.exp(cs)
    chunk_decay = jnp.broadcast_to(jnp.exp(cs_end), (V7X_SUBLANES, V7X_LANES))
    q = jnp.concatenate([dt, dt_decay, chunk_decay], axis=0)
    q_hi = q.astype(BF16)
    q_lo = (q - q_hi.astype(F32)).astype(BF16)

    row_i = lax.broadcasted_iota(I32, (SSD_CHUNK, SSD_CHUNK), 0)
    col_i = lax.broadcasted_iota(I32, (SSD_CHUNK, SSD_CHUNK), 1)
    causal = (row_i >= col_i) if d == 0 else (row_i <= col_i)
    first_head = col_i < SSM_HEAD_DIM

    for g in range(SSM_GROUPS):
        gcols = slice(g * GROUP_WIDTH, (g + 1) * GROUP_WIDTH)
        ncols = slice(g * SSM_STATE, (g + 1) * SSM_STATE)
        c32 = c_ref[:, ncols]
        b_bf = b_ref[:, ncols].astype(BF16)
        c_bf = c32.astype(BF16)
        cb = lax.dot_general(c_bf, b_bf, (((1,), (1,)), ((), ())), preferred_element_type=F32)
        e_g = expand_ref[:, gcols]
        q_e = (jnp.dot(q_hi, e_g, preferred_element_type=F32)
               + jnp.dot(q_lo, e_g, preferred_element_type=F32))
        xs_g = xs_ref[:, gcols]
        x_dt = (xs_g * q_e[0:SSD_CHUNK]).astype(BF16)
        x_dec = (xs_g * q_e[SSD_CHUNK:2 * SSD_CHUNK]).astype(BF16)
        h_old = st_ref[g]
        h_bf = h_old.astype(BF16)
        y_pairs = []
        for pr in range(HEADS_PER_GROUP // 2):
            pcols = slice(pr * V7X_LANES, (pr + 1) * V7X_LANES)
            xp = x_dt[:, pcols]
            hp = h_bf[:, pcols]
            zero = jnp.zeros_like(xp)
            rhs = jnp.concatenate([jnp.where(first_head, xp, zero), jnp.where(first_head, hp, zero),
                                   jnp.where(first_head, zero, xp), jnp.where(first_head, zero, hp)], axis=0)
            parts = []
            for k in range(2):
                lane = lane0 + g * HEADS_PER_GROUP + 2 * pr + k
                seg = jnp.exp(jnp.where(causal, cs[:, lane:lane + 1] - cs_t[lane:lane + 1, :], -jnp.inf))
                parts.append((cb * seg).astype(BF16))
                parts.append((c32 * exp_cs[:, lane:lane + 1]).astype(BF16))
            lhs = jnp.concatenate(parts, axis=1)
            y_pairs.append(jnp.dot(lhs, rhs, preferred_element_type=F32))
        emit_group(g, xs_g, jnp.concatenate(y_pairs, axis=1))
        s_new = lax.dot_general(b_bf, x_dec, (((0,), (0,)), ((), ())), preferred_element_type=F32)
        st_ref[g] = h_old * q_e[2 * SSD_CHUNK:2 * SSD_CHUNK + 1] + s_new


def _ssd_fwd_kernel(xs_ref, b_ref, c_ref, dtda_ref, mcum_ref, expand_ref, y_ref, st_ref):
    @pl.when(pl.program_id(0) == 0)
    def _():
        st_ref[...] = jnp.zeros_like(st_ref)

    def emit(g, xs_g, y_g):
        y_ref[:, g * GROUP_WIDTH:(g + 1) * GROUP_WIDTH] = y_g

    _ssd_chunk(0, xs_ref, b_ref, c_ref, dtda_ref, mcum_ref, expand_ref, st_ref, emit)


def _ssd_bwd_kernel(xs_ref, b_ref, c_ref, dtda_ref, mcum_ref, expand_ref, yf_ref, z_ref,
                    dskip_ref, ngain_ref, o_ref, st_ref):
    @pl.when(pl.program_id(0) == 0)
    def _():
        st_ref[...] = jnp.zeros_like(st_ref)

    def emit(g, xs_g, y_g):
        gcols = slice(g * GROUP_WIDTH, (g + 1) * GROUP_WIDTH)
        y = y_g + yf_ref[:, gcols] + dskip_ref[:, gcols] * xs_g
        gated = y * jax.nn.silu(z_ref[:, gcols])
        ms = jnp.mean(gated * gated, axis=-1, keepdims=True)
        o_ref[:, gcols] = (gated * lax.rsqrt(ms + RMS_EPS) * ngain_ref[:, gcols]).astype(BF16)

    _ssd_chunk(1, xs_ref, b_ref, c_ref, dtda_ref, mcum_ref, expand_ref, st_ref, emit)


def _ssd_specs(chunk_of_step):
    xs_blk = SSM_WIDTH
    n_blk = SSM_GROUPS * SSM_STATE
    return [pl.BlockSpec((SSD_CHUNK, xs_blk), lambda i: (chunk_of_step(i), 0)),
            pl.BlockSpec((SSD_CHUNK, n_blk), lambda i: (chunk_of_step(i), SSM_WIDTH // n_blk)),
            pl.BlockSpec((SSD_CHUNK, n_blk), lambda i: (chunk_of_step(i), SSM_WIDTH // n_blk + 1)),
            pl.BlockSpec((SSD_CHUNK, 2 * V7X_LANES), lambda i: (chunk_of_step(i), 0)),
            pl.BlockSpec((SSD_CHUNK, SSD_CHUNK), lambda i: (0, 0)),
            pl.BlockSpec((V7X_LANES, SSM_WIDTH), lambda i: (0, 0))]


def _ssd_forward(conv, dtda, n_ctx_chunks, n_lat_chunks):
    n_steps = n_ctx_chunks + n_lat_chunks
    tri = (jnp.arange(SSD_CHUNK)[:, None] >= jnp.arange(SSD_CHUNK)[None, :]).astype(F32)
    expand = _head_expand(0)
    return pl.pallas_call(
        _ssd_fwd_kernel,
        out_shape=jax.ShapeDtypeStruct((n_lat_chunks * SSD_CHUNK, SSM_WIDTH), F32),
        grid=(n_steps,),
        in_specs=_ssd_specs(lambda i: i),
        out_specs=pl.BlockSpec((SSD_CHUNK, SSM_WIDTH), lambda i: (jnp.maximum(i - n_ctx_chunks, 0), 0)),
        scratch_shapes=[pltpu.VMEM((SSM_GROUPS, SSM_STATE, GROUP_WIDTH), F32)],
        compiler_params=_cparams(("arbitrary",), 40),
        name="ssd_forward",
    )(conv, conv, conv, dtda, tri, expand)


def _ssd_backward(conv, dtda, y_fwd, proj, d_skip, norm_gain, n_ctx_chunks, n_lat_chunks):
    n_steps = n_ctx_chunks + n_lat_chunks
    tri = (jnp.arange(SSD_CHUNK)[:, None] <= jnp.arange(SSD_CHUNK)[None, :]).astype(F32)
    expand = _head_expand(1)

    def chunk_of_step(i):
        return jnp.where(i < n_ctx_chunks, n_ctx_chunks - 1 - i, n_steps - 1 + n_ctx_chunks - i)

    def lat_of_step(i):
        return jnp.where(i < n_ctx_chunks, n_lat_chunks - 1, n_steps - 1 - i)

    return pl.pallas_call(
        _ssd_bwd_kernel,
        out_shape=jax.ShapeDtypeStruct((n_lat_chunks * SSD_CHUNK, SSM_WIDTH), BF16),
        grid=(n_steps,),
        in_specs=_ssd_specs(chunk_of_step) + [
            pl.BlockSpec((SSD_CHUNK, SSM_WIDTH), lambda i: (lat_of_step(i), 0)),
            pl.BlockSpec((SSD_CHUNK, SSM_WIDTH), lambda i: (chunk_of_step(i), COL_Z // SSM_WIDTH)),
            pl.BlockSpec((1, SSM_WIDTH), lambda i: (0, 0)),
            pl.BlockSpec((1, SSM_WIDTH), lambda i: (0, 0))],
        out_specs=pl.BlockSpec((SSD_CHUNK, SSM_WIDTH), lambda i: (lat_of_step(i), 0)),
        scratch_shapes=[pltpu.VMEM((SSM_GROUPS, SSM_STATE, GROUP_WIDTH), F32)],
        compiler_params=_cparams(("arbitrary",), 40),
        name="ssd_backward",
    )(conv, conv, conv, dtda, tri, expand, y_fwd, proj, d_skip, norm_gain)


def _head_expand(direction):
    lane = jnp.arange(V7X_LANES)[:, None]
    chan = jnp.arange(SSM_WIDTH)[None, :]
    return (lane == direction * SSM_HEADS + chan // SSM_HEAD_DIM).astype(BF16)


def _postmix_kernel(x_ref, mix_ref, p_ref, lng_ref, lnb_ref, wrt_ref, hf_ref, acc_ref, aff_ref, hf_scr):
    gate_m = p_ref[0, 2:3, :]
    shift_f = p_ref[0, 3:4, :]
    scale1_f = 1.0 + p_ref[0, 4:5, :]
    tm = x_ref.shape[0]

    def body(r, carry):
        sl = pl.ds(pl.multiple_of(r * ROW_STRIP, ROW_STRIP), ROW_STRIP)
        xn = _ln_rows(ALPHA * x_ref[sl, :] + gate_m * mix_ref[sl, :]) * lng_ref[...] + lnb_ref[...]
        acc_ref[sl, :] = ALPHA * xn
        hf = _ln_rows(xn) * scale1_f + shift_f
        hf_ref[sl, :] = hf
        hf_scr[sl, :] = hf.astype(BF16)
        return carry

    lax.fori_loop(0, tm // ROW_STRIP, body, 0)
    logits = lax.dot_general(wrt_ref[...], hf_scr[...], (((1,), (1,)), ((), ())),
                             preferred_element_type=F32)
    mx = jnp.max(logits, axis=0, keepdims=True)
    ex = jnp.exp(logits - mx)
    aff_ref[...] = ex / jnp.sum(ex, axis=0, keepdims=True)


def _post_mix(x2, mix, mods, ln_gain, ln_bias, wr_t, tm):
    l, d = x2.shape
    return pl.pallas_call(
        _postmix_kernel,
        out_shape=(jax.ShapeDtypeStruct((l, d), F32),
                   jax.ShapeDtypeStruct((l, d), F32),
                   jax.ShapeDtypeStruct((N_EXPERTS, l), F32)),
        grid=(l // tm,),
        in_specs=[pl.BlockSpec((tm, d), lambda i: (i, 0)),
                  pl.BlockSpec((tm, d), lambda i: (i, 0)),
                  pl.BlockSpec((1, V7X_SUBLANES, d), lambda i: (0, 0, 0)),
                  pl.BlockSpec((1, d), lambda i: (0, 0)),
                  pl.BlockSpec((1, d), lambda i: (0, 0)),
                  pl.BlockSpec((N_EXPERTS, d), lambda i: (0, 0))],
        out_specs=(pl.BlockSpec((tm, d), lambda i: (i, 0)),
                   pl.BlockSpec((tm, d), lambda i: (i, 0)),
                   pl.BlockSpec((N_EXPERTS, tm), lambda i: (0, i))),
        scratch_shapes=[pltpu.VMEM((tm, d), BF16)],
        compiler_params=_cparams(("arbitrary",), 48),
        name="post_mix",
    )(x2, mix, mods, ln_gain, ln_bias, wr_t)


def _topk_kernel(aff_ref, triu_ref, slow_ref, idx_ref, gate_ref, cum_scr, bend_v, bend_s, sem, *, cap):
    n_e, n_blk, n_lane = aff_ref.shape
    aff = aff_ref[...]
    def count(mask):
        c = jnp.sum(jnp.where(mask, 1.0, 0.0), axis=1, keepdims=True)
        return jnp.sum(c, axis=2, keepdims=True)

    def as_float(bits):
        return lax.bitcast_convert_type(bits, F32)

    def search(k, thr):
        cand = thr | jnp.left_shift(jnp.int32(1), 30 - k)
        return jnp.where(count(aff >= as_float(cand)) >= cap, cand, thr)

    thr = lax.fori_loop(0, 31, search, jnp.zeros((n_e, 1, 1), I32))
    above = aff >= as_float(thr + 1)
    equal = jnp.logical_and(aff >= as_float(thr), jnp.logical_not(above))
    need = cap - count(above)

    def prefix(mask):
        m = jnp.where(mask, 1.0, 0.0).astype(BF16)
        within = jnp.dot(m.reshape(n_e * n_blk, n_lane), triu_ref[...],
                         preferred_element_type=F32).reshape(n_e, n_blk, n_lane)
        tot = jnp.broadcast_to(within[:, :, n_lane - 1:n_lane], (n_e, n_blk, n_lane)).astype(BF16)
        offs = [jnp.dot(slow_ref[...], tot[e], preferred_element_type=F32) for e in range(n_e)]
        return within + jnp.stack(offs, axis=0)

    cum_above = prefix(above)
    cum_equal = prefix(equal)
    chosen = jnp.logical_or(above, jnp.logical_and(equal, cum_equal <= need))
    cum = cum_above + jnp.minimum(cum_equal, need)
    cum_scr[...] = jnp.where(chosen, cum, -1.0)

    last_lane = jnp.where(lax.broadcasted_iota(I32, (V7X_SUBLANES, n_lane), 1) == n_lane - 1, 1.0, 0.0)
    bend_v[...] = jnp.zeros_like(bend_v)
    for e in range(n_e):
        ends = lax.dot_general(last_lane, cum[e], (((1,), (1,)), ((), ())),
                               preferred_element_type=F32, precision=lax.Precision.HIGHEST)
        bend_v[e:e + 1, 0:n_blk] = ends[0:1, :].astype(I32)
    to_smem = pltpu.make_async_copy(bend_v, bend_s, sem)
    to_smem.start()
    to_smem.wait()

    jt = SLOT_TILE
    lane_f = lax.broadcasted_iota(I32, (jt, n_lane), 1).astype(F32)

    def per_expert(e, carry):
        def per_tile(t, b_start):
            first = t * jt
            b_lo = lax.while_loop(lambda b: jnp.logical_and(b < n_blk - 1, bend_s[e, b] <= first),
                                  lambda b: b + 1, b_start)
            b_hi = lax.while_loop(lambda b: jnp.logical_and(b < n_blk - 1, bend_s[e, b] < first + jt),
                                  lambda b: b + 1, b_lo)
            slot = (lax.broadcasted_iota(I32, (jt, n_lane), 0) + (first + 1)).astype(F32)

            def visit(b, accs):
                hit = cum_scr[e, pl.ds(b, 1), :] == slot
                return (accs[0] + jnp.where(hit, (b + 1).astype(F32), 0.0),
                        accs[1] + jnp.where(hit, aff_ref[e, pl.ds(b, 1), :], 0.0))

            zeros = jnp.zeros((jt, n_lane), F32)
            blk_acc, gate_acc = lax.fori_loop(b_lo, b_hi + 1, visit, (zeros, zeros))
            tok = jnp.where(blk_acc > 0.0, (blk_acc - 1.0) * n_lane + lane_f, 0.0)
            tok = jnp.sum(tok, axis=1, keepdims=True)
            gate = jnp.sum(gate_acc, axis=1, keepdims=True)
            rows = pl.ds(pl.multiple_of(t * jt, jt), jt)
            idx_ref[e, rows, :] = jnp.broadcast_to(tok, (jt, n_lane)).astype(I32)
            gate_ref[e, rows, :] = jnp.broadcast_to(gate, (jt, n_lane))
            return b_lo
        lax.fori_loop(0, cap // jt, per_tile, jnp.int32(0))
        return carry

    lax.fori_loop(0, n_e, per_expert, 0)


def _expert_choice_topk(aff_t, cap):
    n_e, l = aff_t.shape
    n_blk = l // V7X_LANES
    aff3 = aff_t.reshape(n_e, n_blk, V7X_LANES)
    triu = (jnp.arange(V7X_LANES)[:, None] <= jnp.arange(V7X_LANES)[None, :]).astype(BF16)
    slow = (jnp.arange(n_blk)[:, None] > jnp.arange(n_blk)[None, :]).astype(BF16)
    return pl.pallas_call(
        functools.partial(_topk_kernel, cap=cap),
        out_shape=(jax.ShapeDtypeStruct((n_e, cap, V7X_LANES), I32),
                   jax.ShapeDtypeStruct((n_e, cap, V7X_LANES), F32)),
        grid=(1,),
        in_specs=[pl.BlockSpec((n_e, n_blk, V7X_LANES), lambda i: (0, 0, 0)),
                  pl.BlockSpec((V7X_LANES, V7X_LANES), lambda i: (0, 0)),
                  pl.BlockSpec((n_blk, n_blk), lambda i: (0, 0))],
        out_specs=(pl.BlockSpec((n_e, cap, V7X_LANES), lambda i: (0, 0, 0)),
                   pl.BlockSpec((n_e, cap, V7X_LANES), lambda i: (0, 0, 0))),
        scratch_shapes=[pltpu.VMEM((n_e, n_blk, V7X_LANES), F32),
                        pltpu.VMEM((n_e, V7X_LANES), I32), pltpu.SMEM((n_e, V7X_LANES), I32),
                        pltpu.SemaphoreType.DMA(())],
        compiler_params=_cparams(("arbitrary",), 48),
        name="expert_choice_topk",
    )(aff3, triu, slow)


def _row_copy(src, dst, src_row, dst_row, sem):
    return pltpu.make_async_copy(src.at[pl.ds(src_row, 1), :], dst.at[pl.ds(dst_row, 1), :], sem)


def _wait_rows_in(hbm, buf, sem):
    pltpu.make_async_copy(hbm.at[pl.ds(0, buf.shape[0]), :], buf, sem).wait()


def _wait_rows_out(buf, hbm, sem):
    pltpu.make_async_copy(buf, hbm.at[pl.ds(0, buf.shape[0]), :], sem).wait()


def _ffn_kernel(idx_ref, hf, wg_ref, wu_ref, wd_ref, gate_ref, gf_ref, o_ref,
                xe_scr, h_scr, stage, sems, *, n_ff_tiles, n_experts, n_steps, cap):
    e = pl.program_id(0)
    s = pl.program_id(1)
    rows = cap // n_steps
    tf = wg_ref.shape[2]

    def issue(expert, chunk, slot):
        base = expert * cap + chunk * rows
        for j in range(rows):
            _row_copy(hf, stage.at[slot], idx_ref[base + j], j, sems.at[slot]).start()

    def land(buf, chunk, slot):
        _wait_rows_in(hf, stage.at[slot], sems.at[slot])
        first = chunk * rows
        r = pl.ds(first if isinstance(first, int) else pl.multiple_of(first, rows), rows)
        xe_scr[buf, r, :] = stage[slot].astype(BF16)

    @pl.when(jnp.logical_and(e == 0, s == 0))
    def _():
        issue(0, 0, 0)
        for k in range(n_steps - 1):
            issue(0, k + 1, (k + 1) % STAGE_SLOTS)
            land(0, k, k % STAGE_SLOTS)
        issue(1 % n_experts, 0, 0)

    def step_io():
        due = (s + n_steps - 1) % n_steps
        owner = e + jnp.where(s > 0, 1, 0)
        land(owner % 2, due, due % STAGE_SLOTS)
        ahead = (s + 1) % n_steps
        target = e + 1 + jnp.where(s == n_steps - 1, 1, 0)
        issue(target % n_experts, ahead, ahead % STAGE_SLOTS)

    @pl.when(s < n_ff_tiles)
    def _():
        step_io()
        xe = xe_scr[e % 2]
        a = jnp.dot(xe, wg_ref[0].astype(BF16), preferred_element_type=F32)
        u = jnp.dot(xe, wu_ref[0].astype(BF16), preferred_element_type=F32)
        h_scr[:, pl.ds(pl.multiple_of(s * tf, tf), tf)] = (jax.nn.silu(a) * u).astype(BF16)

    @pl.when(s >= n_ff_tiles)
    def _():
        step_io()
        y = jnp.dot(h_scr[...], wd_ref[0].astype(BF16), preferred_element_type=F32)
        o_ref[0] = (y * gate_ref[0, :, 0:1] * gf_ref[...]).astype(BF16)

    @pl.when(jnp.logical_and(e == n_experts - 1, s == n_steps - 1))
    def _():
        for chunk in (n_steps - 1, 0):
            _wait_rows_in(hf, stage.at[chunk % STAGE_SLOTS], sems.at[chunk % STAGE_SLOTS])


def _expert_ffn(idx_flat, hf_rows, w_gate, w_up, w_down, gates, gate_f, cap, tf, tn):
    n_e, d, ff = w_gate.shape
    n_ff = ff // tf
    n_out = d // tn
    n_steps = n_ff + n_out
    assert cap % n_steps == 0 and ff % tf == 0 and d % tn == 0 and n_steps % STAGE_SLOTS == 0
    rows = cap // n_steps
    col_a = lambda e, s, idx: (e, 0, jnp.minimum(s, n_ff - 1))
    col_b = lambda e, s, idx: (e, 0, jnp.maximum(s - n_ff, 0))
    return pl.pallas_call(
        functools.partial(_ffn_kernel, n_ff_tiles=n_ff, n_experts=n_e, n_steps=n_steps, cap=cap),
        out_shape=jax.ShapeDtypeStruct((n_e, cap, d), BF16),
        grid_spec=pltpu.PrefetchScalarGridSpec(
            num_scalar_prefetch=1, grid=(n_e, n_steps),
            in_specs=[pl.BlockSpec(memory_space=pl.ANY),
                      pl.BlockSpec((1, d, tf), col_a),
                      pl.BlockSpec((1, d, tf), col_a),
                      pl.BlockSpec((1, ff, tn), col_b),
                      pl.BlockSpec((1, cap, V7X_LANES), lambda e, s, idx: (e, 0, 0), pipeline_mode=pl.Buffered(1)),
                      pl.BlockSpec((1, tn), lambda e, s, idx: (0, jnp.maximum(s - n_ff, 0)))],
            out_specs=pl.BlockSpec((1, cap, tn), col_b),
            scratch_shapes=[pltpu.VMEM((2, cap, d), BF16), pltpu.VMEM((cap, ff), BF16),
                            pltpu.VMEM((STAGE_SLOTS, rows, d), F32),
                            pltpu.SemaphoreType.DMA((STAGE_SLOTS,))]),
        compiler_params=_cparams(("arbitrary", "arbitrary"), 60),
        name="expert_ffn",
    )(idx_flat, hf_rows, w_gate, w_up, w_down, gates, gate_f)


def _combine_kernel(idx_ref, ye_ref, acc_in, acc_out, buf, gsem, ssem, *, cap, n_tiles):
    e = pl.program_id(0)
    m = pl.program_id(1)
    n_buf, n_grp, grp, _ = buf.shape
    tm = n_grp * grp

    def for_rows(tile, fn):
        def body(g, carry):
            first = e * cap + tile * tm + g * grp
            for u in range(grp):
                fn(g, u, idx_ref[first + u])
            return carry
        lax.fori_loop(0, n_grp, body, 0)

    def fetch(tile, b):
        for_rows(tile, lambda g, u, tok: _row_copy(acc_in, buf.at[b, g], tok, u, gsem.at[b]).start())

    def put(tile, b):
        for_rows(tile, lambda g, u, tok: _row_copy(buf.at[b, g], acc_out, u, tok, ssem.at[b]).start())

    def fetch_wait(b):
        for g in range(n_grp):
            _wait_rows_in(acc_in, buf.at[b, g], gsem.at[b])

    def put_wait(b):
        for g in range(n_grp):
            _wait_rows_out(buf.at[b, g], acc_out, ssem.at[b])

    for t in range(n_tiles):
        @pl.when(m == t)
        def _(t=t):
            b = t % n_buf
            if t == 0:
                fetch(0, 0)
                if n_tiles > 1:
                    fetch(1, 1 % n_buf)
            elif t + 1 < n_tiles:
                nb = (t + 1) % n_buf
                if t + 1 >= n_buf:
                    put_wait(nb)
                fetch(t + 1, nb)
            fetch_wait(b)
            for g in range(n_grp):
                buf[b, g] = buf[b, g] + ye_ref[0, g * grp:(g + 1) * grp, :].astype(F32)
            put(t, b)
            if t == n_tiles - 1:
                for u in range(max(0, n_tiles - n_buf), n_tiles):
                    put_wait(u % n_buf)


def _combine(idx_flat, ye, acc, tm, n_buf):
    n_e, cap, d = ye.shape
    n_tiles = cap // tm
    return pl.pallas_call(
        functools.partial(_combine_kernel, cap=cap, n_tiles=n_tiles),
        out_shape=jax.ShapeDtypeStruct(acc.shape, acc.dtype),
        grid_spec=pltpu.PrefetchScalarGridSpec(
            num_scalar_prefetch=1, grid=(n_e, n_tiles),
            in_specs=[pl.BlockSpec((1, tm, d), lambda e, m, idx: (e, m, 0)),
                      pl.BlockSpec(memory_space=pl.ANY)],
            out_specs=pl.BlockSpec(memory_space=pl.ANY),
            scratch_shapes=[pltpu.VMEM((n_buf, tm // DMA_GROUP, DMA_GROUP, d), F32),
                            pltpu.SemaphoreType.DMA((n_buf,)), pltpu.SemaphoreType.DMA((n_buf,))]),
        input_output_aliases={2: 0},
        compiler_params=_cparams(("arbitrary", "arbitrary"), 32),
        name="combine",
    )(idx_flat, ye, acc)


def _final_ln_kernel(a_ref, g_ref, b_ref, o_ref):
    def body(r, carry):
        sl = pl.ds(pl.multiple_of(r * ROW_STRIP, ROW_STRIP), ROW_STRIP)
        o_ref[sl, :] = _ln_rows(a_ref[sl, :]) * g_ref[...] + b_ref[...]
        return carry
    lax.fori_loop(0, a_ref.shape[0] // ROW_STRIP, body, 0)


def _final_ln(acc, gain, bias, tm):
    l, d = acc.shape
    return pl.pallas_call(
        _final_ln_kernel,
        out_shape=jax.ShapeDtypeStruct((l, d), F32),
        grid=(l // tm,),
        in_specs=[pl.BlockSpec((tm, d), lambda i: (i, 0)),
                  pl.BlockSpec((1, d), lambda i: (0, 0)),
                  pl.BlockSpec((1, d), lambda i: (0, 0))],
        out_specs=pl.BlockSpec((tm, d), lambda i: (i, 0)),
        compiler_params=_cparams(("arbitrary",), 32),
        name="final_ln",
    )(acc, gain, bias)


def _pad_rows(a, rows):
    return jnp.pad(a, ((0, rows - a.shape[0]), (0, 0)))


def _pad_lanes(a, lanes):
    return jnp.pad(a, ((0, 0), (0, lanes - a.shape[1])))


def kernel(x, c, ctx, c_ctx, w_ada, b_ada, w_in, gm_v_gain, gm_v_bias, gm_w_s, gm_b_s, ssm_conv_w, ssm_conv_b, ssm_dt_bias, ssm_a_log, ssm_d, ssm_norm_gain, w_out, ln_mix_gain, ln_mix_bias, w_router, w_gate, w_up, w_down, ln_ffn_gain, ln_ffn_bias):
    b, l, d = x.shape
    lc = ctx.shape[1]
    assert b == 1 and d == D_MODEL and w_ada.shape[0] == DEPTH
    x2 = x[0]
    ctx2 = ctx[0]
    layer = 0

    cond = _pad_rows(jnp.concatenate([c, c_ctx[None, :]], axis=0), V7X_SUBLANES)
    ada = _ada_params(cond, w_ada[layer], b_ada[layer][None, :])
    mods = jnp.pad(ada[:2].reshape(2, 6, d), ((0, 0), (0, V7X_SUBLANES - 6), (0, 0)))

    w_in_t = jnp.swapaxes(w_in, 1, 2)
    wdt = _pad_rows(w_in_t[layer, COL_DT:, :], V7X_LANES).astype(BF16)
    dt_bias = _pad_lanes(ssm_dt_bias[layer].reshape(1, 2 * SSM_HEADS), V7X_LANES)
    a_log = _pad_lanes(ssm_a_log[layer].reshape(1, 2 * SSM_HEADS), V7X_LANES)
    h, dtda = _modulate_in(x2, ctx2, mods, wdt, dt_bias, a_log)
    proj = _matmul_nt(h, w_in_t, COL_DT, tm=1408, tn=512, name="in_proj")

    conv_w8 = _pad_rows(ssm_conv_w[layer], V7X_SUBLANES)
    conv = _conv_silu(proj, conv_w8, ssm_conv_b[layer][None, :], rb=lc, tc=1024)

    n_lat_chunks = l // CHUNK
    n_ctx_chunks = lc // SSD_CHUNK
    gm = _chunk_mlp(proj, lc, n_lat_chunks, gm_v_gain[layer][None, :], gm_v_bias[layer][None, :],
                    gm_w_s[layer].astype(BF16), gm_b_s[layer].T)

    y_fwd = _ssd_forward(conv, dtda, n_ctx_chunks, n_lat_chunks)
    d_skip = jnp.repeat(ssm_d[layer], SSM_HEAD_DIM)[None, :]
    ss = _ssd_backward(conv, dtda, y_fwd, proj, d_skip, ssm_norm_gain[layer][None, :],
                       n_ctx_chunks, n_lat_chunks)

    mix = _matmul_concat(gm, ss, w_out[layer], tm=1024, tn=512, name="out_proj")

    hf_rows, acc, aff_t = _post_mix(x2, mix, mods, ln_mix_gain[layer][None, :], ln_mix_bias[layer][None, :],
                                      w_router[layer].T.astype(BF16), tm=256)

    cap = EC_FACTOR * l // N_EXPERTS
    idx3, gates = _expert_choice_topk(aff_t, cap)
    idx_flat = idx3[:, :, 0].reshape(N_EXPERTS * cap)

    gate_f = mods[0, 5:6, :]
    ye = _expert_ffn(idx_flat, hf_rows, w_gate[layer], w_up[layer], w_down[layer], gates, gate_f,
                     cap, tf=256, tn=512)
    acc = _combine(idx_flat, ye, acc, tm=256, n_buf=3)
    out = _final_ln(acc, ln_ffn_gain[layer][None, :], ln_ffn_bias[layer][None, :], tm=256)
    return out[None]
```

```python
import functools

import jax
import jax.numpy as jnp
from jax import lax
from jax.experimental import pallas as pl
from jax.experimental.pallas import tpu as pltpu

F32 = jnp.float32
BF16 = jnp.bfloat16
I32 = jnp.int32

D_MODEL = 4096
GRID_W = 64
CHUNK = 128
GM_WIDTH = 2048
GM_HEADS = 8
GM_HEAD_DIM = GM_WIDTH // GM_HEADS
SSM_WIDTH = 2048
SSM_HEAD_DIM = 64
SSM_HEADS = SSM_WIDTH // SSM_HEAD_DIM
SSM_GROUPS = 8
SSM_STATE = 128
SSM_CONV = 5
SSM_XBC = SSM_WIDTH + 2 * SSM_GROUPS * SSM_STATE
SSD_CHUNK = 128
HEADS_PER_GROUP = SSM_HEADS // SSM_GROUPS
GROUP_WIDTH = SSM_WIDTH // SSM_GROUPS
N_EXPERTS = 16
EXPERT_FF = 2048
EC_FACTOR = 2
DEPTH = 1
ALPHA = (2 * DEPTH) ** 0.25
LN_EPS = 1e-5
RMS_EPS = 1e-5

COL_Z = 2 * GM_WIDTH
COL_XBC = COL_Z + SSM_WIDTH
COL_DT = COL_XBC + SSM_XBC

V7X_LANES = 128
V7X_SUBLANES = 8
V7X_VMEM_BYTES = 64 * 1024 * 1024
MIB = 1024 * 1024

ROW_STRIP = 128
SLOT_TILE = 128
DMA_GROUP = 32
STAGE_SLOTS = 4
GMLP_CHUNKS_PER_STEP = 2


def _cparams(semantics, vmem_mib):
    return pltpu.CompilerParams(dimension_semantics=semantics,
                                vmem_limit_bytes=min(vmem_mib * MIB, V7X_VMEM_BYTES - 2 * MIB))


def _ln_rows(x):
    mu = jnp.mean(x, axis=-1, keepdims=True)
    xc = x - mu
    var = jnp.mean(xc * xc, axis=-1, keepdims=True)
    return xc * lax.rsqrt(var + LN_EPS)


def _ada_kernel(c_ref, w_ref, b_ref, o_ref):
    s = jax.nn.silu(c_ref[...]).astype(BF16)
    o_ref[...] = jnp.dot(s, w_ref[...].astype(BF16), preferred_element_type=F32) + b_ref[...]


def _ada_params(cond, w_ada, b_ada):
    k, n = w_ada.shape
    tn = 512
    return pl.pallas_call(
        _ada_kernel,
        out_shape=jax.ShapeDtypeStruct((V7X_SUBLANES, n), F32),
        grid=(n // tn,),
        in_specs=[pl.BlockSpec((V7X_SUBLANES, k), lambda j: (0, 0)),
                  pl.BlockSpec((k, tn), lambda j: (0, j)),
                  pl.BlockSpec((1, tn), lambda j: (0, j))],
        out_specs=pl.BlockSpec((V7X_SUBLANES, tn), lambda j: (0, j)),
        compiler_params=_cparams(("arbitrary",), 32),
        name="ada_params",
    )(cond, w_ada, b_ada)


def _modin_kernel(x_ref, ctx_ref, p_ref, wdt_ref, bias_ref, alog_ref, h_ref, dtda_ref):
    i = pl.program_id(0)
    shift = p_ref[0, 0:1, :]
    scale1 = 1.0 + p_ref[0, 1:2, :]
    tm = h_ref.shape[0]

    def rows(src_ref):
        def body(r, carry):
            sl = pl.ds(pl.multiple_of(r * ROW_STRIP, ROW_STRIP), ROW_STRIP)
            h_ref[sl, :] = (_ln_rows(src_ref[sl, :]) * scale1 + shift).astype(BF16)
            return carry
        lax.fori_loop(0, tm // ROW_STRIP, body, 0)

    @pl.when(i == 0)
    def _():
        rows(ctx_ref)

    @pl.when(i > 0)
    def _():
        rows(x_ref)

    raw = lax.dot_general(h_ref[...], wdt_ref[...], (((1,), (1,)), ((), ())),
                          preferred_element_type=F32) + bias_ref[...]
    dt = jnp.maximum(raw, 0.0) + jnp.log1p(jnp.exp(-jnp.abs(raw)))
    dtda_ref[:, 0:V7X_LANES] = dt
    dtda_ref[:, V7X_LANES:2 * V7X_LANES] = dt * (-jnp.exp(alog_ref[...]))


def _modulate_in(x2, ctx2, mods, wdt, dt_bias, a_log):
    l, d = x2.shape
    lc = ctx2.shape[0]
    tm = lc
    n_tiles = (l + lc) // tm
    return pl.pallas_call(
        _modin_kernel,
        out_shape=(jax.ShapeDtypeStruct((l + lc, d), BF16),
                   jax.ShapeDtypeStruct((l + lc, 2 * V7X_LANES), F32)),
        grid=(n_tiles,),
        in_specs=[pl.BlockSpec((tm, d), lambda i: (jnp.maximum(i - 1, 0), 0)),
                  pl.BlockSpec((tm, d), lambda i: (0, 0)),
                  pl.BlockSpec((1, V7X_SUBLANES, d), lambda i: (jnp.where(i == 0, 1, 0), 0, 0)),
                  pl.BlockSpec((V7X_LANES, d), lambda i: (0, 0)),
                  pl.BlockSpec((1, V7X_LANES), lambda i: (0, 0)),
                  pl.BlockSpec((1, V7X_LANES), lambda i: (0, 0))],
        out_specs=(pl.BlockSpec((tm, d), lambda i: (i, 0)),
                   pl.BlockSpec((tm, 2 * V7X_LANES), lambda i: (i, 0))),
        compiler_params=_cparams(("arbitrary",), 40),
        name="modulate_in",
    )(x2, ctx2, mods, wdt, dt_bias, a_log)


def _mm_nt_kernel(a_ref, wt_ref, o_ref, wbf_ref):
    @pl.when(pl.program_id(1) == 0)
    def _():
        wbf_ref[...] = wt_ref[...].astype(BF16)

    o_ref[...] = lax.dot_general(a_ref[...], wbf_ref[...], (((1,), (1,)), ((), ())),
                                 preferred_element_type=F32)


def _matmul_nt(a, wt3, n_out, tm, tn, name):
    m, k = a.shape
    return pl.pallas_call(
        _mm_nt_kernel,
        out_shape=jax.ShapeDtypeStruct((m, n_out), F32),
        grid=(n_out // tn, m // tm),
        in_specs=[pl.BlockSpec((tm, k), lambda j, i: (i, 0)),
                  pl.BlockSpec((None, tn, k), lambda j, i: (0, j, 0))],
        out_specs=pl.BlockSpec((tm, tn), lambda j, i: (i, j)),
        scratch_shapes=[pltpu.VMEM((tn, k), BF16)],
        compiler_params=_cparams(("arbitrary", "arbitrary"), 56),
        name=name,
    )(a, wt3)


def _mm2_kernel(a0_ref, a1_ref, w_ref, o_ref, wbf_ref):
    @pl.when(pl.program_id(1) == 0)
    def _():
        wbf_ref[...] = w_ref[...].astype(BF16)

    k0 = a0_ref.shape[1]
    acc = jnp.dot(a0_ref[...], wbf_ref[0:k0, :], preferred_element_type=F32)
    o_ref[...] = acc + jnp.dot(a1_ref[...], wbf_ref[k0:, :], preferred_element_type=F32)


def _matmul_concat(a0, a1, w, tm, tn, name):
    m, k0 = a0.shape
    k1 = a1.shape[1]
    n_out = w.shape[1]
    return pl.pallas_call(
        _mm2_kernel,
        out_shape=jax.ShapeDtypeStruct((m, n_out), F32),
        grid=(n_out // tn, m // tm),
        in_specs=[pl.BlockSpec((tm, k0), lambda j, i: (i, 0)),
                  pl.BlockSpec((tm, k1), lambda j, i: (i, 0)),
                  pl.BlockSpec((k0 + k1, tn), lambda j, i: (0, j))],
        out_specs=pl.BlockSpec((tm, tn), lambda j, i: (i, j)),
        scratch_shapes=[pltpu.VMEM((k0 + k1, tn), BF16)],
        compiler_params=_cparams(("arbitrary", "arbitrary"), 56),
        name=name,
    )(a0, a1, w)


def _conv_kernel(main_ref, prev_ref, next_ref, w_ref, b_ref, o_ref, ext_ref):
    i = pl.program_id(0)
    rb = main_ref.shape[0]
    halo = V7X_SUBLANES
    pad = (SSM_CONV - 1) // 2
    seq_start = jnp.logical_or(i == 0, i == 1)
    seq_end = jnp.logical_or(i == 0, i == pl.num_programs(0) - 1)
    ext_ref[0:halo, :] = jnp.where(seq_start, 0.0, prev_ref[...])
    ext_ref[halo:halo + rb, :] = main_ref[...]
    ext_ref[halo + rb:2 * halo + rb, :] = jnp.where(seq_end, 0.0, next_ref[...])
    ext = ext_ref[...]
    n_ext = ext.shape[0]
    acc = jnp.broadcast_to(b_ref[...], o_ref.shape)
    for k in range(SSM_CONV):
        shifted = ext if k == pad else pltpu.roll(ext, (pad - k) % n_ext, axis=0)
        acc = acc + w_ref[k:k + 1, :] * shifted[halo:halo + rb, :]
    o_ref[...] = acc * jax.nn.sigmoid(acc)


def _conv_silu(proj, conv_w8, conv_b, rb, tc):
    rows = proj.shape[0]
    n_row_tiles = rows // rb
    halo_per_tile = rb // V7X_SUBLANES
    n_halo_blocks = rows // V7X_SUBLANES
    col0 = COL_XBC // tc
    return pl.pallas_call(
        _conv_kernel,
        out_shape=jax.ShapeDtypeStruct((rows, SSM_XBC), F32),
        grid=(n_row_tiles, SSM_XBC // tc),
        in_specs=[pl.BlockSpec((rb, tc), lambda i, j: (i, col0 + j)),
                  pl.BlockSpec((V7X_SUBLANES, tc),
                               lambda i, j: (jnp.maximum(i * halo_per_tile - 1, 0), col0 + j)),
                  pl.BlockSpec((V7X_SUBLANES, tc),
                               lambda i, j: (jnp.minimum((i + 1) * halo_per_tile, n_halo_blocks - 1), col0 + j)),
                  pl.BlockSpec((V7X_SUBLANES, tc), lambda i, j: (0, j)),
                  pl.BlockSpec((1, tc), lambda i, j: (0, j))],
        out_specs=pl.BlockSpec((rb, tc), lambda i, j: (i, j)),
        scratch_shapes=[pltpu.VMEM((rb + 2 * V7X_SUBLANES, tc), F32)],
        compiler_params=_cparams(("arbitrary", "arbitrary"), 32),
        name="conv_silu",
    )(proj, proj, proj, conv_w8, conv_b)


def _gmlp_kernel(uv_ref, gain_ref, bias_ref, ws_ref, bst_ref, o_ref):
    for ci in range(uv_ref.shape[0] // CHUNK):
        rows = slice(ci * CHUNK, (ci + 1) * CHUNK)
        g = jax.nn.gelu(uv_ref[rows, :])
        u = g[:, 0:GM_WIDTH]
        v = _ln_rows(g[:, GM_WIDTH:2 * GM_WIDTH]) * gain_ref[...] + bias_ref[...]
        vb = v.astype(BF16)
        for h in range(GM_HEADS):
            cols = slice(h * GM_HEAD_DIM, (h + 1) * GM_HEAD_DIM)
            mixed = jnp.dot(ws_ref[h], vb[:, cols], preferred_element_type=F32) + bst_ref[:, h:h + 1]
            o_ref[rows, cols] = (u[:, cols] * mixed).astype(BF16)


def _chunk_mlp(proj, lat_row0, n_chunks, v_gain, v_bias, ws_bf, bs_t):
    rows_per_step = GMLP_CHUNKS_PER_STEP * CHUNK
    assert lat_row0 % rows_per_step == 0 and n_chunks % GMLP_CHUNKS_PER_STEP == 0
    blk0 = lat_row0 // rows_per_step
    return pl.pallas_call(
        _gmlp_kernel,
        out_shape=jax.ShapeDtypeStruct((n_chunks * CHUNK, GM_WIDTH), BF16),
        grid=(n_chunks // GMLP_CHUNKS_PER_STEP,),
        in_specs=[pl.BlockSpec((rows_per_step, 2 * GM_WIDTH), lambda c: (blk0 + c, 0)),
                  pl.BlockSpec((1, GM_WIDTH), lambda c: (0, 0)),
                  pl.BlockSpec((1, GM_WIDTH), lambda c: (0, 0)),
                  pl.BlockSpec((GM_HEADS, CHUNK, CHUNK), lambda c: (0, 0, 0)),
                  pl.BlockSpec((CHUNK, GM_HEADS), lambda c: (0, 0))],
        out_specs=pl.BlockSpec((rows_per_step, GM_WIDTH), lambda c: (c, 0)),
        compiler_params=_cparams(("arbitrary",), 32),
        name="chunk_mlp",
    )(proj, v_gain, v_bias, ws_bf, bs_t)


def _ssd_chunk(direction, xs_ref, b_ref, c_ref, dtda_ref, mcum_ref, expand_ref, st_ref, emit_group):
    d = direction
    lane0 = d * SSM_HEADS
    dt = dtda_ref[:, 0:V7X_LANES]
    d_a = dtda_ref[:, V7X_LANES:2 * V7X_LANES]
    cs = jnp.dot(mcum_ref[...], d_a, preferred_element_type=F32, precision=lax.Precision.HIGHEST)
    cs_t = cs.T
    last = SSD_CHUNK - 1 if d == 0 else 0
    cs_end = cs[last:last + 1, :]
    dt_decay = dt * jnp.exp(cs_end - cs)
    exp_cs = jnp.exp(cs)
    chunk_decay = jnp.broadcast_to(jnp.exp(cs_end), (V7X_SUBLANES, V7X_LANES))
    q = jnp.concatenate([dt, dt_decay, chunk_decay], axis=0)
    q_hi = q.astype(BF16)
    q_lo = (q - q_hi.astype(F32)).astype(BF16)

    row_i = lax.broadcasted_iota(I32, (SSD_CHUNK, SSD_CHUNK), 0)
    col_i = lax.broadcasted_iota(I32, (SSD_CHUNK, SSD_CHUNK), 1)
    causal = (row_i >= col_i) if d == 0 else (row_i <= col_i)
    first_head = col_i < SSM_HEAD_DIM

    for g in range(SSM_GROUPS):
        gcols = slice(g * GROUP_WIDTH, (g + 1) * GROUP_WIDTH)
        ncols = slice(g * SSM_STATE, (g + 1) * SSM_STATE)
        c32 = c_ref[:, ncols]
        b_bf = b_ref[:, ncols].astype(BF16)
        c_bf = c32.astype(BF16)
        cb = lax.dot_general(c_bf, b_bf, (((1,), (1,)), ((), ())), preferred_element_type=F32)
        e_g = expand_ref[:, gcols]
        q_e = (jnp.dot(q_hi, e_g, preferred_element_type=F32)
               + jnp.dot(q_lo, e_g, preferred_element_type=F32))
        xs_g = xs_ref[:, gcols]
        x_dt = (xs_g * q_e[0:SSD_CHUNK]).astype(BF16)
        x_dec = (xs_g * q_e[SSD_CHUNK:2 * SSD_CHUNK]).astype(BF16)
        h_old = st_ref[g]
        h_bf = h_old.astype(BF16)
        y_pairs = []
        for pr in range(HEADS_PER_GROUP // 2):
            pcols = slice(pr * V7X_LANES, (pr + 1) * V7X_LANES)
            xp = x_dt[:, pcols]
            hp = h_bf[:, pcols]
            zero = jnp.zeros_like(xp)
            rhs = jnp.concatenate([jnp.where(first_head, xp, zero), jnp.where(first_head, hp, zero),
                                   jnp.where(first_head, zero, xp), jnp.where(first_head, zero, hp)], axis=0)
            parts = []
            for k in range(2):
                lane = lane0 + g * HEADS_PER_GROUP + 2 * pr + k
                seg = jnp.exp(jnp.where(causal, cs[:, lane:lane + 1] - cs_t[lane:lane + 1, :], -jnp.inf))
                parts.append((cb * seg).astype(BF16))
                parts.append((c32 * exp_cs[:, lane:lane + 1]).astype(BF16))
            lhs = jnp.concatenate(parts, axis=1)
            y_pairs.append(jnp.dot(lhs, rhs, preferred_element_type=F32))
        emit_group(g, xs_g, jnp.concatenate(y_pairs, axis=1))
        s_new = lax.dot_general(b_bf, x_dec, (((0,), (0,)), ((), ())), preferred_element_type=F32)
        st_ref[g] = h_old * q_e[2 * SSD_CHUNK:2 * SSD_CHUNK + 1] + s_new


def _ssd_fwd_kernel(xs_ref, b_ref, c_ref, dtda_ref, mcum_ref, expand_ref, y_ref, st_ref):
    @pl.when(pl.program_id(0) == 0)
    def _():
        st_ref[...] = jnp.zeros_like(st_ref)

    def emit(g, xs_g, y_g):
        y_ref[:, g * GROUP_WIDTH:(g + 1) * GROUP_WIDTH] = y_g

    _ssd_chunk(0, xs_ref, b_ref, c_ref, dtda_ref, mcum_ref, expand_ref, st_ref, emit)


def _ssd_bwd_kernel(xs_ref, b_ref, c_ref, dtda_ref, mcum_ref, expand_ref, yf_ref, z_ref,
                    dskip_ref, ngain_ref, o_ref, st_ref):
    @pl.when(pl.program_id(0) == 0)
    def _():
        st_ref[...] = jnp.zeros_like(st_ref)

    def emit(g, xs_g, y_g):
        gcols = slice(g * GROUP_WIDTH, (g + 1) * GROUP_WIDTH)
        y = y_g + yf_ref[:, gcols] + dskip_ref[:, gcols] * xs_g
        gated = y * jax.nn.silu(z_ref[:, gcols])
        ms = jnp.mean(gated * gated, axis=-1, keepdims=True)
        o_ref[:, gcols] = (gated * lax.rsqrt(ms + RMS_EPS) * ngain_ref[:, gcols]).astype(BF16)

    _ssd_chunk(1, xs_ref, b_ref, c_ref, dtda_ref, mcum_ref, expand_ref, st_ref, emit)


def _ssd_specs(chunk_of_step):
    xs_blk = SSM_WIDTH
    n_blk = SSM_GROUPS * SSM_STATE
    return [pl.BlockSpec((SSD_CHUNK, xs_blk), lambda i: (chunk_of_step(i), 0)),
            pl.BlockSpec((SSD_CHUNK, n_blk), lambda i: (chunk_of_step(i), SSM_WIDTH // n_blk)),
            pl.BlockSpec((SSD_CHUNK, n_blk), lambda i: (chunk_of_step(i), SSM_WIDTH // n_blk + 1)),
            pl.BlockSpec((SSD_CHUNK, 2 * V7X_LANES), lambda i: (chunk_of_step(i), 0)),
            pl.BlockSpec((SSD_CHUNK, SSD_CHUNK), lambda i: (0, 0)),
            pl.BlockSpec((V7X_LANES, SSM_WIDTH), lambda i: (0, 0))]


def _ssd_forward(conv, dtda, n_ctx_chunks, n_lat_chunks):
    n_steps = n_ctx_chunks + n_lat_chunks
    tri = (jnp.arange(SSD_CHUNK)[:, None] >= jnp.arange(SSD_CHUNK)[None, :]).astype(F32)
    expand = _head_expand(0)
    return pl.pallas_call(
        _ssd_fwd_kernel,
        out_shape=jax.ShapeDtypeStruct((n_lat_chunks * SSD_CHUNK, SSM_WIDTH), F32),
        grid=(n_steps,),
        in_specs=_ssd_specs(lambda i: i),
        out_specs=pl.BlockSpec((SSD_CHUNK, SSM_WIDTH), lambda i: (jnp.maximum(i - n_ctx_chunks, 0), 0)),
        scratch_shapes=[pltpu.VMEM((SSM_GROUPS, SSM_STATE, GROUP_WIDTH), F32)],
        compiler_params=_cparams(("arbitrary",), 40),
        name="ssd_forward",
    )(conv, conv, conv, dtda, tri, expand)


def _ssd_backward(conv, dtda, y_fwd, proj, d_skip, norm_gain, n_ctx_chunks, n_lat_chunks):
    n_steps = n_ctx_chunks + n_lat_chunks
    tri = (jnp.arange(SSD_CHUNK)[:, None] <= jnp.arange(SSD_CHUNK)[None, :]).astype(F32)
    expand = _head_expand(1)

    def chunk_of_step(i):
        return jnp.where(i < n_ctx_chunks, n_ctx_chunks - 1 - i, n_steps - 1 + n_ctx_chunks - i)

    def lat_of_step(i):
        return jnp.where(i < n_ctx_chunks, n_lat_chunks - 1, n_steps - 1 - i)

    return pl.pallas_call(
        _ssd_bwd_kernel,
        out_shape=jax.ShapeDtypeStruct((n_lat_chunks * SSD_CHUNK, SSM_WIDTH), BF16),
        grid=(n_steps,),
        in_specs=_ssd_specs(chunk_of_step) + [
            pl.BlockSpec((SSD_CHUNK, SSM_WIDTH), lambda i: (lat_of_step(i), 0)),
            pl.BlockSpec((SSD_CHUNK, SSM_WIDTH), lambda i: (chunk_of_step(i), COL_Z // SSM_WIDTH)),
            pl.BlockSpec((1, SSM_WIDTH), lambda i: (0, 0)),
            pl.BlockSpec((1, SSM_WIDTH), lambda i: (0, 0))],
        out_specs=pl.BlockSpec((SSD_CHUNK, SSM_WIDTH), lambda i: (lat_of_step(i), 0)),
        scratch_shapes=[pltpu.VMEM((SSM_GROUPS, SSM_STATE, GROUP_WIDTH), F32)],
        compiler_params=_cparams(("arbitrary",), 40),
        name="ssd_backward",
    )(conv, conv, conv, dtda, tri, expand, y_fwd, proj, d_skip, norm_gain)


def _head_expand(direction):
    lane = jnp.arange(V7X_LANES)[:, None]
    chan = jnp.arange(SSM_WIDTH)[None, :]
    return (lane == direction * SSM_HEADS + chan // SSM_HEAD_DIM).astype(BF16)


def _postmix_kernel(x_ref, mix_ref, p_ref, lng_ref, lnb_ref, wrt_ref, hf_ref, acc_ref, aff_ref, hf_scr):
    gate_m = p_ref[0, 2:3, :]
    shift_f = p_ref[0, 3:4, :]
    scale1_f = 1.0 + p_ref[0, 4:5, :]
    tm = x_ref.shape[0]

    def body(r, carry):
        sl = pl.ds(pl.multiple_of(r * ROW_STRIP, ROW_STRIP), ROW_STRIP)
        xn = _ln_rows(ALPHA * x_ref[sl, :] + gate_m * mix_ref[sl, :]) * lng_ref[...] + lnb_ref[...]
        acc_ref[sl, :] = ALPHA * xn
        hf = _ln_rows(xn) * scale1_f + shift_f
        hf_ref[sl, :] = hf
        hf_scr[sl, :] = hf.astype(BF16)
        return carry

    lax.fori_loop(0, tm // ROW_STRIP, body, 0)
    logits = lax.dot_general(wrt_ref[...], hf_scr[...], (((1,), (1,)), ((), ())),
                             preferred_element_type=F32)
    mx = jnp.max(logits, axis=0, keepdims=True)
    ex = jnp.exp(logits - mx)
    aff_ref[...] = ex / jnp.sum(ex, axis=0, keepdims=True)


def _post_mix(x2, mix, mods, ln_gain, ln_bias, wr_t, tm):
    l, d = x2.shape
    return pl.pallas_call(
        _postmix_kernel,
        out_shape=(jax.ShapeDtypeStruct((l, d), F32),
                   jax.ShapeDtypeStruct((l, d), F32),
                   jax.ShapeDtypeStruct((N_EXPERTS, l), F32)),
        grid=(l // tm,),
        in_specs=[pl.BlockSpec((tm, d), lambda i: (i, 0)),
                  pl.BlockSpec((tm, d), lambda i: (i, 0)),
                  pl.BlockSpec((1, V7X_SUBLANES, d), lambda i: (0, 0, 0)),
                  pl.BlockSpec((1, d), lambda i: (0, 0)),
                  pl.BlockSpec((1, d), lambda i: (0, 0)),
                  pl.BlockSpec((N_EXPERTS, d), lambda i: (0, 0))],
        out_specs=(pl.BlockSpec((tm, d), lambda i: (i, 0)),
                   pl.BlockSpec((tm, d), lambda i: (i, 0)),
                   pl.BlockSpec((N_EXPERTS, tm), lambda i: (0, i))),
        scratch_shapes=[pltpu.VMEM((tm, d), BF16)],
        compiler_params=_cparams(("arbitrary",), 48),
        name="post_mix",
    )(x2, mix, mods, ln_gain, ln_bias, wr_t)


def _topk_kernel(aff_ref, triu_ref, slow_ref, idx_ref, gate_ref, cum_scr, bend_v, bend_s, sem, *, cap):
    n_e, n_blk, n_lane = aff_ref.shape
    aff = aff_ref[...]
    def count(mask):
        c = jnp.sum(jnp.where(mask, 1.0, 0.0), axis=1, keepdims=True)
        return jnp.sum(c, axis=2, keepdims=True)

    def as_float(bits):
        return lax.bitcast_convert_type(bits, F32)

    def search(k, thr):
        cand = thr | jnp.left_shift(jnp.int32(1), 30 - k)
        return jnp.where(count(aff >= as_float(cand)) >= cap, cand, thr)

    thr = lax.fori_loop(0, 31, search, jnp.zeros((n_e, 1, 1), I32))
    above = aff >= as_float(thr + 1)
    equal = jnp.logical_and(aff >= as_float(thr), jnp.logical_not(above))
    need = cap - count(above)

    def prefix(mask):
        m = jnp.where(mask, 1.0, 0.0).astype(BF16)
        within = jnp.dot(m.reshape(n_e * n_blk, n_lane), triu_ref[...],
                         preferred_element_type=F32).reshape(n_e, n_blk, n_lane)
        tot = jnp.broadcast_to(within[:, :, n_lane - 1:n_lane], (n_e, n_blk, n_lane)).astype(BF16)
        offs = [jnp.dot(slow_ref[...], tot[e], preferred_element_type=F32) for e in range(n_e)]
        return within + jnp.stack(offs, axis=0)

    cum_above = prefix(above)
    cum_equal = prefix(equal)
    chosen = jnp.logical_or(above, jnp.logical_and(equal, cum_equal <= need))
    cum = cum_above + jnp.minimum(cum_equal, need)
    cum_scr[...] = jnp.where(chosen, cum, -1.0)

    last_lane = jnp.where(lax.broadcasted_iota(I32, (V7X_SUBLANES, n_lane), 1) == n_lane - 1, 1.0, 0.0)
    bend_v[...] = jnp.zeros_like(bend_v)
    for e in range(n_e):
        ends = lax.dot_general(last_lane, cum[e], (((1,), (1,)), ((), ())),
                               preferred_element_type=F32, precision=lax.Precision.HIGHEST)
        bend_v[e:e + 1, 0:n_blk] = ends[0:1, :].astype(I32)
    to_smem = pltpu.make_async_copy(bend_v, bend_s, sem)
    to_smem.start()
    to_smem.wait()

    jt = SLOT_TILE
    lane_f = lax.broadcasted_iota(I32, (jt, n_lane), 1).astype(F32)

    def per_expert(e, carry):
        def per_tile(t, b_start):
            first = t * jt
            b_lo = lax.while_loop(lambda b: jnp.logical_and(b < n_blk - 1, bend_s[e, b] <= first),
                                  lambda b: b + 1, b_start)
            b_hi = lax.while_loop(lambda b: jnp.logical_and(b < n_blk - 1, bend_s[e, b] < first + jt),
                                  lambda b: b + 1, b_lo)
            slot = (lax.broadcasted_iota(I32, (jt, n_lane), 0) + (first + 1)).astype(F32)

            def visit(b, accs):
                hit = cum_scr[e, pl.ds(b, 1), :] == slot
                return (accs[0] + jnp.where(hit, (b + 1).astype(F32), 0.0),
                        accs[1] + jnp.where(hit, aff_ref[e, pl.ds(b, 1), :], 0.0))

            zeros = jnp.zeros((jt, n_lane), F32)
            blk_acc, gate_acc = lax.fori_loop(b_lo, b_hi + 1, visit, (zeros, zeros))
            tok = jnp.where(blk_acc > 0.0, (blk_acc - 1.0) * n_lane + lane_f, 0.0)
            tok = jnp.sum(tok, axis=1, keepdims=True)
            gate = jnp.sum(gate_acc, axis=1, keepdims=True)
            rows = pl.ds(pl.multiple_of(t * jt, jt), jt)
            idx_ref[e, rows, :] = jnp.broadcast_to(tok, (jt, n_lane)).astype(I32)
            gate_ref[e, rows, :] = jnp.broadcast_to(gate, (jt, n_lane))
            return b_lo
        lax.fori_loop(0, cap // jt, per_tile, jnp.int32(0))
        return carry

    lax.fori_loop(0, n_e, per_expert, 0)


def _expert_choice_topk(aff_t, cap):
    n_e, l = aff_t.shape
    n_blk = l // V7X_LANES
    aff3 = aff_t.reshape(n_e, n_blk, V7X_LANES)
    triu = (jnp.arange(V7X_LANES)[:, None] <= jnp.arange(V7X_LANES)[None, :]).astype(BF16)
    slow = (jnp.arange(n_blk)[:, None] > jnp.arange(n_blk)[None, :]).astype(BF16)
    return pl.pallas_call(
        functools.partial(_topk_kernel, cap=cap),
        out_shape=(jax.ShapeDtypeStruct((n_e, cap, V7X_LANES), I32),
                   jax.ShapeDtypeStruct((n_e, cap, V7X_LANES), F32)),
        grid=(1,),
        in_specs=[pl.BlockSpec((n_e, n_blk, V7X_LANES), lambda i: (0, 0, 0)),
                  pl.BlockSpec((V7X_LANES, V7X_LANES), lambda i: (0, 0)),
                  pl.BlockSpec((n_blk, n_blk), lambda i: (0, 0))],
        out_specs=(pl.BlockSpec((n_e, cap, V7X_LANES), lambda i: (0, 0, 0)),
                   pl.BlockSpec((n_e, cap, V7X_LANES), lambda i: (0, 0, 0))),
        scratch_shapes=[pltpu.VMEM((n_e, n_blk, V7X_LANES), F32),
                        pltpu.VMEM((n_e, V7X_LANES), I32), pltpu.SMEM((n_e, V7X_LANES), I32),
                        pltpu.SemaphoreType.DMA(())],
        compiler_params=_cparams(("arbitrary",), 48),
        name="expert_choice_topk",
    )(aff3, triu, slow)


def _row_copy(src, dst, src_row, dst_row, sem):
    return pltpu.make_async_copy(src.at[pl.ds(src_row, 1), :], dst.at[pl.ds(dst_row, 1), :], sem)


def _wait_rows_in(hbm, buf, sem):
    pltpu.make_async_copy(hbm.at[pl.ds(0, buf.shape[0]), :], buf, sem).wait()


def _wait_rows_out(buf, hbm, sem):
    pltpu.make_async_copy(buf, hbm.at[pl.ds(0, buf.shape[0]), :], sem).wait()


def _ffn_kernel(idx_ref, hf, wg_ref, wu_ref, wd_ref, gate_ref, gf_ref, o_ref,
                xe_scr, h_scr, stage, sems, *, n_ff_tiles, n_experts, n_steps, cap):
    e = pl.program_id(0)
    s = pl.program_id(1)
    rows = cap // n_steps
    tf = wg_ref.shape[2]

    def issue(expert, chunk, slot):
        base = expert * cap + chunk * rows
        for j in range(rows):
            _row_copy(hf, stage.at[slot], idx_ref[base + j], j, sems.at[slot]).start()

    def land(buf, chunk, slot):
        _wait_rows_in(hf, stage.at[slot], sems.at[slot])
        first = chunk * rows
        r = pl.ds(first if isinstance(first, int) else pl.multiple_of(first, rows), rows)
        xe_scr[buf, r, :] = stage[slot].astype(BF16)

    @pl.when(jnp.logical_and(e == 0, s == 0))
    def _():
        issue(0, 0, 0)
        for k in range(n_steps - 1):
            issue(0, k + 1, (k + 1) % STAGE_SLOTS)
            land(0, k, k % STAGE_SLOTS)
        issue(1 % n_experts, 0, 0)

    def step_io():
        due = (s + n_steps - 1) % n_steps
        owner = e + jnp.where(s > 0, 1, 0)
        land(owner % 2, due, due % STAGE_SLOTS)
        ahead = (s + 1) % n_steps
        target = e + 1 + jnp.where(s == n_steps - 1, 1, 0)
        issue(target % n_experts, ahead, ahead % STAGE_SLOTS)

    @pl.when(s < n_ff_tiles)
    def _():
        step_io()
        xe = xe_scr[e % 2]
        a = jnp.dot(xe, wg_ref[0].astype(BF16), preferred_element_type=F32)
        u = jnp.dot(xe, wu_ref[0].astype(BF16), preferred_element_type=F32)
        h_scr[:, pl.ds(pl.multiple_of(s * tf, tf), tf)] = (jax.nn.silu(a) * u).astype(BF16)

    @pl.when(s >= n_ff_tiles)
    def _():
        step_io()
        y = jnp.dot(h_scr[...], wd_ref[0].astype(BF16), preferred_element_type=F32)
        o_ref[0] = (y * gate_ref[0, :, 0:1] * gf_ref[...]).astype(BF16)

    @pl.when(jnp.logical_and(e == n_experts - 1, s == n_steps - 1))
    def _():
        for chunk in (n_steps - 1, 0):
            _wait_rows_in(hf, stage.at[chunk % STAGE_SLOTS], sems.at[chunk % STAGE_SLOTS])


def _expert_ffn(idx_flat, hf_rows, w_gate, w_up, w_down, gates, gate_f, cap, tf, tn):
    n_e, d, ff = w_gate.shape
    n_ff = ff // tf
    n_out = d // tn
    n_steps = n_ff + n_out
    assert cap % n_steps == 0 and ff % tf == 0 and d % tn == 0 and n_steps % STAGE_SLOTS == 0
    rows = cap // n_steps
    col_a = lambda e, s, idx: (e, 0, jnp.minimum(s, n_ff - 1))
    col_b = lambda e, s, idx: (e, 0, jnp.maximum(s - n_ff, 0))
    return pl.pallas_call(
        functools.partial(_ffn_kernel, n_ff_tiles=n_ff, n_experts=n_e, n_steps=n_steps, cap=cap),
        out_shape=jax.ShapeDtypeStruct((n_e, cap, d), BF16),
        grid_spec=pltpu.PrefetchScalarGridSpec(
            num_scalar_prefetch=1, grid=(n_e, n_steps),
            in_specs=[pl.BlockSpec(memory_space=pl.ANY),
                      pl.BlockSpec((1, d, tf), col_a),
                      pl.BlockSpec((1, d, tf), col_a),
                      pl.BlockSpec((1, ff, tn), col_b),
                      pl.BlockSpec((1, cap, V7X_LANES), lambda e, s, idx: (e, 0, 0), pipeline_mode=pl.Buffered(1)),
                      pl.BlockSpec((1, tn), lambda e, s, idx: (0, jnp.maximum(s - n_ff, 0)))],
            out_specs=pl.BlockSpec((1, cap, tn), col_b),
            scratch_shapes=[pltpu.VMEM((2, cap, d), BF16), pltpu.VMEM((cap, ff), BF16),
                            pltpu.VMEM((STAGE_SLOTS, rows, d), F32),
                            pltpu.SemaphoreType.DMA((STAGE_SLOTS,))]),
        compiler_params=_cparams(("arbitrary", "arbitrary"), 60),
        name="expert_ffn",
    )(idx_flat, hf_rows, w_gate, w_up, w_down, gates, gate_f)


def _combine_kernel(idx_ref, ye_ref, acc_in, acc_out, buf, gsem, ssem, *, cap, n_tiles):
    e = pl.program_id(0)
    m = pl.program_id(1)
    n_buf, n_grp, grp, _ = buf.shape
    tm = n_grp * grp

    def for_rows(tile, fn):
        def body(g, carry):
            first = e * cap + tile * tm + g * grp
            for u in range(grp):
                fn(g, u, idx_ref[first + u])
            return carry
        lax.fori_loop(0, n_grp, body, 0)

    def fetch(tile, b):
        for_rows(tile, lambda g, u, tok: _row_copy(acc_in, buf.at[b, g], tok, u, gsem.at[b]).start())

    def put(tile, b):
        for_rows(tile, lambda g, u, tok: _row_copy(buf.at[b, g], acc_out, u, tok, ssem.at[b]).start())

    def fetch_wait(b):
        for g in range(n_grp):
            _wait_rows_in(acc_in, buf.at[b, g], gsem.at[b])

    def put_wait(b):
        for g in range(n_grp):
            _wait_rows_out(buf.at[b, g], acc_out, ssem.at[b])

    for t in range(n_tiles):
        @pl.when(m == t)
        def _(t=t):
            b = t % n_buf
            if t == 0:
                fetch(0, 0)
                if n_tiles > 1:
                    fetch(1, 1 % n_buf)
            elif t + 1 < n_tiles:
                nb = (t + 1) % n_buf
                if t + 1 >= n_buf:
                    put_wait(nb)
                fetch(t + 1, nb)
            fetch_wait(b)
            for g in range(n_grp):
                buf[b, g] = buf[b, g] + ye_ref[0, g * grp:(g + 1) * grp, :].astype(F32)
            put(t, b)
            if t == n_tiles - 1:
                for u in range(max(0, n_tiles - n_buf), n_tiles):
                    put_wait(u % n_buf)


def _combine(idx_flat, ye, acc, tm, n_buf):
    n_e, cap, d = ye.shape
    n_tiles = cap // tm
    return pl.pallas_call(
        functools.partial(_combine_kernel, cap=cap, n_tiles=n_tiles),
        out_shape=jax.ShapeDtypeStruct(acc.shape, acc.dtype),
        grid_spec=pltpu.PrefetchScalarGridSpec(
            num_scalar_prefetch=1, grid=(n_e, n_tiles),
            in_specs=[pl.BlockSpec((1, tm, d), lambda e, m, idx: (e, m, 0)),
                      pl.BlockSpec(memory_space=pl.ANY)],
            out_specs=pl.BlockSpec(memory_space=pl.ANY),
            scratch_shapes=[pltpu.VMEM((n_buf, tm // DMA_GROUP, DMA_GROUP, d), F32),
                            pltpu.SemaphoreType.DMA((n_buf,)), pltpu.SemaphoreType.DMA((n_buf,))]),
        input_output_aliases={2: 0},
        compiler_params=_cparams(("arbitrary", "arbitrary"), 32),
        name="combine",
    )(idx_flat, ye, acc)


def _final_ln_kernel(a_ref, g_ref, b_ref, o_ref):
    def body(r, carry):
        sl = pl.ds(pl.multiple_of(r * ROW_STRIP, ROW_STRIP), ROW_STRIP)
        o_ref[sl, :] = _ln_rows(a_ref[sl, :]) * g_ref[...] + b_ref[...]
        return carry
    lax.fori_loop(0, a_ref.shape[0] // ROW_STRIP, body, 0)


def _final_ln(acc, gain, bias, tm):
    l, d = acc.shape
    return pl.pallas_call(
        _final_ln_kernel,
        out_shape=jax.ShapeDtypeStruct((l, d), F32),
        grid=(l // tm,),
        in_specs=[pl.BlockSpec((tm, d), lambda i: (i, 0)),
                  pl.BlockSpec((1, d), lambda i: (0, 0)),
                  pl.BlockSpec((1, d), lambda i: (0, 0))],
        out_specs=pl.BlockSpec((tm, d), lambda i: (i, 0)),
        compiler_params=_cparams(("arbitrary",), 32),
        name="final_ln",
    )(acc, gain, bias)


def _pad_rows(a, rows):
    return jnp.pad(a, ((0, rows - a.shape[0]), (0, 0)))


def _pad_lanes(a, lanes):
    return jnp.pad(a, ((0, 0), (0, lanes - a.shape[1])))


def kernel(x, c, ctx, c_ctx, w_ada, b_ada, w_in, gm_v_gain, gm_v_bias, gm_w_s, gm_b_s, ssm_conv_w, ssm_conv_b, ssm_dt_bias, ssm_a_log, ssm_d, ssm_norm_gain, w_out, ln_mix_gain, ln_mix_bias, w_router, w_gate, w_up, w_down, ln_ffn_gain, ln_ffn_bias):
    b, l, d = x.shape
    lc = ctx.shape[1]
    assert b == 1 and d == D_MODEL and w_ada.shape[0] == DEPTH
    x2 = x[0]
    ctx2 = ctx[0]
    layer = 0

    cond = _pad_rows(jnp.concatenate([c, c_ctx[None, :]], axis=0), V7X_SUBLANES)
    ada = _ada_params(cond, w_ada[layer], b_ada[layer][None, :])
    mods = jnp.pad(ada[:2].reshape(2, 6, d), ((0, 0), (0, V7X_SUBLANES - 6), (0, 0)))

    w_in_t = jnp.swapaxes(w_in, 1, 2)
    wdt = _pad_rows(w_in_t[layer, COL_DT:, :], V7X_LANES).astype(BF16)
    dt_bias = _pad_lanes(ssm_dt_bias[layer].reshape(1, 2 * SSM_HEADS), V7X_LANES)
    a_log = _pad_lanes(ssm_a_log[layer].reshape(1, 2 * SSM_HEADS), V7X_LANES)
    h, dtda = _modulate_in(x2, ctx2, mods, wdt, dt_bias, a_log)
    proj = _matmul_nt(h, w_in_t, COL_DT, tm=1408, tn=512, name="in_proj")

    conv_w8 = _pad_rows(ssm_conv_w[layer], V7X_SUBLANES)
    conv = _conv_silu(proj, conv_w8, ssm_conv_b[layer][None, :], rb=lc, tc=2048)

    n_lat_chunks = l // CHUNK
    n_ctx_chunks = lc // SSD_CHUNK
    gm = _chunk_mlp(proj, lc, n_lat_chunks, gm_v_gain[layer][None, :], gm_v_bias[layer][None, :],
                    gm_w_s[layer].astype(BF16), gm_b_s[layer].T)

    y_fwd = _ssd_forward(conv, dtda, n_ctx_chunks, n_lat_chunks)
    d_skip = jnp.repeat(ssm_d[layer], SSM_HEAD_DIM)[None, :]
    ss = _ssd_backward(conv, dtda, y_fwd, proj, d_skip, ssm_norm_gain[layer][None, :],
                       n_ctx_chunks, n_lat_chunks)

    mix = _matmul_concat(gm, ss, w_out[layer], tm=1024, tn=512, name="out_proj")

    hf_rows, acc, aff_t = _post_mix(x2, mix, mods, ln_mix_gain[layer][None, :], ln_mix_bias[layer][None, :],
                                      w_router[layer].T.astype(BF16), tm=256)

    cap = EC_FACTOR * l // N_EXPERTS
    idx3, gates = _expert_choice_topk(aff_t, cap)
    idx_flat = idx3[:, :, 0].reshape(N_EXPERTS * cap)

    gate_f = mods[0, 5:6, :]
    ye = _expert_ffn(idx_flat, hf_rows, w_gate[layer], w_up[layer], w_down[layer], gates, gate_f,
                     cap, tf=256, tn=512)
    acc = _combine(idx_flat, ye, acc, tm=256, n_buf=3)
    out = _final_ln(acc, ln_ffn_gain[layer][None, :], ln_ffn_bias[layer][None, :], tm=256)
    return out[None]
```

```python
import functools

import jax
import jax.numpy as jnp
from jax import lax
from jax.experimental import pallas as pl
from jax.experimental.pallas import tpu as pltpu

F32 = jnp.float32
BF16 = jnp.bfloat16
I32 = jnp.int32

D_MODEL = 4096
GRID_W = 64
CHUNK = 128
GM_WIDTH = 2048
GM_HEADS = 8
GM_HEAD_DIM = GM_WIDTH // GM_HEADS
SSM_WIDTH = 2048
SSM_HEAD_DIM = 64
SSM_HEADS = SSM_WIDTH // SSM_HEAD_DIM
SSM_GROUPS = 8
SSM_STATE = 128
SSM_CONV = 5
SSM_XBC = SSM_WIDTH + 2 * SSM_GROUPS * SSM_STATE
SSD_CHUNK = 128
HEADS_PER_GROUP = SSM_HEADS // SSM_GROUPS
GROUP_WIDTH = SSM_WIDTH // SSM_GROUPS
N_EXPERTS = 16
EXPERT_FF = 2048
EC_FACTOR = 2
DEPTH = 1
ALPHA = (2 * DEPTH) ** 0.25
LN_EPS = 1e-5
RMS_EPS = 1e-5

COL_Z = 2 * GM_WIDTH
COL_XBC = COL_Z + SSM_WIDTH
COL_DT = COL_XBC + SSM_XBC

V7X_LANES = 128
V7X_SUBLANES = 8
V7X_VMEM_BYTES = 64 * 1024 * 1024
MIB = 1024 * 1024

ROW_STRIP = 128
SLOT_TILE = 128
DMA_GROUP = 32
STAGE_SLOTS = 4
GMLP_CHUNKS_PER_STEP = 2


def _cparams(semantics, vmem_mib):
    return pltpu.CompilerParams(dimension_semantics=semantics,
                                vmem_limit_bytes=min(vmem_mib * MIB, V7X_VMEM_BYTES - 2 * MIB))


def _ln_rows(x):
    mu = jnp.mean(x, axis=-1, keepdims=True)
    xc = x - mu
    var = jnp.mean(xc * xc, axis=-1, keepdims=True)
    return xc * lax.rsqrt(var + LN_EPS)


def _ada_kernel(c_ref, w_ref, b_ref, o_ref):
    s = jax.nn.silu(c_ref[...]).astype(BF16)
    o_ref[...] = jnp.dot(s, w_ref[...].astype(BF16), preferred_element_type=F32) + b_ref[...]


def _ada_params(cond, w_ada, b_ada):
    k, n = w_ada.shape
    tn = 512
    return pl.pallas_call(
        _ada_kernel,
        out_shape=jax.ShapeDtypeStruct((V7X_SUBLANES, n), F32),
        grid=(n // tn,),
        in_specs=[pl.BlockSpec((V7X_SUBLANES, k), lambda j: (0, 0)),
                  pl.BlockSpec((k, tn), lambda j: (0, j)),
                  pl.BlockSpec((1, tn), lambda j: (0, j))],
        out_specs=pl.BlockSpec((V7X_SUBLANES, tn), lambda j: (0, j)),
        compiler_params=_cparams(("arbitrary",), 32),
        name="ada_params",
    )(cond, w_ada, b_ada)


def _modin_kernel(x_ref, ctx_ref, p_ref, wdt_ref, bias_ref, alog_ref, h_ref, dtda_ref):
    i = pl.program_id(0)
    shift = p_ref[0, 0:1, :]
    scale1 = 1.0 + p_ref[0, 1:2, :]
    tm = h_ref.shape[0]

    def rows(src_ref):
        def body(r, carry):
            sl = pl.ds(pl.multiple_of(r * ROW_STRIP, ROW_STRIP), ROW_STRIP)
            h_ref[sl, :] = (_ln_rows(src_ref[sl, :]) * scale1 + shift).astype(BF16)
            return carry
        lax.fori_loop(0, tm // ROW_STRIP, body, 0)

    @pl.when(i == 0)
    def _():
        rows(ctx_ref)

    @pl.when(i > 0)
    def _():
        rows(x_ref)

    raw = lax.dot_general(h_ref[...], wdt_ref[...], (((1,), (1,)), ((), ())),
                          preferred_element_type=F32) + bias_ref[...]
    dt = jnp.maximum(raw, 0.0) + jnp.log1p(jnp.exp(-jnp.abs(raw)))
    dtda_ref[:, 0:V7X_LANES] = dt
    dtda_ref[:, V7X_LANES:2 * V7X_LANES] = dt * (-jnp.exp(alog_ref[...]))


def _modulate_in(x2, ctx2, mods, wdt, dt_bias, a_log):
    l, d = x2.shape
    lc = ctx2.shape[0]
    tm = lc
    n_tiles = (l + lc) // tm
    return pl.pallas_call(
        _modin_kernel,
        out_shape=(jax.ShapeDtypeStruct((l + lc, d), BF16),
                   jax.ShapeDtypeStruct((l + lc, 2 * V7X_LANES), F32)),
        grid=(n_tiles,),
        in_specs=[pl.BlockSpec((tm, d), lambda i: (jnp.maximum(i - 1, 0), 0)),
                  pl.BlockSpec((tm, d), lambda i: (0, 0)),
                  pl.BlockSpec((1, V7X_SUBLANES, d), lambda i: (jnp.where(i == 0, 1, 0), 0, 0)),
                  pl.BlockSpec((V7X_LANES, d), lambda i: (0, 0)),
                  pl.BlockSpec((1, V7X_LANES), lambda i: (0, 0)),
                  pl.BlockSpec((1, V7X_LANES), lambda i: (0, 0))],
        out_specs=(pl.BlockSpec((tm, d), lambda i: (i, 0)),
                   pl.BlockSpec((tm, 2 * V7X_LANES), lambda i: (i, 0))),
        compiler_params=_cparams(("arbitrary",), 40),
        name="modulate_in",
    )(x2, ctx2, mods, wdt, dt_bias, a_log)


def _mm_nt_kernel(a_ref, wt_ref, o_ref, wbf_ref):
    @pl.when(pl.program_id(1) == 0)
    def _():
        wbf_ref[...] = wt_ref[...].astype(BF16)

    o_ref[...] = lax.dot_general(a_ref[...], wbf_ref[...], (((1,), (1,)), ((), ())),
                                 preferred_element_type=F32)


def _matmul_nt(a, wt3, n_out, tm, tn, name):
    m, k = a.shape
    return pl.pallas_call(
        _mm_nt_kernel,
        out_shape=jax.ShapeDtypeStruct((m, n_out), F32),
        grid=(n_out // tn, m // tm),
        in_specs=[pl.BlockSpec((tm, k), lambda j, i: (i, 0)),
                  pl.BlockSpec((None, tn, k), lambda j, i: (0, j, 0))],
        out_specs=pl.BlockSpec((tm, tn), lambda j, i: (i, j)),
        scratch_shapes=[pltpu.VMEM((tn, k), BF16)],
        compiler_params=_cparams(("arbitrary", "arbitrary"), 56),
        name=name,
    )(a, wt3)


def _mm2_kernel(a0_ref, a1_ref, w_ref, o_ref, wbf_ref):
    @pl.when(pl.program_id(1) == 0)
    def _():
        wbf_ref[...] = w_ref[...].astype(BF16)

    k0 = a0_ref.shape[1]
    acc = jnp.dot(a0_ref[...], wbf_ref[0:k0, :], preferred_element_type=F32)
    o_ref[...] = acc + jnp.dot(a1_ref[...], wbf_ref[k0:, :], preferred_element_type=F32)


def _matmul_concat(a0, a1, w, tm, tn, name):
    m, k0 = a0.shape
    k1 = a1.shape[1]
    n_out = w.shape[1]
    return pl.pallas_call(
        _mm2_kernel,
        out_shape=jax.ShapeDtypeStruct((m, n_out), F32),
        grid=(n_out // tn, m // tm),
        in_specs=[pl.BlockSpec((tm, k0), lambda j, i: (i, 0)),
                  pl.BlockSpec((tm, k1), lambda j, i: (i, 0)),
                  pl.BlockSpec((k0 + k1, tn), lambda j, i: (0, j))],
        out_specs=pl.BlockSpec((tm, tn), lambda j, i: (i, j)),
        scratch_shapes=[pltpu.VMEM((k0 + k1, tn), BF16)],
        compiler_params=_cparams(("arbitrary", "arbitrary"), 56),
        name=name,
    )(a0, a1, w)


def _conv_kernel(main_ref, prev_ref, next_ref, w_ref, b_ref, o_ref, ext_ref):
    i = pl.program_id(0)
    rb = main_ref.shape[0]
    halo = V7X_SUBLANES
    pad = (SSM_CONV - 1) // 2
    seq_start = jnp.logical_or(i == 0, i == 1)
    seq_end = jnp.logical_or(i == 0, i == pl.num_programs(0) - 1)
    ext_ref[0:halo, :] = jnp.where(seq_start, 0.0, prev_ref[...])
    ext_ref[halo:halo + rb, :] = main_ref[...]
    ext_ref[halo + rb:2 * halo + rb, :] = jnp.where(seq_end, 0.0, next_ref[...])
    ext = ext_ref[...]
    n_ext = ext.shape[0]
    acc = jnp.broadcast_to(b_ref[...], o_ref.shape)
    for k in range(SSM_CONV):
        shifted = ext if k == pad else pltpu.roll(ext, (pad - k) % n_ext, axis=0)
        acc = acc + w_ref[k:k + 1, :] * shifted[halo:halo + rb, :]
    o_ref[...] = acc * jax.nn.sigmoid(acc)


def _conv_silu(proj, conv_w8, conv_b, rb, tc):
    rows = proj.shape[0]
    n_row_tiles = rows // rb
    halo_per_tile = rb // V7X_SUBLANES
    n_halo_blocks = rows // V7X_SUBLANES
    col0 = COL_XBC // tc
    return pl.pallas_call(
        _conv_kernel,
        out_shape=jax.ShapeDtypeStruct((rows, SSM_XBC), F32),
        grid=(n_row_tiles, SSM_XBC // tc),
        in_specs=[pl.BlockSpec((rb, tc), lambda i, j: (i, col0 + j)),
                  pl.BlockSpec((V7X_SUBLANES, tc),
                               lambda i, j: (jnp.maximum(i * halo_per_tile - 1, 0), col0 + j)),
                  pl.BlockSpec((V7X_SUBLANES, tc),
                               lambda i, j: (jnp.minimum((i + 1) * halo_per_tile, n_halo_blocks - 1), col0 + j)),
                  pl.BlockSpec((V7X_SUBLANES, tc), lambda i, j: (0, j)),
                  pl.BlockSpec((1, tc), lambda i, j: (0, j))],
        out_specs=pl.BlockSpec((rb, tc), lambda i, j: (i, j)),
        scratch_shapes=[pltpu.VMEM((rb + 2 * V7X_SUBLANES, tc), F32)],
        compiler_params=_cparams(("arbitrary", "arbitrary"), 32),
        name="conv_silu",
    )(proj, proj, proj, conv_w8, conv_b)


def _gmlp_kernel(uv_ref, gain_ref, bias_ref, ws_ref, bst_ref, o_ref):
    for ci in range(uv_ref.shape[0] // CHUNK):
        rows = slice(ci * CHUNK, (ci + 1) * CHUNK)
        g = jax.nn.gelu(uv_ref[rows, :])
        u = g[:, 0:GM_WIDTH]
        v = _ln_rows(g[:, GM_WIDTH:2 * GM_WIDTH]) * gain_ref[...] + bias_ref[...]
        vb = v.astype(BF16)
        for h in range(GM_HEADS):
            cols = slice(h * GM_HEAD_DIM, (h + 1) * GM_HEAD_DIM)
            mixed = jnp.dot(ws_ref[h], vb[:, cols], preferred_element_type=F32) + bst_ref[:, h:h + 1]
            o_ref[rows, cols] = (u[:, cols] * mixed).astype(BF16)


def _chunk_mlp(proj, lat_row0, n_chunks, v_gain, v_bias, ws_bf, bs_t):
    rows_per_step = GMLP_CHUNKS_PER_STEP * CHUNK
    assert lat_row0 % rows_per_step == 0 and n_chunks % GMLP_CHUNKS_PER_STEP == 0
    blk0 = lat_row0 // rows_per_step
    return pl.pallas_call(
        _gmlp_kernel,
        out_shape=jax.ShapeDtypeStruct((n_chunks * CHUNK, GM_WIDTH), BF16),
        grid=(n_chunks // GMLP_CHUNKS_PER_STEP,),
        in_specs=[pl.BlockSpec((rows_per_step, 2 * GM_WIDTH), lambda c: (blk0 + c, 0)),
                  pl.BlockSpec((1, GM_WIDTH), lambda c: (0, 0)),
                  pl.BlockSpec((1, GM_WIDTH), lambda c: (0, 0)),
                  pl.BlockSpec((GM_HEADS, CHUNK, CHUNK), lambda c: (0, 0, 0)),
                  pl.BlockSpec((CHUNK, GM_HEADS), lambda c: (0, 0))],
        out_specs=pl.BlockSpec((rows_per_step, GM_WIDTH), lambda c: (c, 0)),
        compiler_params=_cparams(("arbitrary",), 32),
        name="chunk_mlp",
    )(proj, v_gain, v_bias, ws_bf, bs_t)


def _ssd_chunk(direction, xs_ref, b_ref, c_ref, dtda_ref, mcum_ref, expand_ref, st_ref, emit_group):
    d = direction
    lane0 = d * SSM_HEADS
    dt = dtda_ref[:, 0:V7X_LANES]
    d_a = dtda_ref[:, V7X_LANES:2 * V7X_LANES]
    cs = jnp.dot(mcum_ref[...], d_a, preferred_element_type=F32, precision=lax.Precision.HIGHEST)
    cs_t = cs.T
    last = SSD_CHUNK - 1 if d == 0 else 0
    cs_end = cs[last:last + 1, :]
    dt_decay = dt * jnp.exp(cs_end - cs)
    exp_cs = jnp.exp(cs)
    chunk_decay = jnp.broadcast_to(jnp.exp(cs_end), (V7X_SUBLANES, V7X_LANES))
    q = jnp.concatenate([dt, dt_decay, chunk_decay], axis=0)
    q_hi = q.astype(BF16)
    q_lo = (q - q_hi.astype(F32)).astype(BF16)

    row_i = lax.broadcasted_iota(I32, (SSD_CHUNK, SSD_CHUNK), 0)
    col_i = lax.broadcasted_iota(I32, (SSD_CHUNK, SSD_CHUNK), 1)
    causal = (row_i >= col_i) if d == 0 else (row_i <= col_i)
    first_head = col_i < SSM_HEAD_DIM

    for g in range(SSM_GROUPS):
        gcols = slice(g * GROUP_WIDTH, (g + 1) * GROUP_WIDTH)
        ncols = slice(g * SSM_STATE, (g + 1) * SSM_STATE)
        c32 = c_ref[:, ncols]
        b_bf = b_ref[:, ncols].astype(BF16)
        c_bf = c32.astype(BF16)
        cb = lax.dot_general(c_bf, b_bf, (((1,), (1,)), ((), ())), preferred_element_type=F32)
        e_g = expand_ref[:, gcols]
        q_e = (jnp.dot(q_hi, e_g, preferred_element_type=F32)
               + jnp.dot(q_lo, e_g, preferred_element_type=F32))
        xs_g = xs_ref[:, gcols]
        x_dt = (xs_g * q_e[0:SSD_CHUNK]).astype(BF16)
        x_dec = (xs_g * q_e[SSD_CHUNK:2 * SSD_CHUNK]).astype(BF16)
        h_old = st_ref[g]
        h_bf = h_old.astype(BF16)
        y_pairs = []
        for pr in range(HEADS_PER_GROUP // 2):
            pcols = slice(pr * V7X_LANES, (pr + 1) * V7X_LANES)
            xp = x_dt[:, pcols]
            hp = h_bf[:, pcols]
            zero = jnp.zeros_like(xp)
            rhs = jnp.concatenate([jnp.where(first_head, xp, zero), jnp.where(first_head, hp, zero),
                                   jnp.where(first_head, zero, xp), jnp.where(first_head, zero, hp)], axis=0)
            parts = []
            for k in range(2):
                lane = lane0 + g * HEADS_PER_GROUP + 2 * pr + k
                seg = jnp.exp(jnp.where(causal, cs[:, lane:lane + 1] - cs_t[lane:lane + 1, :], -jnp.inf))
                parts.append((cb * seg).astype(BF16))
                parts.append((c32 * exp_cs[:, lane:lane + 1]).astype(BF16))
            lhs = jnp.concatenate(parts, axis=1)
            y_pairs.append(jnp.dot(lhs, rhs, preferred_element_type=F32))
        emit_group(g, xs_g, jnp.concatenate(y_pairs, axis=1))
        s_new = lax.dot_general(b_bf, x_dec, (((0,), (0,)), ((), ())), preferred_element_type=F32)
        st_ref[g] = h_old * q_e[2 * SSD_CHUNK:2 * SSD_CHUNK + 1] + s_new


def _ssd_fwd_kernel(xs_ref, b_ref, c_ref, dtda_ref, mcum_ref, expand_ref, y_ref, st_ref):
    @pl.when(pl.program_id(0) == 0)
    def _():
        st_ref[...] = jnp.zeros_like(st_ref)

    def emit(g, xs_g, y_g):
        y_ref[:, g * GROUP_WIDTH:(g + 1) * GROUP_WIDTH] = y_g

    _ssd_chunk(0, xs_ref, b_ref, c_ref, dtda_ref, mcum_ref, expand_ref, st_ref, emit)


def _ssd_bwd_kernel(xs_ref, b_ref, c_ref, dtda_ref, mcum_ref, expand_ref, yf_ref, z_ref,
                    dskip_ref, ngain_ref, o_ref, st_ref):
    @pl.when(pl.program_id(0) == 0)
    def _():
        st_ref[...] = jnp.zeros_like(st_ref)

    def emit(g, xs_g, y_g):
        gcols = slice(g * GROUP_WIDTH, (g + 1) * GROUP_WIDTH)
        y = y_g + yf_ref[:, gcols] + dskip_ref[:, gcols] * xs_g
        gated = y * jax.nn.silu(z_ref[:, gcols])
        ms = jnp.mean(gated * gated, axis=-1, keepdims=True)
        o_ref[:, gcols] = (gated * lax.rsqrt(ms + RMS_EPS) * ngain_ref[:, gcols]).astype(BF16)

    _ssd_chunk(1, xs_ref, b_ref, c_ref, dtda_ref, mcum_ref, expand_ref, st_ref, emit)


def _ssd_specs(chunk_of_step):
    xs_blk = SSM_WIDTH
    n_blk = SSM_GROUPS * SSM_STATE
    return [pl.BlockSpec((SSD_CHUNK, xs_blk), lambda i: (chunk_of_step(i), 0)),
            pl.BlockSpec((SSD_CHUNK, n_blk), lambda i: (chunk_of_step(i), SSM_WIDTH // n_blk)),
            pl.BlockSpec((SSD_CHUNK, n_blk), lambda i: (chunk_of_step(i), SSM_WIDTH // n_blk + 1)),
            pl.BlockSpec((SSD_CHUNK, 2 * V7X_LANES), lambda i: (chunk_of_step(i), 0)),
            pl.BlockSpec((SSD_CHUNK, SSD_CHUNK), lambda i: (0, 0)),
            pl.BlockSpec((V7X_LANES, SSM_WIDTH), lambda i: (0, 0))]


def _ssd_forward(conv, dtda, n_ctx_chunks, n_lat_chunks):
    n_steps = n_ctx_chunks + n_lat_chunks
    tri = (jnp.arange(SSD_CHUNK)[:, None] >= jnp.arange(SSD_CHUNK)[None, :]).astype(F32)
    expand = _head_expand(0)
    return pl.pallas_call(
        _ssd_fwd_kernel,
        out_shape=jax.ShapeDtypeStruct((n_lat_chunks * SSD_CHUNK, SSM_WIDTH), F32),
        grid=(n_steps,),
        in_specs=_ssd_specs(lambda i: i),
        out_specs=pl.BlockSpec((SSD_CHUNK, SSM_WIDTH), lambda i: (jnp.maximum(i - n_ctx_chunks, 0), 0)),
        scratch_shapes=[pltpu.VMEM((SSM_GROUPS, SSM_STATE, GROUP_WIDTH), F32)],
        compiler_params=_cparams(("arbitrary",), 40),
        name="ssd_forward",
    )(conv, conv, conv, dtda, tri, expand)


def _ssd_backward(conv, dtda, y_fwd, proj, d_skip, norm_gain, n_ctx_chunks, n_lat_chunks):
    n_steps = n_ctx_chunks + n_lat_chunks
    tri = (jnp.arange(SSD_CHUNK)[:, None] <= jnp.arange(SSD_CHUNK)[None, :]).astype(F32)
    expand = _head_expand(1)

    def chunk_of_step(i):
        return jnp.where(i < n_ctx_chunks, n_ctx_chunks - 1 - i, n_steps - 1 + n_ctx_chunks - i)

    def lat_of_step(i):
        return jnp.where(i < n_ctx_chunks, n_lat_chunks - 1, n_steps - 1 - i)

    return pl.pallas_call(
        _ssd_bwd_kernel,
        out_shape=jax.ShapeDtypeStruct((n_lat_chunks * SSD_CHUNK, SSM_WIDTH), BF16),
        grid=(n_steps,),
        in_specs=_ssd_specs(chunk_of_step) + [
            pl.BlockSpec((SSD_CHUNK, SSM_WIDTH), lambda i: (lat_of_step(i), 0)),
            pl.BlockSpec((SSD_CHUNK, SSM_WIDTH), lambda i: (chunk_of_step(i), COL_Z // SSM_WIDTH)),
            pl.BlockSpec((1, SSM_WIDTH), lambda i: (0, 0)),
            pl.BlockSpec((1, SSM_WIDTH), lambda i: (0, 0))],
        out_specs=pl.BlockSpec((SSD_CHUNK, SSM_WIDTH), lambda i: (lat_of_step(i), 0)),
        scratch_shapes=[pltpu.VMEM((SSM_GROUPS, SSM_STATE, GROUP_WIDTH), F32)],
        compiler_params=_cparams(("arbitrary",), 40),
        name="ssd_backward",
    )(conv, conv, conv, dtda, tri, expand, y_fwd, proj, d_skip, norm_gain)


def _head_expand(direction):
    lane = jnp.arange(V7X_LANES)[:, None]
    chan = jnp.arange(SSM_WIDTH)[None, :]
    return (lane == direction * SSM_HEADS + chan // SSM_HEAD_DIM).astype(BF16)


def _postmix_kernel(x_ref, mix_ref, p_ref, lng_ref, lnb_ref, wrt_ref, hf_ref, acc_ref, aff_ref, hf_scr):
    gate_m = p_ref[0, 2:3, :]
    shift_f = p_ref[0, 3:4, :]
    scale1_f = 1.0 + p_ref[0, 4:5, :]
    tm = x_ref.shape[0]

    def body(r, carry):
        sl = pl.ds(pl.multiple_of(r * ROW_STRIP, ROW_STRIP), ROW_STRIP)
        xn = _ln_rows(ALPHA * x_ref[sl, :] + gate_m * mix_ref[sl, :]) * lng_ref[...] + lnb_ref[...]
        acc_ref[sl, :] = ALPHA * xn
        hf = _ln_rows(xn) * scale1_f + shift_f
        hf_ref[sl, :] = hf
        hf_scr[sl, :] = hf.astype(BF16)
        return carry

    lax.fori_loop(0, tm // ROW_STRIP, body, 0)
    logits = lax.dot_general(wrt_ref[...], hf_scr[...], (((1,), (1,)), ((), ())),
                             preferred_element_type=F32)
    mx = jnp.max(logits, axis=0, keepdims=True)
    ex = jnp.exp(logits - mx)
    aff_ref[...] = ex / jnp.sum(ex, axis=0, keepdims=True)


def _post_mix(x2, mix, mods, ln_gain, ln_bias, wr_t, tm):
    l, d = x2.shape
    return pl.pallas_call(
        _postmix_kernel,
        out_shape=(jax.ShapeDtypeStruct((l, d), F32),
                   jax.ShapeDtypeStruct((l, d), F32),
                   jax.ShapeDtypeStruct((N_EXPERTS, l), F32)),
        grid=(l // tm,),
        in_specs=[pl.BlockSpec((tm, d), lambda i: (i, 0)),
                  pl.BlockSpec((tm, d), lambda i: (i, 0)),
                  pl.BlockSpec((1, V7X_SUBLANES, d), lambda i: (0, 0, 0)),
                  pl.BlockSpec((1, d), lambda i: (0, 0)),
                  pl.BlockSpec((1, d), lambda i: (0, 0)),
                  pl.BlockSpec((N_EXPERTS, d), lambda i: (0, 0))],
        out_specs=(pl.BlockSpec((tm, d), lambda i: (i, 0)),
                   pl.BlockSpec((tm, d), lambda i: (i, 0)),
                   pl.BlockSpec((N_EXPERTS, tm), lambda i: (0, i))),
        scratch_shapes=[pltpu.VMEM((tm, d), BF16)],
        compiler_params=_cparams(("arbitrary",), 48),
        name="post_mix",
    )(x2, mix, mods, ln_gain, ln_bias, wr_t)


def _topk_kernel(aff_ref, triu_ref, slow_ref, idx_ref, gate_ref, cum_scr, bend_v, bend_s, sem, *, cap):
    n_e, n_blk, n_lane = aff_ref.shape
    aff = aff_ref[...]
    def count(mask):
        c = jnp.sum(jnp.where(mask, 1.0, 0.0), axis=1, keepdims=True)
        return jnp.sum(c, axis=2, keepdims=True)

    def as_float(bits):
        return lax.bitcast_convert_type(bits, F32)

    def search(k, thr):
        cand = thr | jnp.left_shift(jnp.int32(1), 30 - k)
        return jnp.where(count(aff >= as_float(cand)) >= cap, cand, thr)

    thr = lax.fori_loop(0, 31, search, jnp.zeros((n_e, 1, 1), I32))
    above = aff >= as_float(thr + 1)
    equal = jnp.logical_and(aff >= as_float(thr), jnp.logical_not(above))
    need = cap - count(above)

    def prefix(mask):
        m = jnp.where(mask, 1.0, 0.0).astype(BF16)
        within = jnp.dot(m.reshape(n_e * n_blk, n_lane), triu_ref[...],
                         preferred_element_type=F32).reshape(n_e, n_blk, n_lane)
        tot = jnp.broadcast_to(within[:, :, n_lane - 1:n_lane], (n_e, n_blk, n_lane)).astype(BF16)
        offs = [jnp.dot(slow_ref[...], tot[e], preferred_element_type=F32) for e in range(n_e)]
        return within + jnp.stack(offs, axis=0)

    cum_above = prefix(above)
    cum_equal = prefix(equal)
    chosen = jnp.logical_or(above, jnp.logical_and(equal, cum_equal <= need))
    cum = cum_above + jnp.minimum(cum_equal, need)
    cum_scr[...] = jnp.where(chosen, cum, -1.0)

    last_lane = jnp.where(lax.broadcasted_iota(I32, (V7X_SUBLANES, n_lane), 1) == n_lane - 1, 1.0, 0.0)
    bend_v[...] = jnp.zeros_like(bend_v)
    for e in range(n_e):
        ends = lax.dot_general(last_lane, cum[e], (((1,), (1,)), ((), ())),
                               preferred_element_type=F32, precision=lax.Precision.HIGHEST)
        bend_v[e:e + 1, 0:n_blk] = ends[0:1, :].astype(I32)
    to_smem = pltpu.make_async_copy(bend_v, bend_s, sem)
    to_smem.start()
    to_smem.wait()

    jt = SLOT_TILE
    lane_f = lax.broadcasted_iota(I32, (jt, n_lane), 1).astype(F32)

    def per_expert(e, carry):
        def per_tile(t, b_start):
            first = t * jt
            b_lo = lax.while_loop(lambda b: jnp.logical_and(b < n_blk - 1, bend_s[e, b] <= first),
                                  lambda b: b + 1, b_start)
            b_hi = lax.while_loop(lambda b: jnp.logical_and(b < n_blk - 1, bend_s[e, b] < first + jt),
                                  lambda b: b + 1, b_lo)
            slot = (lax.broadcasted_iota(I32, (jt, n_lane), 0) + (first + 1)).astype(F32)

            def visit(b, accs):
                hit = cum_scr[e, pl.ds(b, 1), :] == slot
                return (accs[0] + jnp.where(hit, (b + 1).astype(F32), 0.0),
                        accs[1] + jnp.where(hit, aff_ref[e, pl.ds(b, 1), :], 0.0))

            zeros = jnp.zeros((jt, n_lane), F32)
            blk_acc, gate_acc = lax.fori_loop(b_lo, b_hi + 1, visit, (zeros, zeros))
            tok = jnp.where(blk_acc > 0.0, (blk_acc - 1.0) * n_lane + lane_f, 0.0)
            tok = jnp.sum(tok, axis=1, keepdims=True)
            gate = jnp.sum(gate_acc, axis=1, keepdims=True)
            rows = pl.ds(pl.multiple_of(t * jt, jt), jt)
            idx_ref[e, rows, :] = jnp.broadcast_to(tok, (jt, n_lane)).astype(I32)
            gate_ref[e, rows, :] = jnp.broadcast_to(gate, (jt, n_lane))
            return b_lo
        lax.fori_loop(0, cap // jt, per_tile, jnp.int32(0))
        return carry

    lax.fori_loop(0, n_e, per_expert, 0)


def _expert_choice_topk(aff_t, cap):
    n_e, l = aff_t.shape
    n_blk = l // V7X_LANES
    aff3 = aff_t.reshape(n_e, n_blk, V7X_LANES)
    triu = (jnp.arange(V7X_LANES)[:, None] <= jnp.arange(V7X_LANES)[None, :]).astype(BF16)
    slow = (jnp.arange(n_blk)[:, None] > jnp.arange(n_blk)[None, :]).astype(BF16)
    return pl.pallas_call(
        functools.partial(_topk_kernel, cap=cap),
        out_shape=(jax.ShapeDtypeStruct((n_e, cap, V7X_LANES), I32),
                   jax.ShapeDtypeStruct((n_e, cap, V7X_LANES), F32)),
        grid=(1,),
        in_specs=[pl.BlockSpec((n_e, n_blk, V7X_LANES), lambda i: (0, 0, 0)),
                  pl.BlockSpec((V7X_LANES, V7X_LANES), lambda i: (0, 0)),
                  pl.BlockSpec((n_blk, n_blk), lambda i: (0, 0))],
        out_specs=(pl.BlockSpec((n_e, cap, V7X_LANES), lambda i: (0, 0, 0)),
                   pl.BlockSpec((n_e, cap, V7X_LANES), lambda i: (0, 0, 0))),
        scratch_shapes=[pltpu.VMEM((n_e, n_blk, V7X_LANES), F32),
                        pltpu.VMEM((n_e, V7X_LANES), I32), pltpu.SMEM((n_e, V7X_LANES), I32),
                        pltpu.SemaphoreType.DMA(())],
        compiler_params=_cparams(("arbitrary",), 48),
        name="expert_choice_topk",
    )(aff3, triu, slow)


def _row_copy(src, dst, src_row, dst_row, sem):
    return pltpu.make_async_copy(src.at[pl.ds(src_row, 1), :], dst.at[pl.ds(dst_row, 1), :], sem)


def _wait_rows_in(hbm, buf, sem):
    pltpu.make_async_copy(hbm.at[pl.ds(0, buf.shape[0]), :], buf, sem).wait()


def _wait_rows_out(buf, hbm, sem):
    pltpu.make_async_copy(buf, hbm.at[pl.ds(0, buf.shape[0]), :], sem).wait()


def _ffn_kernel(idx_ref, hf, wg_ref, wu_ref, wd_ref, gate_ref, gf_ref, o_ref,
                xe_scr, h_scr, stage, sems, *, n_ff_tiles, n_experts, n_steps, cap):
    e = pl.program_id(0)
    s = pl.program_id(1)
    rows = cap // n_steps
    tf = wg_ref.shape[2]

    def issue(expert, chunk, slot):
        base = expert * cap + chunk * rows
        for j in range(rows):
            _row_copy(hf, stage.at[slot], idx_ref[base + j], j, sems.at[slot]).start()

    def land(buf, chunk, slot):
        _wait_rows_in(hf, stage.at[slot], sems.at[slot])
        first = chunk * rows
        r = pl.ds(first if isinstance(first, int) else pl.multiple_of(first, rows), rows)
        xe_scr[buf, r, :] = stage[slot].astype(BF16)

    @pl.when(jnp.logical_and(e == 0, s == 0))
    def _():
        issue(0, 0, 0)
        for k in range(n_steps - 1):
            issue(0, k + 1, (k + 1) % STAGE_SLOTS)
            land(0, k, k % STAGE_SLOTS)
        issue(1 % n_experts, 0, 0)

    def step_io():
        due = (s + n_steps - 1) % n_steps
        owner = e + jnp.where(s > 0, 1, 0)
        land(owner % 2, due, due % STAGE_SLOTS)
        ahead = (s + 1) % n_steps
        target = e + 1 + jnp.where(s == n_steps - 1, 1, 0)
        issue(target % n_experts, ahead, ahead % STAGE_SLOTS)

    @pl.when(s < n_ff_tiles)
    def _():
        step_io()
        xe = xe_scr[e % 2]
        a = jnp.dot(xe, wg_ref[0].astype(BF16), preferred_element_type=F32)
        u = jnp.dot(xe, wu_ref[0].astype(BF16), preferred_element_type=F32)
        h_scr[:, pl.ds(pl.multiple_of(s * tf, tf), tf)] = (jax.nn.silu(a) * u).astype(BF16)

    @pl.when(s >= n_ff_tiles)
    def _():
        step_io()
        y = jnp.dot(h_scr[...], wd_ref[0].astype(BF16), preferred_element_type=F32)
        o_ref[0] = (y * gate_ref[0, :, 0:1] * gf_ref[...]).astype(BF16)

    @pl.when(jnp.logical_and(e == n_experts - 1, s == n_steps - 1))
    def _():
        for chunk in (n_steps - 1, 0):
            _wait_rows_in(hf, stage.at[chunk % STAGE_SLOTS], sems.at[chunk % STAGE_SLOTS])


def _expert_ffn(idx_flat, hf_rows, w_gate, w_up, w_down, gates, gate_f, cap, tf, tn):
    n_e, d, ff = w_gate.shape
    n_ff = ff // tf
    n_out = d // tn
    n_steps = n_ff + n_out
    assert cap % n_steps == 0 and ff % tf == 0 and d % tn == 0 and n_steps % STAGE_SLOTS == 0
    rows = cap // n_steps
    col_a = lambda e, s, idx: (e, 0, jnp.minimum(s, n_ff - 1))
    col_b = lambda e, s, idx: (e, 0, jnp.maximum(s - n_ff, 0))
    return pl.pallas_call(
        functools.partial(_ffn_kernel, n_ff_tiles=n_ff, n_experts=n_e, n_steps=n_steps, cap=cap),
        out_shape=jax.ShapeDtypeStruct((n_e, cap, d), BF16),
        grid_spec=pltpu.PrefetchScalarGridSpec(
            num_scalar_prefetch=1, grid=(n_e, n_steps),
            in_specs=[pl.BlockSpec(memory_space=pl.ANY),
                      pl.BlockSpec((1, d, tf), col_a),
                      pl.BlockSpec((1, d, tf), col_a),
                      pl.BlockSpec((1, ff, tn), col_b),
                      pl.BlockSpec((1, cap, V7X_LANES), lambda e, s, idx: (e, 0, 0), pipeline_mode=pl.Buffered(1)),
                      pl.BlockSpec((1, tn), lambda e, s, idx: (0, jnp.maximum(s - n_ff, 0)))],
            out_specs=pl.BlockSpec((1, cap, tn), col_b),
            scratch_shapes=[pltpu.VMEM((2, cap, d), BF16), pltpu.VMEM((cap, ff), BF16),
                            pltpu.VMEM((STAGE_SLOTS, rows, d), F32),
                            pltpu.SemaphoreType.DMA((STAGE_SLOTS,))]),
        compiler_params=_cparams(("arbitrary", "arbitrary"), 60),
        name="expert_ffn",
    )(idx_flat, hf_rows, w_gate, w_up, w_down, gates, gate_f)


def _combine_kernel(idx_ref, ye_ref, acc_in, acc_out, buf, gsem, ssem, *, cap, n_tiles):
    e = pl.program_id(0)
    m = pl.program_id(1)
    n_buf, n_grp, grp, _ = buf.shape
    tm = n_grp * grp

    def for_rows(tile, fn):
        def body(g, carry):
            first = e * cap + tile * tm + g * grp
            for u in range(grp):
                fn(g, u, idx_ref[first + u])
            return carry
        lax.fori_loop(0, n_grp, body, 0)

    def fetch(tile, b):
        for_rows(tile, lambda g, u, tok: _row_copy(acc_in, buf.at[b, g], tok, u, gsem.at[b]).start())

    def put(tile, b):
        for_rows(tile, lambda g, u, tok: _row_copy(buf.at[b, g], acc_out, u, tok, ssem.at[b]).start())

    def fetch_wait(b):
        for g in range(n_grp):
            _wait_rows_in(acc_in, buf.at[b, g], gsem.at[b])

    def put_wait(b):
        for g in range(n_grp):
            _wait_rows_out(buf.at[b, g], acc_out, ssem.at[b])

    for t in range(n_tiles):
        @pl.when(m == t)
        def _(t=t):
            b = t % n_buf
            if t == 0:
                fetch(0, 0)
                if n_tiles > 1:
                    fetch(1, 1 % n_buf)
            elif t + 1 < n_tiles:
                nb = (t + 1) % n_buf
                if t + 1 >= n_buf:
                    put_wait(nb)
                fetch(t + 1, nb)
            fetch_wait(b)
            for g in range(n_grp):
                buf[b, g] = buf[b, g] + ye_ref[0, g * grp:(g + 1) * grp, :].astype(F32)
            put(t, b)
            if t == n_tiles - 1:
                for u in range(max(0, n_tiles - n_buf), n_tiles):
                    put_wait(u % n_buf)


def _combine(idx_flat, ye, acc, tm, n_buf):
    n_e, cap, d = ye.shape
    n_tiles = cap // tm
    return pl.pallas_call(
        functools.partial(_combine_kernel, cap=cap, n_tiles=n_tiles),
        out_shape=jax.ShapeDtypeStruct(acc.shape, acc.dtype),
        grid_spec=pltpu.PrefetchScalarGridSpec(
            num_scalar_prefetch=1, grid=(n_e, n_tiles),
            in_specs=[pl.BlockSpec((1, tm, d), lambda e, m, idx: (e, m, 0)),
                      pl.BlockSpec(memory_space=pl.ANY)],
            out_specs=pl.BlockSpec(memory_space=pl.ANY),
            scratch_shapes=[pltpu.VMEM((n_buf, tm // DMA_GROUP, DMA_GROUP, d), F32),
                            pltpu.SemaphoreType.DMA((n_buf,)), pltpu.SemaphoreType.DMA((n_buf,))]),
        input_output_aliases={2: 0},
        compiler_params=_cparams(("arbitrary", "arbitrary"), 40),
        name="combine",
    )(idx_flat, ye, acc)


def _final_ln_kernel(a_ref, g_ref, b_ref, o_ref):
    def body(r, carry):
        sl = pl.ds(pl.multiple_of(r * ROW_STRIP, ROW_STRIP), ROW_STRIP)
        o_ref[sl, :] = _ln_rows(a_ref[sl, :]) * g_ref[...] + b_ref[...]
        return carry
    lax.fori_loop(0, a_ref.shape[0] // ROW_STRIP, body, 0)


def _final_ln(acc, gain, bias, tm):
    l, d = acc.shape
    return pl.pallas_call(
        _final_ln_kernel,
        out_shape=jax.ShapeDtypeStruct((l, d), F32),
        grid=(l // tm,),
        in_specs=[pl.BlockSpec((tm, d), lambda i: (i, 0)),
                  pl.BlockSpec((1, d), lambda i: (0, 0)),
                  pl.BlockSpec((1, d), lambda i: (0, 0))],
        out_specs=pl.BlockSpec((tm, d), lambda i: (i, 0)),
        compiler_params=_cparams(("arbitrary",), 32),
        name="final_ln",
    )(acc, gain, bias)


def _pad_rows(a, rows):
    return jnp.pad(a, ((0, rows - a.shape[0]), (0, 0)))


def _pad_lanes(a, lanes):
    return jnp.pad(a, ((0, 0), (0, lanes - a.shape[1])))


def kernel(x, c, ctx, c_ctx, w_ada, b_ada, w_in, gm_v_gain, gm_v_bias, gm_w_s, gm_b_s, ssm_conv_w, ssm_conv_b, ssm_dt_bias, ssm_a_log, ssm_d, ssm_norm_gain, w_out, ln_mix_gain, ln_mix_bias, w_router, w_gate, w_up, w_down, ln_ffn_gain, ln_ffn_bias):
    b, l, d = x.shape
    lc = ctx.shape[1]
    assert b == 1 and d == D_MODEL and w_ada.shape[0] == DEPTH
    x2 = x[0]
    ctx2 = ctx[0]
    layer = 0

    cond = _pad_rows(jnp.concatenate([c, c_ctx[None, :]], axis=0), V7X_SUBLANES)
    ada = _ada_params(cond, w_ada[layer], b_ada[layer][None, :])
    mods = jnp.pad(ada[:2].reshape(2, 6, d), ((0, 0), (0, V7X_SUBLANES - 6), (0, 0)))

    w_in_t = jnp.swapaxes(w_in, 1, 2)
    wdt = _pad_rows(w_in_t[layer, COL_DT:, :], V7X_LANES).astype(BF16)
    dt_bias = _pad_lanes(ssm_dt_bias[layer].reshape(1, 2 * SSM_HEADS), V7X_LANES)
    a_log = _pad_lanes(ssm_a_log[layer].reshape(1, 2 * SSM_HEADS), V7X_LANES)
    h, dtda = _modulate_in(x2, ctx2, mods, wdt, dt_bias, a_log)
    proj = _matmul_nt(h, w_in_t, COL_DT, tm=1408, tn=512, name="in_proj")

    conv_w8 = _pad_rows(ssm_conv_w[layer], V7X_SUBLANES)
    conv = _conv_silu(proj, conv_w8, ssm_conv_b[layer][None, :], rb=lc, tc=2048)

    n_lat_chunks = l // CHUNK
    n_ctx_chunks = lc // SSD_CHUNK
    gm = _chunk_mlp(proj, lc, n_lat_chunks, gm_v_gain[layer][None, :], gm_v_bias[layer][None, :],
                    gm_w_s[layer].astype(BF16), gm_b_s[layer].T)

    y_fwd = _ssd_forward(conv, dtda, n_ctx_chunks, n_lat_chunks)
    d_skip = jnp.repeat(ssm_d[layer], SSM_HEAD_DIM)[None, :]
    ss = _ssd_backward(conv, dtda, y_fwd, proj, d_skip, ssm_norm_gain[layer][None, :],
                       n_ctx_chunks, n_lat_chunks)

    mix = _matmul_concat(gm, ss, w_out[layer], tm=1024, tn=512, name="out_proj")

    hf_rows, acc, aff_t = _post_mix(x2, mix, mods, ln_mix_gain[layer][None, :], ln_mix_bias[layer][None, :],
                                      w_router[layer].T.astype(BF16), tm=256)

    cap = EC_FACTOR * l // N_EXPERTS
    idx3, gates = _expert_choice_topk(aff_t, cap)
    idx_flat = idx3[:, :, 0].reshape(N_EXPERTS * cap)

    gate_f = mods[0, 5:6, :]
    ye = _expert_ffn(idx_flat, hf_rows, w_gate[layer], w_up[layer], w_down[layer], gates, gate_f,
                     cap, tf=256, tn=512)
    acc = _combine(idx_flat, ye, acc, tm=512, n_buf=2)
    out = _final_ln(acc, ln_ffn_gain[layer][None, :], ln_ffn_bias[layer][None, :], tm=256)
    return out[None]
```
